```python
import jax, jax.numpy as jnp
from jax import lax
import numpy as np

D_MODEL = 4096
BATCH = 2
SEQ = 4096
DEPTH = 1
DEC_BATCH = 2
DEC_SEQ = 8192
PAST_LEN = 128

HEAD_DIM = 128
ROPE_THETA = 10000.0
NORM_EPS = 1e-6
NEG_INF = -1e30

DIL_CONFIGS = ((128, 1), (512, 4), (2048, 16))
N_DIL_GROUPS = 3
A_HEADS = 8
A_WIDTH = A_HEADS * HEAD_DIM
A_QKV_WIDTH = N_DIL_GROUPS * A_WIDTH

B_Q_HEADS = 16
B_KV_HEADS = 4
B_RADIUS = 128
B_BLOCK = 128
B_Q_WIDTH = B_Q_HEADS * HEAD_DIM
B_KV_WIDTH = B_KV_HEADS * HEAD_DIM

MEM_LEN = 256
M_HEADS = 4
M_HEAD_DIM = 256
M_WIDTH = M_HEADS * M_HEAD_DIM

MIX_WIDTH = A_WIDTH + B_Q_WIDTH + M_WIDTH
N_BRANCH = 3
IN_WIDTH = 3 * A_QKV_WIDTH + B_Q_WIDTH + 2 * B_KV_WIDTH + M_WIDTH + MIX_WIDTH + N_BRANCH * D_MODEL

kernel_name = "hybrid_dilated_window_memory_encoder"


def _col_slices():
    sizes = (("a_qkv", 3 * A_QKV_WIDTH), ("b_q", B_Q_WIDTH), ("b_kv", 2 * B_KV_WIDTH), ("m_q", M_WIDTH),
             ("z_a", A_WIDTH), ("z_b", B_Q_WIDTH), ("z_m", M_WIDTH),
             ("g_a", D_MODEL), ("g_b", D_MODEL), ("g_m", D_MODEL))
    out = {}
    lo = 0
    for name, n in sizes:
        out[name] = (lo, lo + n)
        lo += n
    return out


def _branch_rows():
    return {"a": (0, A_WIDTH), "b": (A_WIDTH, A_WIDTH + B_Q_WIDTH), "m": (A_WIDTH + B_Q_WIDTH, MIX_WIDTH)}


def _proj(h, w_in, name):
    lo, hi = _col_slices()[name]
    return jnp.einsum('bsd,de->bse', h, w_in[:, lo:hi])


def _rmsnorm(x, g):
    x32 = x.astype(jnp.float32)
    y = x32 * lax.rsqrt(jnp.mean(x32 * x32, axis=-1, keepdims=True) + NORM_EPS)
    return y.astype(x.dtype) * g


def _rope(x):
    s, dh = x.shape[1], x.shape[-1]
    inv_freq = ROPE_THETA ** (-jnp.arange(0, dh, 2, dtype=jnp.float32) / dh)
    ang = jnp.arange(s, dtype=jnp.float32)[:, None] * inv_freq[None, :]
    shp = (s,) + (1,) * (x.ndim - 3) + (dh // 2,)
    cos = jnp.cos(ang).reshape(shp).astype(x.dtype)
    sin = jnp.sin(ang).reshape(shp).astype(x.dtype)
    x1, x2 = x[..., : dh // 2], x[..., dh // 2:]
    return jnp.concatenate([x1 * cos - x2 * sin, x2 * cos + x1 * sin], axis=-1)


def _banded_attention(q, k, v, radius, blk, sink=None):
    n, L, H, dh = q.shape
    hkv = k.shape[2]
    g = H // hkv
    nb = -(-L // blk)
    lp = nb * blk
    pad = lp - L
    qb = jnp.pad(q, ((0, 0), (0, pad), (0, 0), (0, 0))).reshape(n, nb, blk, hkv, g, dh)

    def windows(t):
        tp = jnp.pad(t, ((0, 0), (blk, pad + blk), (0, 0), (0, 0))).reshape(n, nb + 2, blk, hkv, dh)
        return jnp.concatenate([tp[:, :-2], tp[:, 1:-1], tp[:, 2:]], axis=2)

    kw, vw = windows(k), windows(v)
    s = jnp.einsum('nbqhgd,nbkhd->nbhgqk', qb, kw, preferred_element_type=jnp.float32) * (dh ** -0.5)
    qpos = jnp.arange(nb)[:, None] * blk + jnp.arange(blk)[None, :]
    kpos = jnp.arange(nb)[:, None] * blk - blk + jnp.arange(3 * blk)[None, :]
    valid = ((jnp.abs(qpos[:, :, None] - kpos[:, None, :]) <= radius)
             & (kpos[:, None, :] >= 0) & (kpos[:, None, :] < L))
    s = jnp.where(valid[None, :, None, None], s, NEG_INF)
    m = jnp.max(s, axis=-1, keepdims=True)
    if sink is not None:
        sk = sink.astype(jnp.float32).reshape(1, 1, hkv, g, 1, 1)
        m = jnp.maximum(m, sk)
    e = jnp.exp(s - m)
    den = jnp.sum(e, axis=-1, keepdims=True)
    if sink is not None:
        den = den + jnp.exp(sk - m)
    lse = (m + jnp.log(den))[..., 0]
    p = (e / den).astype(v.dtype)
    o = jnp.einsum('nbhgqk,nbkhd->nbqhgd', p, vw).reshape(n, lp, H, dh)[:, :L]
    lse = lse.transpose(0, 1, 4, 2, 3).reshape(n, lp, H)[:, :L]
    return o, lse


def _dilated_group(q, k, v, window, dilation):
    b, s, h, dh = q.shape
    radius = window // (2 * dilation)
    sd = s // dilation

    def split(t):
        return t.reshape(b, sd, dilation, h, dh).transpose(0, 2, 1, 3, 4).reshape(b * dilation, sd, h, dh)

    o, lse = _banded_attention(split(q), split(k), split(v), radius, radius)
    o = o.reshape(b, dilation, sd, h, dh).transpose(0, 2, 1, 3, 4).reshape(b, s, h, dh)
    lse = lse.reshape(b, dilation, sd, h).transpose(0, 2, 1, 3).reshape(b, s, h)
    return o, lse


def _mixer_a(h, w_in):
    b, s, _ = h.shape
    qkv = _proj(h, w_in, "a_qkv").reshape(b, s, 3, N_DIL_GROUPS, A_HEADS, HEAD_DIM)
    q, k, v = _rope(qkv[:, :, 0]), _rope(qkv[:, :, 1]), qkv[:, :, 2]
    outs, lses = [], []
    for gi, (window, dilation) in enumerate(DIL_CONFIGS):
        o, lse = _dilated_group(q[:, :, gi], k[:, :, gi], v[:, :, gi], window, dilation)
        outs.append(o)
        lses.append(lse)
    alpha = jax.nn.softmax(jnp.stack(lses, axis=0), axis=0).astype(h.dtype)
    o = alpha[0][..., None] * outs[0] + alpha[1][..., None] * outs[1] + alpha[2][..., None] * outs[2]
    return o.reshape(b, s, A_WIDTH)


def _mixer_b(h, w_in, sink):
    b, s, _ = h.shape
    q = _rope(_proj(h, w_in, "b_q").reshape(b, s, B_Q_HEADS, HEAD_DIM))
    kv = _proj(h, w_in, "b_kv").reshape(b, s, 2, B_KV_HEADS, HEAD_DIM)
    k, v = _rope(kv[:, :, 0]), kv[:, :, 1]
    o, _ = _banded_attention(q, k, v, B_RADIUS, B_BLOCK, sink)
    return o.reshape(b, s, B_Q_WIDTH)


def _memory_attn(h, mem, w_in, g_mem, w_mem_kv):
    b, s, _ = h.shape
    q = _proj(h, w_in, "m_q").reshape(b, s, M_HEADS, M_HEAD_DIM)
    kv = jnp.einsum('bmd,de->bme', _rmsnorm(mem, g_mem), w_mem_kv).reshape(b, MEM_LEN, 2, M_HEADS, M_HEAD_DIM)
    sc = jnp.einsum('bshd,bmhd->bhsm', q, kv[:, :, 0], preferred_element_type=jnp.float32) * (M_HEAD_DIM ** -0.5)
    p = jax.nn.softmax(sc, axis=-1).astype(h.dtype)
    o = jnp.einsum('bhsm,bmhd->bshd', p, kv[:, :, 1])
    return o.reshape(b, s, M_WIDTH)


def _layer(x, mem, g_norm, w_in, sink, g_mem, w_mem_kv, w_branch, w_out):
    h = _rmsnorm(x, g_norm)
    branch_out = {"a": _mixer_a(h, w_in), "b": _mixer_b(h, w_in, sink),
                  "m": _memory_attn(h, mem, w_in, g_mem, w_mem_kv)}
    rows = _branch_rows()
    u = None
    for name in ("a", "b", "m"):
        z = jax.nn.silu(_proj(h, w_in, "z_" + name))
        gate = jax.nn.sigmoid(_proj(h, w_in, "g_" + name))
        lo, hi = rows[name]
        term = gate * jnp.einsum('bse,ed->bsd', branch_out[name] * z, w_branch[lo:hi])
        u = term if u is None else u + term
    return x + jnp.einsum('bsd,de->bse', u, w_out)


def _trunk(x, mem, g_norm, w_in, attn_sink, g_mem, w_mem_kv, w_branch, w_out, g_final):
    for l in range(DEPTH):
        x = _layer(x, mem, g_norm[l], w_in[l], attn_sink[l], g_mem[l], w_mem_kv[l], w_branch[l], w_out[l])
    return _rmsnorm(x, g_final)


def setup_inputs(seed: int = 0) -> dict:
    key = jax.random.key(seed)
    ks = jax.random.split(key, 12)
    f32 = jnp.float32
    return {
        "x_prompt": jax.random.normal(ks[0], (BATCH, SEQ, D_MODEL), f32),
        "x_sample": jax.random.normal(ks[1], (DEC_BATCH, DEC_SEQ, D_MODEL), f32),
        "mem_prompt": jax.random.normal(ks[2], (BATCH, MEM_LEN, D_MODEL), f32),
        "mem_sample": jax.random.normal(ks[3], (DEC_BATCH, MEM_LEN, D_MODEL), f32),
        "g_norm": 1.0 + 0.02 * jax.random.normal(ks[4], (DEPTH, D_MODEL), f32),
        "w_in": jax.random.normal(ks[5], (DEPTH, D_MODEL, IN_WIDTH), f32) * (D_MODEL ** -0.5),
        "attn_sink": 0.5 * jax.random.normal(ks[6], (DEPTH, B_Q_HEADS), f32),
        "g_mem": 1.0 + 0.02 * jax.random.normal(ks[7], (DEPTH, D_MODEL), f32),
        "w_mem_kv": jax.random.normal(ks[8], (DEPTH, D_MODEL, 2 * M_WIDTH), f32) * (D_MODEL ** -0.5),
        "w_branch": jax.random.normal(ks[9], (DEPTH, MIX_WIDTH, D_MODEL), f32) * (MIX_WIDTH ** -0.5),
        "w_out": jax.random.normal(ks[10], (DEPTH, D_MODEL, D_MODEL), f32) * (D_MODEL ** -0.5),
        "g_final": 1.0 + 0.02 * jax.random.normal(ks[11], (D_MODEL,), f32),
    }


def reference(x_prompt, x_sample, mem_prompt, mem_sample, g_norm, w_in, attn_sink, g_mem, w_mem_kv, w_branch, w_out, g_final):
    y_prompt = _trunk(x_prompt, mem_prompt, g_norm, w_in, attn_sink, g_mem, w_mem_kv, w_branch, w_out, g_final)
    y_sample = _trunk(x_sample, mem_sample, g_norm, w_in, attn_sink, g_mem, w_mem_kv, w_branch, w_out, g_final)
    return (y_prompt, y_sample)
```

```python
import functools

import jax
import jax.numpy as jnp
from jax import lax
from jax.experimental import pallas as pl
from jax.experimental.pallas import tpu as pltpu

D_MODEL = 4096
HEAD_DIM = 128
ROPE_THETA = 10000.0
NORM_EPS = 1e-6
NEG_INF = -1e30

DIL_CONFIGS = ((128, 1), (512, 4), (2048, 16))
A_HEADS = 8
A_WIDTH = A_HEADS * HEAD_DIM
A_RADIUS = 64

B_Q_HEADS = 16
B_KV_HEADS = 4
B_GROUP = B_Q_HEADS // B_KV_HEADS
B_RADIUS = 128

MEM_LEN = 256
M_HEADS = 4
M_HEAD_DIM = 256

TN = 1024
A_TILES = (0, 9)
B_TILES = (9, 13)
Z_TILES = (13, 17)
G_TILES = (17, 29)

VMEM_LIMIT = 56 * 1024 * 1024


def _params(sem, vmem=VMEM_LIMIT):
    return pltpu.CompilerParams(dimension_semantics=sem, vmem_limit_bytes=vmem)


def _rmsnorm_kernel(x_ref, g_ref, o_ref):
    x = x_ref[...]
    ms = jnp.mean(x * x, axis=-1, keepdims=True)
    o_ref[...] = ((x * lax.rsqrt(ms + NORM_EPS)) * g_ref[...]).astype(o_ref.dtype)


def _rmsnorm(x, g, out_dtype, tm=256):
    m, d = x.shape
    return pl.pallas_call(
        _rmsnorm_kernel,
        grid=(m // tm,),
        in_specs=[pl.BlockSpec((tm, d), lambda i: (i, 0)),
                  pl.BlockSpec((1, d), lambda i: (0, 0))],
        out_specs=pl.BlockSpec((tm, d), lambda i: (i, 0)),
        out_shape=jax.ShapeDtypeStruct((m, d), out_dtype),
        compiler_params=_params(("parallel",)),
        name="rmsnorm",
    )(x, g.reshape(1, d))


def _rope(x, cos, sin_signed):
    return x * cos + pltpu.roll(x, HEAD_DIM // 2, 1) * sin_signed


def _proj_kernel(*refs, epis):
    has_rope = any(e.startswith("rope") for e in epis)
    if has_rope:
        h_ref, w_ref, cos_ref, sin_ref, o_ref = refs
    else:
        h_ref, w_ref, o_ref = refs
    acc = jnp.dot(h_ref[...], w_ref[...], preferred_element_type=jnp.float32)

    def emit(kind):
        if kind == "none":
            o_ref[...] = acc.astype(o_ref.dtype)
        elif kind == "silu":
            o_ref[...] = (acc * jax.nn.sigmoid(acc)).astype(o_ref.dtype)
        elif kind == "sigmoid":
            o_ref[...] = jax.nn.sigmoid(acc).astype(o_ref.dtype)
        else:
            n_rope = TN // HEAD_DIM if kind == "rope" else TN // HEAD_DIM // 2
            cos = cos_ref[...]
            sin = sin_ref[...]
            for hh in range(TN // HEAD_DIM):
                sl = slice(hh * HEAD_DIM, (hh + 1) * HEAD_DIM)
                x = acc[:, sl]
                if hh < n_rope:
                    x = _rope(x, cos, sin)
                o_ref[:, sl] = x.astype(o_ref.dtype)

    kinds = sorted(set(epis))
    if len(kinds) == 1:
        emit(kinds[0])
    else:
        j = pl.program_id(1)
        for kind in kinds:
            cond = functools.reduce(
                jnp.logical_or, [j == t for t, e in enumerate(epis) if e == kind])
            pl.when(cond)(functools.partial(emit, kind))


def _proj(h, w, tiles, epis, *, tm, seq=None, rope=None, name):
    m, k = h.shape
    lo, hi = tiles
    nt = hi - lo
    assert len(epis) == nt
    in_specs = [pl.BlockSpec((tm, k), lambda i, j: (i, 0)),
                pl.BlockSpec((k, TN), lambda i, j: (0, lo + j))]
    args = [h, w]
    if rope is not None:
        blocks_per_seq = seq // tm
        tab_spec = pl.BlockSpec((tm, HEAD_DIM), lambda i, j: (i % blocks_per_seq, 0))
        in_specs += [tab_spec, tab_spec]
        args += list(rope)
    return pl.pallas_call(
        functools.partial(_proj_kernel, epis=tuple(epis)),
        grid=(m // tm, nt),
        in_specs=in_specs,
        out_specs=pl.BlockSpec((None, tm, TN), lambda i, j: (j, i, 0)),
        out_shape=jax.ShapeDtypeStruct((nt, m, TN), jnp.bfloat16),
        compiler_params=_params(("parallel", "arbitrary")),
        name=name,
    )(*args)


A_TQ = 128
A_HALO = 64


def _attn_a_kernel(q_ref, kp_ref, kc_ref, kn_ref, vp_ref, vc_ref, vn_ref, o_ref, lse_ref, *, seq):
    start = pl.program_id(1) * A_TQ
    nk = A_TQ + 2 * A_HALO
    r = lax.broadcasted_iota(jnp.int32, (A_TQ, nk), 0)
    c = lax.broadcasted_iota(jnp.int32, (A_TQ, nk), 1)
    kpos = start - A_HALO + c
    rel = c - A_HALO - r
    valid = (jnp.abs(rel) <= A_RADIUS) & (kpos >= 0) & (kpos < seq)
    lane = lax.broadcasted_iota(jnp.int32, (A_TQ, A_HEADS), 1)
    lse_all = jnp.zeros((A_TQ, A_HEADS), jnp.float32)
    scale = HEAD_DIM ** -0.5
    for hh in range(A_HEADS):
        sl = slice(hh * HEAD_DIM, (hh + 1) * HEAD_DIM)
        k = jnp.concatenate([kp_ref[:, sl], kc_ref[:, sl], kn_ref[:, sl]], axis=0)
        v = jnp.concatenate([vp_ref[:, sl], vc_ref[:, sl], vn_ref[:, sl]], axis=0)
        s = lax.dot_general(q_ref[:, sl], k, (((1,), (1,)), ((), ())),
                            preferred_element_type=jnp.float32) * scale
        s = jnp.where(valid, s, NEG_INF)
        m = jnp.max(s, axis=1, keepdims=True)
        e = jnp.exp(s - m)
        den = jnp.sum(e, axis=1, keepdims=True)
        o = jnp.dot(e.astype(jnp.bfloat16), v, preferred_element_type=jnp.float32)
        o_ref[:, sl] = o * (1.0 / den)
        lse_all = jnp.where(lane == hh, m + jnp.log(den), lse_all)
    lse_ref[...] = lse_all


def _attn_a(qkv, n, seq, offs):
    qo, ko, vo = offs
    nq = seq // A_TQ
    nh = seq // A_HALO
    ratio = A_TQ // A_HALO
    cur = lambda off: pl.BlockSpec((None, A_TQ, A_WIDTH), lambda s, i: (off + s, i, 0))
    prev = lambda off: pl.BlockSpec(
        (None, A_HALO, A_WIDTH), lambda s, i: (off + s, jnp.maximum(ratio * i - 1, 0), 0))
    nxt = lambda off: pl.BlockSpec(
        (None, A_HALO, A_WIDTH), lambda s, i: (off + s, jnp.minimum(ratio * (i + 1), nh - 1), 0))
    return pl.pallas_call(
        functools.partial(_attn_a_kernel, seq=seq),
        grid=(n, nq),
        in_specs=[cur(qo), prev(ko), cur(ko), nxt(ko), prev(vo), cur(vo), nxt(vo)],
        out_specs=[pl.BlockSpec((None, A_TQ, A_WIDTH), lambda s, i: (s, i, 0)),
                   pl.BlockSpec((None, A_TQ, A_HEADS), lambda s, i: (s, i, 0))],
        out_shape=[jax.ShapeDtypeStruct((n, seq, A_WIDTH), jnp.float32),
                   jax.ShapeDtypeStruct((n, seq, A_HEADS), jnp.float32)],
        compiler_params=_params(("parallel", "parallel")),
        name="attn_a",
    )(qkv, qkv, qkv, qkv, qkv, qkv, qkv)


def _merge_a_kernel(o0_ref, o1_ref, o2_ref, l0_ref, l1_ref, l2_ref, z_ref, u_ref):
    l0, l1, l2 = l0_ref[...], l1_ref[...], l2_ref[...]
    m = jnp.maximum(jnp.maximum(l0, l1), l2)
    e0, e1, e2 = jnp.exp(l0 - m), jnp.exp(l1 - m), jnp.exp(l2 - m)
    inv = 1.0 / (e0 + e1 + e2)
    a0, a1, a2 = e0 * inv, e1 * inv, e2 * inv
    for hh in range(A_HEADS):
        sl = slice(hh * HEAD_DIM, (hh + 1) * HEAD_DIM)
        col = slice(hh, hh + 1)
        o = a0[:, col] * o0_ref[:, sl] + a1[:, col] * o1_ref[:, sl] + a2[:, col] * o2_ref[:, sl]
        u_ref[:, sl] = (o * z_ref[:, sl].astype(jnp.float32)).astype(u_ref.dtype)


def _merge_a(outs, lses, pz, tm=512):
    m = outs[0].shape[0]
    o_spec = pl.BlockSpec((tm, A_WIDTH), lambda i: (i, 0))
    l_spec = pl.BlockSpec((tm, A_HEADS), lambda i: (i, 0))
    return pl.pallas_call(
        _merge_a_kernel,
        grid=(m // tm,),
        in_specs=[o_spec] * 3 + [l_spec] * 3 + [pl.BlockSpec((None, tm, TN), lambda i: (0, i, 0))],
        out_specs=pl.BlockSpec((tm, A_WIDTH), lambda i: (i, 0)),
        out_shape=jax.ShapeDtypeStruct((m, A_WIDTH), jnp.bfloat16),
        compiler_params=_params(("parallel",)),
        name="merge_a",
    )(*outs, *lses, pz)


B_TQ = 128


def _attn_b_kernel(sink_ref, q0_ref, q1_ref, kvp_ref, kvc_ref, kvn_ref, z0_ref, z1_ref, u_ref, *, seq):
    start = pl.program_id(1) * B_TQ
    rows = B_GROUP * B_TQ
    nk = 3 * B_TQ
    r = lax.broadcasted_iota(jnp.int32, (rows, nk), 0) & (B_TQ - 1)
    c = lax.broadcasted_iota(jnp.int32, (rows, nk), 1)
    kpos = start - B_TQ + c
    rel = c - B_TQ - r
    valid = (jnp.abs(rel) <= B_RADIUS) & (kpos >= 0) & (kpos < seq)
    grp = lax.broadcasted_iota(jnp.int32, (rows, 1), 0) // B_TQ
    scale = HEAD_DIM ** -0.5
    kv_half = B_KV_HEADS * HEAD_DIM
    for kh in range(B_KV_HEADS):
        q_ref = q0_ref if kh < 2 else q1_ref
        z_ref = z0_ref if kh < 2 else z1_ref
        base = (kh % 2) * B_GROUP * HEAD_DIM
        ksl = slice(kh * HEAD_DIM, (kh + 1) * HEAD_DIM)
        vsl = slice(kv_half + kh * HEAD_DIM, kv_half + (kh + 1) * HEAD_DIM)
        q = jnp.concatenate(
            [q_ref[:, base + g * HEAD_DIM: base + (g + 1) * HEAD_DIM] for g in range(B_GROUP)], axis=0)
        k = jnp.concatenate([kvp_ref[:, ksl], kvc_ref[:, ksl], kvn_ref[:, ksl]], axis=0)
        v = jnp.concatenate([kvp_ref[:, vsl], kvc_ref[:, vsl], kvn_ref[:, vsl]], axis=0)
        s = lax.dot_general(q, k, (((1,), (1,)), ((), ())),
                            preferred_element_type=jnp.float32) * scale
        s = jnp.where(valid, s, NEG_INF)
        sk = jnp.full((rows, 1), sink_ref[kh * B_GROUP], jnp.float32)
        for g in range(1, B_GROUP):
            sk = jnp.where(grp == g, sink_ref[kh * B_GROUP + g], sk)
        m = jnp.maximum(jnp.max(s, axis=1, keepdims=True), sk)
        e = jnp.exp(s - m)
        den = jnp.sum(e, axis=1, keepdims=True) + jnp.exp(sk - m)
        o = jnp.dot(e.astype(jnp.bfloat16), v, preferred_element_type=jnp.float32) * (1.0 / den)
        for g in range(B_GROUP):
            zsl = slice(base + g * HEAD_DIM, base + (g + 1) * HEAD_DIM)
            col = (kh * B_GROUP + g) * HEAD_DIM
            u_ref[:, col:col + HEAD_DIM] = (
                o[g * B_TQ:(g + 1) * B_TQ] * z_ref[:, zsl].astype(jnp.float32)).astype(u_ref.dtype)


def _attn_b(sink, pb, pz, batch, seq):
    nq = seq // B_TQ
    row = lambda slot: pl.BlockSpec((None, B_TQ, TN), lambda b, i: (slot, b * nq + i, 0))
    kv_prev = pl.BlockSpec((None, B_TQ, TN), lambda b, i: (2, b * nq + jnp.maximum(i - 1, 0), 0))
    kv_next = pl.BlockSpec((None, B_TQ, TN), lambda b, i: (2, b * nq + jnp.minimum(i + 1, nq - 1), 0))
    width = B_Q_HEADS * HEAD_DIM
    return pl.pallas_call(
        functools.partial(_attn_b_kernel, seq=seq),
        grid=(batch, nq),
        in_specs=[pl.BlockSpec(memory_space=pltpu.SMEM),
                  row(0), row(1), kv_prev, row(2), kv_next, row(1), row(2)],
        out_specs=pl.BlockSpec((B_TQ, width), lambda b, i: (b * nq + i, 0)),
        out_shape=jax.ShapeDtypeStruct((batch * seq, width), jnp.bfloat16),
        compiler_params=_params(("parallel", "parallel")),
        name="attn_b",
    )(sink, pb, pb, pb, pb, pb, pz, pz)


M_TQ = 256


def _attn_m_kernel(q_ref, k_ref, v_ref, z_ref, u_ref):
    scale = M_HEAD_DIM ** -0.5
    for hh in range(M_HEADS):
        sl = slice(hh * M_HEAD_DIM, (hh + 1) * M_HEAD_DIM)
        s = lax.dot_general(q_ref[:, sl], k_ref[:, sl], (((1,), (1,)), ((), ())),
                            preferred_element_type=jnp.float32) * scale
        m = jnp.max(s, axis=1, keepdims=True)
        e = jnp.exp(s - m)
        den = jnp.sum(e, axis=1, keepdims=True)
        o = jnp.dot(e.astype(jnp.bfloat16), v_ref[:, sl], preferred_element_type=jnp.float32) * (1.0 / den)
        u_ref[:, sl] = (o * z_ref[:, sl].astype(jnp.float32)).astype(u_ref.dtype)


def _attn_m(pb, kv_mem, pz, batch, seq):
    nq = seq // M_TQ
    width = M_HEADS * M_HEAD_DIM
    return pl.pallas_call(
        _attn_m_kernel,
        grid=(batch, nq),
        in_specs=[pl.BlockSpec((None, M_TQ, TN), lambda b, i: (3, b * nq + i, 0)),
                  pl.BlockSpec((None, MEM_LEN, TN), lambda b, i: (0, b, 0)),
                  pl.BlockSpec((None, MEM_LEN, TN), lambda b, i: (1, b, 0)),
                  pl.BlockSpec((None, M_TQ, TN), lambda b, i: (3, b * nq + i, 0))],
        out_specs=pl.BlockSpec((M_TQ, width), lambda b, i: (b * nq + i, 0)),
        out_shape=jax.ShapeDtypeStruct((batch * seq, width), jnp.bfloat16),
        compiler_params=_params(("parallel", "parallel")),
        name="attn_m",
    )(pb, kv_mem, kv_mem, pz)


def _branch_kernel(ua_ref, ub_ref, um_ref, w_ref, ga_ref, gb_ref, gm_ref, u_ref):
    a_hi = A_WIDTH
    b_hi = A_WIDTH + B_Q_HEADS * HEAD_DIM
    dot = functools.partial(jnp.dot, preferred_element_type=jnp.float32)
    acc = ga_ref[...].astype(jnp.float32) * dot(ua_ref[...], w_ref[:a_hi, :])
    acc += gb_ref[...].astype(jnp.float32) * dot(ub_ref[...], w_ref[a_hi:b_hi, :])
    acc += gm_ref[...].astype(jnp.float32) * dot(um_ref[...], w_ref[b_hi:, :])
    u_ref[...] = acc.astype(u_ref.dtype)


def _branch(ua, ub, um, w_branch, pg, tm=512):
    m = ua.shape[0]
    k = w_branch.shape[0]
    n_tiles = D_MODEL // TN
    act = lambda width: pl.BlockSpec((tm, width), lambda i, j: (i, 0))
    gate = lambda br: pl.BlockSpec((None, tm, TN), lambda i, j: (br * n_tiles + j, i, 0))
    return pl.pallas_call(
        _branch_kernel,
        grid=(m // tm, n_tiles),
        in_specs=[act(ua.shape[1]), act(ub.shape[1]), act(um.shape[1]),
                  pl.BlockSpec((k, TN), lambda i, j: (0, j)),
                  gate(0), gate(1), gate(2)],
        out_specs=pl.BlockSpec((tm, TN), lambda i, j: (i, j)),
        out_shape=jax.ShapeDtypeStruct((m, D_MODEL), jnp.bfloat16),
        compiler_params=_params(("parallel", "arbitrary")),
        name="branch_proj",
    )(ua, ub, um, w_branch, pg, pg, pg)


def _out_kernel(u_ref, w_ref, x_ref, y_ref):
    y_ref[...] = x_ref[...] + jnp.dot(u_ref[...], w_ref[...], preferred_element_type=jnp.float32)


def _out_proj(u, w_out, x, tm=512):
    m, k = u.shape
    n = x.shape[1]
    tn = min(TN, n)
    return pl.pallas_call(
        _out_kernel,
        grid=(m // tm, n // tn),
        in_specs=[pl.BlockSpec((tm, k), lambda i, j: (i, 0)),
                  pl.BlockSpec((k, tn), lambda i, j: (0, j)),
                  pl.BlockSpec((tm, tn), lambda i, j: (i, j))],
        out_specs=pl.BlockSpec((tm, tn), lambda i, j: (i, j)),
        out_shape=jax.ShapeDtypeStruct((m, n), jnp.float32),
        compiler_params=_params(("parallel", "arbitrary")),
        name="out_proj",
    )(u, w_out, x)


def _rope_tables(seq):
    inv_freq = ROPE_THETA ** (-jnp.arange(0, HEAD_DIM, 2, dtype=jnp.float32) / HEAD_DIM)
    ang = jnp.arange(seq, dtype=jnp.float32)[:, None] * inv_freq[None, :]
    cos, sin = jnp.cos(ang), jnp.sin(ang)
    return jnp.concatenate([cos, cos], axis=1), jnp.concatenate([-sin, sin], axis=1)


def _trunk(x, mem, g_norm, w_in, sink, g_mem, w_mem_kv, w_branch, w_out, g_final):
    batch, seq, d = x.shape
    rows = batch * seq
    x2 = x.reshape(rows, d)
    rope = _rope_tables(seq)

    h = _rmsnorm(x2, g_norm, jnp.bfloat16)
    pa = _proj(h, w_in, A_TILES, ["rope"] * 6 + ["none"] * 3, tm=1024, seq=seq, rope=rope, name="proj_a")
    pb = _proj(h, w_in, B_TILES, ["rope", "rope", "rope_half", "none"], tm=1024, seq=seq, rope=rope,
               name="proj_b")
    pz = _proj(h, w_in, Z_TILES, ["silu"] * 4, tm=1024, name="proj_z")
    pg = _proj(h, w_in, G_TILES, ["sigmoid"] * 12, tm=1024, name="proj_g")

    hm = _rmsnorm(mem.reshape(batch * MEM_LEN, d), g_mem, jnp.bfloat16)
    kv_mem = _proj(hm, w_mem_kv, (0, 2), ["none"] * 2, tm=batch * MEM_LEN, name="proj_mem")

    outs, lses = [], []
    for gi, (_, dil) in enumerate(DIL_CONFIGS):
        sd = seq // dil
        n = batch * dil
        if dil == 1:
            qkv = pa.reshape(9 * batch, seq, A_WIDTH)
            offs = (gi * batch, (3 + gi) * batch, (6 + gi) * batch)
        else:
            qkv = pa.reshape(3, 3, batch, sd, dil, A_WIDTH)[:, gi]
            qkv = qkv.transpose(0, 1, 3, 2, 4).reshape(3 * n, sd, A_WIDTH)
            offs = (0, n, 2 * n)
        o, lse = _attn_a(qkv, n, sd, offs)
        if dil > 1:
            o = o.reshape(batch, dil, sd, A_WIDTH).transpose(0, 2, 1, 3)
            lse = lse.reshape(batch, dil, sd, A_HEADS).transpose(0, 2, 1, 3)
        outs.append(o.reshape(rows, A_WIDTH))
        lses.append(lse.reshape(rows, A_HEADS))
    ua = _merge_a(outs, lses, pz)

    ub = _attn_b(sink, pb, pz, batch, seq)
    um = _attn_m(pb, kv_mem, pz, batch, seq)

    u = _branch(ua, ub, um, w_branch, pg)
    y = _out_proj(u, w_out, x2)
    return _rmsnorm(y, g_final, jnp.float32).reshape(batch, seq, d)


def kernel(x_prompt, x_sample, mem_prompt, mem_sample, g_norm, w_in, attn_sink, g_mem, w_mem_kv, w_branch, w_out, g_final):
    bf = jnp.bfloat16
    w = (g_norm[0], w_in[0].astype(bf), attn_sink[0], g_mem[0], w_mem_kv[0].astype(bf),
         w_branch[0].astype(bf), w_out[0].astype(bf), g_final)
    return (_trunk(x_prompt, mem_prompt, *w), _trunk(x_sample, mem_sample, *w))
```

```python
import functools

import jax
import jax.numpy as jnp
from jax import lax
from jax.experimental import pallas as pl
from jax.experimental.pallas import tpu as pltpu

D_MODEL = 4096
HEAD_DIM = 128
ROPE_THETA = 10000.0
NORM_EPS = 1e-6
NEG_INF = -1e30

DIL_CONFIGS = ((128, 1), (512, 4), (2048, 16))
A_HEADS = 8
A_WIDTH = A_HEADS * HEAD_DIM
A_RADIUS = 64

B_Q_HEADS = 16
B_KV_HEADS = 4
B_GROUP = B_Q_HEADS // B_KV_HEADS
B_RADIUS = 128

MEM_LEN = 256
M_HEADS = 4
M_HEAD_DIM = 256

TN = 1024
A_TILES = (0, 9)
B_TILES = (9, 13)
Z_TILES = (13, 17)
G_TILES = (17, 29)

VMEM_LIMIT = 56 * 1024 * 1024


def _params(sem, vmem=VMEM_LIMIT):
    return pltpu.CompilerParams(dimension_semantics=sem, vmem_limit_bytes=vmem)


def _rmsnorm_kernel(x_ref, g_ref, o_ref):
    x = x_ref[...]
    ms = jnp.mean(x * x, axis=-1, keepdims=True)
    o_ref[...] = ((x * lax.rsqrt(ms + NORM_EPS)) * g_ref[...]).astype(o_ref.dtype)


def _rmsnorm(x, g, out_dtype, tm=256):
    m, d = x.shape
    return pl.pallas_call(
        _rmsnorm_kernel,
        grid=(m // tm,),
        in_specs=[pl.BlockSpec((tm, d), lambda i: (i, 0)),
                  pl.BlockSpec((1, d), lambda i: (0, 0))],
        out_specs=pl.BlockSpec((tm, d), lambda i: (i, 0)),
        out_shape=jax.ShapeDtypeStruct((m, d), out_dtype),
        compiler_params=_params(("parallel",)),
        name="rmsnorm",
    )(x, g.reshape(1, d))


def _rope(x, cos, sin_signed):
    return x * cos + pltpu.roll(x, HEAD_DIM // 2, 1) * sin_signed


def _proj_kernel(*refs, epis, dil):
    has_rope = any(e.startswith("rope") for e in epis)
    refs = list(refs)
    nat_ref = refs.pop() if dil > 1 else None
    o_ref = refs.pop()
    h_ref, w_ref = refs[:2]
    cos_ref, sin_ref = refs[2:] if has_rope else (None, None)
    tm = h_ref.shape[0]
    acc = jnp.dot(h_ref[...], w_ref[...], preferred_element_type=jnp.float32)

    heads = [slice(hh * HEAD_DIM, (hh + 1) * HEAD_DIM) for hh in range(TN // HEAD_DIM)]

    def put(hh, x):
        if dil > 1:
            nat_ref[hh] = x
        else:
            o_ref[0, :, heads[hh]] = x.astype(o_ref.dtype)

    def emit(kind):
        if kind == "silu":
            o_ref[0] = (acc * jax.nn.sigmoid(acc)).astype(o_ref.dtype)
            return
        if kind == "sigmoid":
            o_ref[0] = jax.nn.sigmoid(acc).astype(o_ref.dtype)
            return
        n_rope = {"none": 0, "rope": len(heads), "rope_half": len(heads) // 2}[kind]
        if n_rope:
            cos = cos_ref[...]
            sin = sin_ref[...]
        for hh, sl in enumerate(heads):
            x = acc[:, sl]
            put(hh, _rope(x, cos, sin) if hh < n_rope else x)
        for r in range(dil if dil > 1 else 0):
            for hh, sl in enumerate(heads):
                o_ref[r, :, sl] = nat_ref[hh, pl.ds(r, tm // dil, stride=dil), :].astype(o_ref.dtype)

    kinds = sorted(set(epis))
    if len(kinds) == 1:
        emit(kinds[0])
    else:
        j = pl.program_id(1)
        for kind in kinds:
            cond = functools.reduce(
                jnp.logical_or, [j == t for t, e in enumerate(epis) if e == kind])
            pl.when(cond)(functools.partial(emit, kind))


def _proj(h, w, tiles, epis, *, batch, tm, dil=1, rope=None, name):
    m, k = h.shape
    seq = m // batch
    col_tiles = range(*tiles)
    lo, step, nt = col_tiles.start, col_tiles.step, len(col_tiles)
    assert len(epis) == nt and seq % tm == 0 and tm % dil == 0
    assert dil == 1 or all(e in ("none", "rope", "rope_half") for e in epis)
    bps = seq // tm
    in_specs = [pl.BlockSpec((tm, k), lambda i, j: (i, 0)),
                pl.BlockSpec((k, TN), lambda i, j: (0, lo + step * j))]
    args = [h, w]
    if rope is not None:
        tab_spec = pl.BlockSpec((tm, HEAD_DIM), lambda i, j: (i % bps, 0))
        in_specs += [tab_spec, tab_spec]
        args += list(rope)
    return pl.pallas_call(
        functools.partial(_proj_kernel, epis=tuple(epis), dil=dil),
        grid=(m // tm, nt),
        in_specs=in_specs,
        out_specs=pl.BlockSpec((None, None, dil, tm // dil, TN),
                               lambda i, j: (j, i // bps, 0, i % bps, 0)),
        out_shape=jax.ShapeDtypeStruct((nt, batch, dil, seq // dil, TN), jnp.bfloat16),
        scratch_shapes=[pltpu.VMEM((TN // HEAD_DIM, tm, HEAD_DIM), jnp.float32)] if dil > 1 else [],
        compiler_params=_params(("parallel", "arbitrary")),
        name=name,
    )(*args)


A_TQ = 128
A_HALO = 64
LSE_LANES = 128


def _attn_a_kernel(q_ref, kp_ref, kc_ref, kn_ref, vp_ref, vc_ref, vn_ref, o_ref, lse_ref, *, seq):
    start = pl.program_id(1) * A_TQ
    nk = A_TQ + 2 * A_HALO
    r = lax.broadcasted_iota(jnp.int32, (A_TQ, nk), 0)
    c = lax.broadcasted_iota(jnp.int32, (A_TQ, nk), 1)
    kpos = start - A_HALO + c
    rel = c - A_HALO - r
    valid = (jnp.abs(rel) <= A_RADIUS) & (kpos >= 0) & (kpos < seq)
    lane = lax.broadcasted_iota(jnp.int32, (A_TQ, LSE_LANES), 1)
    lse_all = jnp.zeros((A_TQ, LSE_LANES), jnp.float32)
    scale = HEAD_DIM ** -0.5
    for hh in range(A_HEADS):
        sl = slice(hh * HEAD_DIM, (hh + 1) * HEAD_DIM)
        k = jnp.concatenate([kp_ref[:, sl], kc_ref[:, sl], kn_ref[:, sl]], axis=0)
        v = jnp.concatenate([vp_ref[:, sl], vc_ref[:, sl], vn_ref[:, sl]], axis=0)
        s = lax.dot_general(q_ref[:, sl], k, (((1,), (1,)), ((), ())),
                            preferred_element_type=jnp.float32) * scale
        s = jnp.where(valid, s, NEG_INF)
        m = jnp.max(s, axis=1, keepdims=True)
        e = jnp.exp(s - m)
        den = jnp.sum(e, axis=1, keepdims=True)
        o = jnp.dot(e.astype(jnp.bfloat16), v, preferred_element_type=jnp.float32)
        o_ref[:, sl] = o * (1.0 / den)
        lse_all = jnp.where(lane == hh, m + jnp.log(den), lse_all)
    lse_ref[...] = lse_all


def _attn_a(qkv, n, seq, offs):
    qo, ko, vo = offs
    nq = seq // A_TQ
    nh = seq // A_HALO
    ratio = A_TQ // A_HALO
    cur = lambda off: pl.BlockSpec((None, A_TQ, A_WIDTH), lambda s, i: (off + s, i, 0))
    prev = lambda off: pl.BlockSpec(
        (None, A_HALO, A_WIDTH), lambda s, i: (off + s, jnp.maximum(ratio * i - 1, 0), 0))
    nxt = lambda off: pl.BlockSpec(
        (None, A_HALO, A_WIDTH), lambda s, i: (off + s, jnp.minimum(ratio * (i + 1), nh - 1), 0))
    return pl.pallas_call(
        functools.partial(_attn_a_kernel, seq=seq),
        grid=(n, nq),
        in_specs=[cur(qo), prev(ko), cur(ko), nxt(ko), prev(vo), cur(vo), nxt(vo)],
        out_specs=[pl.BlockSpec((None, A_TQ, A_WIDTH), lambda s, i: (s, i, 0)),
                   pl.BlockSpec((None, A_TQ, LSE_LANES), lambda s, i: (s, i, 0))],
        out_shape=[jax.ShapeDtypeStruct((n, seq, A_WIDTH), jnp.float32),
                   jax.ShapeDtypeStruct((n, seq, LSE_LANES), jnp.float32)],
        compiler_params=_params(("parallel", "parallel")),
        name="attn_a",
    )(qkv, qkv, qkv, qkv, qkv, qkv, qkv)


def _merge_a_kernel(c0_ref, c1_ref, c2_ref, cl0_ref, cl1_ref, cl2_ref, z_ref, u_ref,
                    o1_ref, o2_ref, l1_ref, l2_ref):
    tm = u_ref.shape[0]
    o0_ref, l0_ref = c0_ref.at[0], cl0_ref.at[0]
    for c_ref, cl_ref, o_ref, l_ref in ((c1_ref, cl1_ref, o1_ref, l1_ref), (c2_ref, cl2_ref, o2_ref, l2_ref)):
        dil = c_ref.shape[0]
        for r in range(dil):
            rows = pl.ds(r, tm // dil, stride=dil)
            for hh in range(A_HEADS):
                o_ref[hh, rows, :] = c_ref[r, :, hh * HEAD_DIM:(hh + 1) * HEAD_DIM]
            l_ref[rows, :] = cl_ref[r]
    l0, l1, l2 = l0_ref[...], l1_ref[...], l2_ref[...]
    m = jnp.maximum(jnp.maximum(l0, l1), l2)
    e0, e1, e2 = jnp.exp(l0 - m), jnp.exp(l1 - m), jnp.exp(l2 - m)
    inv = 1.0 / (e0 + e1 + e2)
    a0, a1, a2 = e0 * inv, e1 * inv, e2 * inv
    for hh in range(A_HEADS):
        sl = slice(hh * HEAD_DIM, (hh + 1) * HEAD_DIM)
        col = slice(hh, hh + 1)
        o = a0[:, col] * o0_ref[:, sl] + a1[:, col] * o1_ref[hh] + a2[:, col] * o2_ref[hh]
        u_ref[:, sl] = (o * z_ref[:, sl].astype(jnp.float32)).astype(u_ref.dtype)


def _merge_a(outs, lses, pz, batch, seq, tm=512):
    bps = seq // tm
    cls = lambda dil, width: pl.BlockSpec((None, dil, tm // dil, width), lambda i: (i // bps, 0, i % bps, 0))
    dils = [d for _, d in DIL_CONFIGS]
    return pl.pallas_call(
        _merge_a_kernel,
        grid=(batch * bps,),
        in_specs=([cls(d, A_WIDTH) for d in dils] + [cls(d, LSE_LANES) for d in dils]
                  + [pl.BlockSpec((None, tm, TN), lambda i: (0, i, 0))]),
        out_specs=pl.BlockSpec((tm, A_WIDTH), lambda i: (i, 0)),
        out_shape=jax.ShapeDtypeStruct((batch * seq, A_WIDTH), jnp.bfloat16),
        scratch_shapes=[pltpu.VMEM((A_HEADS, tm, HEAD_DIM), jnp.float32)] * 2
                       + [pltpu.VMEM((tm, LSE_LANES), jnp.float32)] * 2,
        compiler_params=_params(("parallel",)),
        name="merge_a",
    )(*outs, *lses, pz)


B_TQ = 128


def _attn_b_kernel(sink_ref, q0_ref, q1_ref, kvp_ref, kvc_ref, kvn_ref, z0_ref, z1_ref, u_ref, *, seq):
    start = pl.program_id(1) * B_TQ
    rows = B_GROUP * B_TQ
    nk = 3 * B_TQ
    r = lax.broadcasted_iota(jnp.int32, (rows, nk), 0) & (B_TQ - 1)
    c = lax.broadcasted_iota(jnp.int32, (rows, nk), 1)
    kpos = start - B_TQ + c
    rel = c - B_TQ - r
    valid = (jnp.abs(rel) <= B_RADIUS) & (kpos >= 0) & (kpos < seq)
    grp = lax.broadcasted_iota(jnp.int32, (rows, 1), 0) // B_TQ
    scale = HEAD_DIM ** -0.5
    kv_half = B_KV_HEADS * HEAD_DIM
    for kh in range(B_KV_HEADS):
        q_ref = q0_ref if kh < 2 else q1_ref
        z_ref = z0_ref if kh < 2 else z1_ref
        base = (kh % 2) * B_GROUP * HEAD_DIM
        ksl = slice(kh * HEAD_DIM, (kh + 1) * HEAD_DIM)
        vsl = slice(kv_half + kh * HEAD_DIM, kv_half + (kh + 1) * HEAD_DIM)
        q = jnp.concatenate(
            [q_ref[:, base + g * HEAD_DIM: base + (g + 1) * HEAD_DIM] for g in range(B_GROUP)], axis=0)
        k = jnp.concatenate([kvp_ref[:, ksl], kvc_ref[:, ksl], kvn_ref[:, ksl]], axis=0)
        v = jnp.concatenate([kvp_ref[:, vsl], kvc_ref[:, vsl], kvn_ref[:, vsl]], axis=0)
        s = lax.dot_general(q, k, (((1,), (1,)), ((), ())),
                            preferred_element_type=jnp.float32) * scale
        s = jnp.where(valid, s, NEG_INF)
        sk = jnp.full((rows, 1), sink_ref[kh * B_GROUP], jnp.float32)
        for g in range(1, B_GROUP):
            sk = jnp.where(grp == g, sink_ref[kh * B_GROUP + g], sk)
        m = jnp.maximum(jnp.max(s, axis=1, keepdims=True), sk)
        e = jnp.exp(s - m)
        den = jnp.sum(e, axis=1, keepdims=True) + jnp.exp(sk - m)
        o = jnp.dot(e.astype(jnp.bfloat16), v, preferred_element_type=jnp.float32) * (1.0 / den)
        for g in range(B_GROUP):
            zsl = slice(base + g * HEAD_DIM, base + (g + 1) * HEAD_DIM)
            col = (kh * B_GROUP + g) * HEAD_DIM
            u_ref[:, col:col + HEAD_DIM] = (
                o[g * B_TQ:(g + 1) * B_TQ] * z_ref[:, zsl].astype(jnp.float32)).astype(u_ref.dtype)


def _attn_b(sink, pb, pz, batch, seq):
    nq = seq // B_TQ
    row = lambda slot: pl.BlockSpec((None, B_TQ, TN), lambda b, i: (slot, b * nq + i, 0))
    kv_prev = pl.BlockSpec((None, B_TQ, TN), lambda b, i: (2, b * nq + jnp.maximum(i - 1, 0), 0))
    kv_next = pl.BlockSpec((None, B_TQ, TN), lambda b, i: (2, b * nq + jnp.minimum(i + 1, nq - 1), 0))
    width = B_Q_HEADS * HEAD_DIM
    return pl.pallas_call(
        functools.partial(_attn_b_kernel, seq=seq),
        grid=(batch, nq),
        in_specs=[pl.BlockSpec(memory_space=pltpu.SMEM),
                  row(0), row(1), kv_prev, row(2), kv_next, row(1), row(2)],
        out_specs=pl.BlockSpec((B_TQ, width), lambda b, i: (b * nq + i, 0)),
        out_shape=jax.ShapeDtypeStruct((batch * seq, width), jnp.bfloat16),
        compiler_params=_params(("parallel", "parallel")),
        name="attn_b",
    )(sink, pb, pb, pb, pb, pb, pz, pz)


M_TQ = 256


def _attn_m_kernel(q_ref, k_ref, v_ref, z_ref, u_ref):
    scale = M_HEAD_DIM ** -0.5
    for hh in range(M_HEADS):
        sl = slice(hh * M_HEAD_DIM, (hh + 1) * M_HEAD_DIM)
        s = lax.dot_general(q_ref[:, sl], k_ref[:, sl], (((1,), (1,)), ((), ())),
                            preferred_element_type=jnp.float32) * scale
        m = jnp.max(s, axis=1, keepdims=True)
        e = jnp.exp(s - m)
        den = jnp.sum(e, axis=1, keepdims=True)
        o = jnp.dot(e.astype(jnp.bfloat16), v_ref[:, sl], preferred_element_type=jnp.float32) * (1.0 / den)
        u_ref[:, sl] = (o * z_ref[:, sl].astype(jnp.float32)).astype(u_ref.dtype)


def _attn_m(pb, kv_mem, pz, batch, seq):
    nq = seq // M_TQ
    width = M_HEADS * M_HEAD_DIM
    return pl.pallas_call(
        _attn_m_kernel,
        grid=(batch, nq),
        in_specs=[pl.BlockSpec((None, M_TQ, TN), lambda b, i: (3, b * nq + i, 0)),
                  pl.BlockSpec((None, MEM_LEN, TN), lambda b, i: (0, b, 0)),
                  pl.BlockSpec((None, MEM_LEN, TN), lambda b, i: (1, b, 0)),
                  pl.BlockSpec((None, M_TQ, TN), lambda b, i: (3, b * nq + i, 0))],
        out_specs=pl.BlockSpec((M_TQ, width), lambda b, i: (b * nq + i, 0)),
        out_shape=jax.ShapeDtypeStruct((batch * seq, width), jnp.bfloat16),
        compiler_params=_params(("parallel", "parallel")),
        name="attn_m",
    )(pb, kv_mem, kv_mem, pz)


def _branch_kernel(ua_ref, ub_ref, um_ref, w_ref, ga_ref, gb_ref, gm_ref, u_ref):
    a_hi = A_WIDTH
    b_hi = A_WIDTH + B_Q_HEADS * HEAD_DIM
    dot = functools.partial(jnp.dot, preferred_element_type=jnp.float32)
    acc = ga_ref[...].astype(jnp.float32) * dot(ua_ref[...], w_ref[:a_hi, :])
    acc += gb_ref[...].astype(jnp.float32) * dot(ub_ref[...], w_ref[a_hi:b_hi, :])
    acc += gm_ref[...].astype(jnp.float32) * dot(um_ref[...], w_ref[b_hi:, :])
    u_ref[...] = acc.astype(u_ref.dtype)


def _branch(ua, ub, um, w_branch, pg, tm=512):
    m = ua.shape[0]
    k = w_branch.shape[0]
    n_tiles = D_MODEL // TN
    act = lambda width: pl.BlockSpec((tm, width), lambda i, j: (i, 0))
    gate = lambda br: pl.BlockSpec((None, tm, TN), lambda i, j: (br * n_tiles + j, i, 0))
    return pl.pallas_call(
        _branch_kernel,
        grid=(m // tm, n_tiles),
        in_specs=[act(ua.shape[1]), act(ub.shape[1]), act(um.shape[1]),
                  pl.BlockSpec((k, TN), lambda i, j: (0, j)),
                  gate(0), gate(1), gate(2)],
        out_specs=pl.BlockSpec((tm, TN), lambda i, j: (i, j)),
        out_shape=jax.ShapeDtypeStruct((m, D_MODEL), jnp.bfloat16),
        compiler_params=_params(("parallel", "arbitrary")),
        name="branch_proj",
    )(ua, ub, um, w_branch, pg, pg, pg)


def _out_kernel(u_ref, w_ref, x_ref, y_ref):
    y_ref[...] = x_ref[...] + jnp.dot(u_ref[...], w_ref[...], preferred_element_type=jnp.float32)


def _out_proj(u, w_out, x, tm=512):
    m, k = u.shape
    n = x.shape[1]
    tn = min(TN, n)
    return pl.pallas_call(
        _out_kernel,
        grid=(m // tm, n // tn),
        in_specs=[pl.BlockSpec((tm, k), lambda i, j: (i, 0)),
                  pl.BlockSpec((k, tn), lambda i, j: (0, j)),
                  pl.BlockSpec((tm, tn), lambda i, j: (i, j))],
        out_specs=pl.BlockSpec((tm, tn), lambda i, j: (i, j)),
        out_shape=jax.ShapeDtypeStruct((m, n), jnp.float32),
        compiler_params=_params(("parallel", "arbitrary")),
        name="out_proj",
    )(u, w_out, x)


def _rope_tables(seq):
    inv_freq = ROPE_THETA ** (-jnp.arange(0, HEAD_DIM, 2, dtype=jnp.float32) / HEAD_DIM)
    ang = jnp.arange(seq, dtype=jnp.float32)[:, None] * inv_freq[None, :]
    cos, sin = jnp.cos(ang), jnp.sin(ang)
    return jnp.concatenate([cos, cos], axis=1), jnp.concatenate([-sin, sin], axis=1)


def _trunk(x, mem, g_norm, w_in, sink, g_mem, w_mem_kv, w_branch, w_out, g_final):
    batch, seq, d = x.shape
    rows = batch * seq
    x2 = x.reshape(rows, d)
    rope = _rope_tables(seq)

    h = _rmsnorm(x2, g_norm, jnp.bfloat16)
    flat = lambda p: p.reshape(p.shape[0], rows, TN)
    pb = flat(_proj(h, w_in, B_TILES, ["rope", "rope", "rope_half", "none"], batch=batch, tm=1024,
                    rope=rope, name="proj_b"))
    pz = flat(_proj(h, w_in, Z_TILES, ["silu"] * 4, batch=batch, tm=1024, name="proj_z"))
    pg = flat(_proj(h, w_in, G_TILES, ["sigmoid"] * 12, batch=batch, tm=1024, name="proj_g"))

    hm = _rmsnorm(mem.reshape(batch * MEM_LEN, d), g_mem, jnp.bfloat16)
    kv_mem = _proj(hm, w_mem_kv, (0, 2), ["none"] * 2, batch=1, tm=batch * MEM_LEN, name="proj_mem")
    kv_mem = kv_mem.reshape(2, batch * MEM_LEN, TN)

    outs, lses = [], []
    for gi, (_, dil) in enumerate(DIL_CONFIGS):
        sd = seq // dil
        n = batch * dil
        tiles = (A_TILES[0] + gi, A_TILES[1], len(DIL_CONFIGS))
        qkv = _proj(h, w_in, tiles, ["rope", "rope", "none"], batch=batch, tm=1024, dil=dil,
                    rope=rope, name=f"proj_a{gi}")
        o, lse = _attn_a(qkv.reshape(3 * n, sd, A_WIDTH), n, sd, (0, n, 2 * n))
        outs.append(o.reshape(batch, dil, sd, A_WIDTH))
        lses.append(lse.reshape(batch, dil, sd, LSE_LANES))
    ua = _merge_a(outs, lses, pz, batch, seq)

    ub = _attn_b(sink, pb, pz, batch, seq)
    um = _attn_m(pb, kv_mem, pz, batch, seq)

    u = _branch(ua, ub, um, w_branch, pg)
    y = _out_proj(u, w_out, x2)
    return _rmsnorm(y, g_final, jnp.float32).reshape(batch, seq, d)


def kernel(x_prompt, x_sample, mem_prompt, mem_sample, g_norm, w_in, attn_sink, g_mem, w_mem_kv, w_branch, w_out, g_final):
    bf = jnp.bfloat16
    w = (g_norm[0], w_in[0].astype(bf), attn_sink[0], g_mem[0], w_mem_kv[0].astype(bf),
         w_branch[0].astype(bf), w_out[0].astype(bf), g_final)
    return (_trunk(x_prompt, mem_prompt, *w), _trunk(x_sample, mem_sample, *w))
```

```python
import functools

import jax
import jax.numpy as jnp
from jax import lax
from jax.experimental import pallas as pl
from jax.experimental.pallas import tpu as pltpu

D_MODEL = 4096
HEAD_DIM = 128
ROPE_THETA = 10000.0
NORM_EPS = 1e-6
NEG_INF = -1e30
LOG2E = 1.4426950408889634
LN2 = 0.6931471805599453

DIL_CONFIGS = ((128, 1), (512, 4), (2048, 16))
A_HEADS = 8
A_WIDTH = A_HEADS * HEAD_DIM
A_RADIUS = 64

B_Q_HEADS = 16
B_KV_HEADS = 4
B_GROUP = B_Q_HEADS // B_KV_HEADS
B_RADIUS = 128

MEM_LEN = 256
M_HEADS = 4
M_HEAD_DIM = 256

TN = 1024
A_TILES = (0, 9)
B_TILES = (9, 13)
Z_TILES = (13, 17)
G_TILES = (17, 29)

VMEM_LIMIT = 56 * 1024 * 1024


def _params(sem, vmem=VMEM_LIMIT):
    return pltpu.CompilerParams(dimension_semantics=sem, vmem_limit_bytes=vmem)


def _rmsnorm_kernel(x_ref, g_ref, o_ref):
    x = x_ref[...]
    ms = jnp.mean(x * x, axis=-1, keepdims=True)
    o_ref[...] = ((x * lax.rsqrt(ms + NORM_EPS)) * g_ref[...]).astype(o_ref.dtype)


def _rmsnorm(x, g, out_dtype, tm=256):
    m, d = x.shape
    return pl.pallas_call(
        _rmsnorm_kernel,
        grid=(m // tm,),
        in_specs=[pl.BlockSpec((tm, d), lambda i: (i, 0)),
                  pl.BlockSpec((1, d), lambda i: (0, 0))],
        out_specs=pl.BlockSpec((tm, d), lambda i: (i, 0)),
        out_shape=jax.ShapeDtypeStruct((m, d), out_dtype),
        compiler_params=_params(("parallel",)),
        name="rmsnorm",
    )(x, g.reshape(1, d))


def _rope(x, cos, sin_signed):
    return x * cos + pltpu.roll(x, HEAD_DIM // 2, 1) * sin_signed


def _proj_kernel(*refs, epis, dil):
    has_rope = any(e.startswith("rope") for e in epis)
    refs = list(refs)
    nat_ref = refs.pop() if dil > 1 else None
    o_ref = refs.pop()
    h_ref, w_ref = refs[:2]
    cos_ref, sin_ref, cosq_ref, sinq_ref = refs[2:] if has_rope else (None,) * 4
    tm = h_ref.shape[0]
    acc = jnp.dot(h_ref[...], w_ref[...], preferred_element_type=jnp.float32)

    heads = [slice(hh * HEAD_DIM, (hh + 1) * HEAD_DIM) for hh in range(TN // HEAD_DIM)]

    def put(hh, x):
        if dil > 1:
            nat_ref[hh] = x
        else:
            o_ref[0, :, heads[hh]] = x.astype(o_ref.dtype)

    def emit(kind):
        if kind == "silu":
            o_ref[0] = (acc * jax.nn.sigmoid(acc)).astype(o_ref.dtype)
            return
        if kind == "sigmoid":
            o_ref[0] = jax.nn.sigmoid(acc).astype(o_ref.dtype)
            return
        n_rope = {"none": 0, "scale_m": 0, "rope": len(heads), "rope_q": len(heads),
                  "rope_half": len(heads) // 2}[kind]
        if n_rope:
            cos = (cosq_ref if kind == "rope_q" else cos_ref)[...]
            sin = (sinq_ref if kind == "rope_q" else sin_ref)[...]
        for hh, sl in enumerate(heads):
            x = acc[:, sl]
            if kind == "scale_m":
                x = x * (M_HEAD_DIM ** -0.5 * LOG2E)
            put(hh, _rope(x, cos, sin) if hh < n_rope else x)
        for r in range(dil if dil > 1 else 0):
            for hh, sl in enumerate(heads):
                o_ref[r, :, sl] = nat_ref[hh, pl.ds(r, tm // dil, stride=dil), :].astype(o_ref.dtype)

    kinds = sorted(set(epis))
    if len(kinds) == 1:
        emit(kinds[0])
    else:
        j = pl.program_id(1)
        for kind in kinds:
            cond = functools.reduce(
                jnp.logical_or, [j == t for t, e in enumerate(epis) if e == kind])
            pl.when(cond)(functools.partial(emit, kind))


def _proj(h, w, tiles, epis, *, batch, tm, dil=1, rope=None, name):
    m, k = h.shape
    seq = m // batch
    col_tiles = range(*tiles)
    lo, step, nt = col_tiles.start, col_tiles.step, len(col_tiles)
    assert len(epis) == nt and seq % tm == 0 and tm % dil == 0
    assert dil == 1 or all(e in ("none", "rope", "rope_q") for e in epis)
    bps = seq // tm
    in_specs = [pl.BlockSpec((tm, k), lambda i, j: (i, 0)),
                pl.BlockSpec((k, TN), lambda i, j: (0, lo + step * j))]
    args = [h, w]
    if rope is not None:
        tab_spec = pl.BlockSpec((tm, HEAD_DIM), lambda i, j: (i % bps, 0))
        in_specs += [tab_spec] * len(rope)
        args += list(rope)
    return pl.pallas_call(
        functools.partial(_proj_kernel, epis=tuple(epis), dil=dil),
        grid=(m // tm, nt),
        in_specs=in_specs,
        out_specs=pl.BlockSpec((None, None, dil, tm // dil, TN),
                               lambda i, j: (j, i // bps, 0, i % bps, 0)),
        out_shape=jax.ShapeDtypeStruct((nt, batch, dil, seq // dil, TN), jnp.bfloat16),
        scratch_shapes=[pltpu.VMEM((TN // HEAD_DIM, tm, HEAD_DIM), jnp.float32)] if dil > 1 else [],
        compiler_params=_params(("parallel", "arbitrary")),
        name=name,
    )(*args)


A_TQ = 128
A_HALO = 64
LSE_LANES = 128


def _attn_a_kernel(q_ref, kp_ref, kc_ref, kn_ref, vp_ref, vc_ref, vn_ref, o_ref, lse_ref, *, seq):
    start = pl.program_id(1) * A_TQ
    nk = A_TQ + 2 * A_HALO
    r = lax.broadcasted_iota(jnp.int32, (A_TQ, nk), 0)
    c = lax.broadcasted_iota(jnp.int32, (A_TQ, nk), 1)
    kpos = start - A_HALO + c
    rel = c - A_HALO - r
    valid = (jnp.abs(rel) <= A_RADIUS) & (kpos >= 0) & (kpos < seq)
    lane = lax.broadcasted_iota(jnp.int32, (A_TQ, LSE_LANES), 1)
    lse_all = jnp.zeros((A_TQ, LSE_LANES), jnp.float32)
    bias = jnp.where(valid, 0.0, NEG_INF)
    ones = jnp.ones((nk, HEAD_DIM), jnp.bfloat16)
    for hh in range(A_HEADS):
        sl = slice(hh * HEAD_DIM, (hh + 1) * HEAD_DIM)
        k = jnp.concatenate([kp_ref[:, sl], kc_ref[:, sl], kn_ref[:, sl]], axis=0)
        v = jnp.concatenate([vp_ref[:, sl], vc_ref[:, sl], vn_ref[:, sl]], axis=0)
        s = lax.dot_general(q_ref[:, sl], k, (((1,), (1,)), ((), ())),
                            preferred_element_type=jnp.float32) + bias
        m = jnp.max(s, axis=1, keepdims=True)
        e = jnp.exp2(s - m).astype(jnp.bfloat16)
        ov = jnp.dot(e, jnp.concatenate([v, ones], axis=1), preferred_element_type=jnp.float32)
        den = ov[:, HEAD_DIM:HEAD_DIM + 1]
        o_ref[:, sl] = ov[:, :HEAD_DIM] * (1.0 / den)
        lse_all = jnp.where(lane == hh, m * LN2 + jnp.log(den), lse_all)
    lse_ref[...] = lse_all


def _attn_a(qkv, n, seq, offs):
    qo, ko, vo = offs
    nq = seq // A_TQ
    nh = seq // A_HALO
    ratio = A_TQ // A_HALO
    cur = lambda off: pl.BlockSpec((None, A_TQ, A_WIDTH), lambda s, i: (off + s, i, 0))
    prev = lambda off: pl.BlockSpec(
        (None, A_HALO, A_WIDTH), lambda s, i: (off + s, jnp.maximum(ratio * i - 1, 0), 0))
    nxt = lambda off: pl.BlockSpec(
        (None, A_HALO, A_WIDTH), lambda s, i: (off + s, jnp.minimum(ratio * (i + 1), nh - 1), 0))
    return pl.pallas_call(
        functools.partial(_attn_a_kernel, seq=seq),
        grid=(n, nq),
        in_specs=[cur(qo), prev(ko), cur(ko), nxt(ko), prev(vo), cur(vo), nxt(vo)],
        out_specs=[pl.BlockSpec((None, A_TQ, A_WIDTH), lambda s, i: (s, i, 0)),
                   pl.BlockSpec((None, A_TQ, LSE_LANES), lambda s, i: (s, i, 0))],
        out_shape=[jax.ShapeDtypeStruct((n, seq, A_WIDTH), jnp.float32),
                   jax.ShapeDtypeStruct((n, seq, LSE_LANES), jnp.float32)],
        compiler_params=_params(("parallel", "parallel")),
        name="attn_a",
    )(qkv, qkv, qkv, qkv, qkv, qkv, qkv)


def _merge_a_kernel(c0_ref, c1_ref, c2_ref, cl0_ref, cl1_ref, cl2_ref, z_ref, u_ref,
                    o1_ref, o2_ref, l1_ref, l2_ref):
    tm = u_ref.shape[0]
    o0_ref, l0_ref = c0_ref.at[0], cl0_ref.at[0]
    for c_ref, cl_ref, o_ref, l_ref in ((c1_ref, cl1_ref, o1_ref, l1_ref), (c2_ref, cl2_ref, o2_ref, l2_ref)):
        dil = c_ref.shape[0]
        for r in range(dil):
            rows = pl.ds(r, tm // dil, stride=dil)
            for hh in range(A_HEADS):
                o_ref[hh, rows, :] = c_ref[r, :, hh * HEAD_DIM:(hh + 1) * HEAD_DIM]
            l_ref[rows, :] = cl_ref[r]
    l0, l1, l2 = l0_ref[...], l1_ref[...], l2_ref[...]
    m = jnp.maximum(jnp.maximum(l0, l1), l2)
    e0, e1, e2 = jnp.exp(l0 - m), jnp.exp(l1 - m), jnp.exp(l2 - m)
    inv = 1.0 / (e0 + e1 + e2)
    a0, a1, a2 = e0 * inv, e1 * inv, e2 * inv
    for hh in range(A_HEADS):
        sl = slice(hh * HEAD_DIM, (hh + 1) * HEAD_DIM)
        col = slice(hh, hh + 1)
        o = a0[:, col] * o0_ref[:, sl] + a1[:, col] * o1_ref[hh] + a2[:, col] * o2_ref[hh]
        u_ref[:, sl] = (o * z_ref[:, sl].astype(jnp.float32)).astype(u_ref.dtype)


def _merge_a(outs, lses, pz, batch, seq, tm=512):
    bps = seq // tm
    cls = lambda dil, width: pl.BlockSpec((None, dil, tm // dil, width), lambda i: (i // bps, 0, i % bps, 0))
    dils = [d for _, d in DIL_CONFIGS]
    return pl.pallas_call(
        _merge_a_kernel,
        grid=(batch * bps,),
        in_specs=([cls(d, A_WIDTH) for d in dils] + [cls(d, LSE_LANES) for d in dils]
                  + [pl.BlockSpec((None, tm, TN), lambda i: (0, i, 0))]),
        out_specs=pl.BlockSpec((tm, A_WIDTH), lambda i: (i, 0)),
        out_shape=jax.ShapeDtypeStruct((batch * seq, A_WIDTH), jnp.bfloat16),
        scratch_shapes=[pltpu.VMEM((A_HEADS, tm, HEAD_DIM), jnp.float32)] * 2
                       + [pltpu.VMEM((tm, LSE_LANES), jnp.float32)] * 2,
        compiler_params=_params(("parallel",)),
        name="merge_a",
    )(*outs, *lses, pz)


B_TQ = 128


def _attn_b_kernel(sink_ref, q0_ref, q1_ref, kvp_ref, kvc_ref, kvn_ref, z0_ref, z1_ref, u_ref, *, seq):
    start = pl.program_id(1) * B_TQ
    nk = 3 * B_TQ
    r = lax.broadcasted_iota(jnp.int32, (B_TQ, nk), 0)
    c = lax.broadcasted_iota(jnp.int32, (B_TQ, nk), 1)
    kpos = start - B_TQ + c
    rel = c - B_TQ - r
    valid = (jnp.abs(rel) <= B_RADIUS) & (kpos >= 0) & (kpos < seq)
    bias = jnp.where(valid, 0.0, NEG_INF)
    ones = jnp.ones((nk, HEAD_DIM), jnp.bfloat16)
    kv_half = B_KV_HEADS * HEAD_DIM
    for kh in range(B_KV_HEADS):
        q_ref = q0_ref if kh < 2 else q1_ref
        z_ref = z0_ref if kh < 2 else z1_ref
        base = (kh % 2) * B_GROUP * HEAD_DIM
        ksl = slice(kh * HEAD_DIM, (kh + 1) * HEAD_DIM)
        vsl = slice(kv_half + kh * HEAD_DIM, kv_half + (kh + 1) * HEAD_DIM)
        heads = [slice(base + g * HEAD_DIM, base + (g + 1) * HEAD_DIM) for g in range(B_GROUP)]
        q = jnp.concatenate([q_ref[:, sl] for sl in heads], axis=0)
        k = jnp.concatenate([kvp_ref[:, ksl], kvc_ref[:, ksl], kvn_ref[:, ksl]], axis=0)
        v = jnp.concatenate([kvp_ref[:, vsl], kvc_ref[:, vsl], kvn_ref[:, vsl]], axis=0)
        s = lax.dot_general(q, k, (((1,), (1,)), ((), ())), preferred_element_type=jnp.float32)
        es, ms, sks = [], [], []
        for g in range(B_GROUP):
            sg = s[g * B_TQ:(g + 1) * B_TQ] + bias
            sk = sink_ref[kh * B_GROUP + g] * LOG2E
            m = jnp.maximum(jnp.max(sg, axis=1, keepdims=True), sk)
            es.append(jnp.exp2(sg - m).astype(jnp.bfloat16))
            ms.append(m)
            sks.append(sk)
        ov = jnp.dot(jnp.concatenate(es, axis=0), jnp.concatenate([v, ones], axis=1),
                     preferred_element_type=jnp.float32)
        for g in range(B_GROUP):
            og = ov[g * B_TQ:(g + 1) * B_TQ]
            den = og[:, HEAD_DIM:HEAD_DIM + 1] + jnp.exp2(sks[g] - ms[g])
            col = (kh * B_GROUP + g) * HEAD_DIM
            u_ref[:, col:col + HEAD_DIM] = (
                og[:, :HEAD_DIM] * (1.0 / den) * z_ref[:, heads[g]].astype(jnp.float32)).astype(u_ref.dtype)


def _attn_b(sink, pb, pz, batch, seq):
    nq = seq // B_TQ
    row = lambda slot: pl.BlockSpec((None, B_TQ, TN), lambda b, i: (slot, b * nq + i, 0))
    kv_prev = pl.BlockSpec((None, B_TQ, TN), lambda b, i: (2, b * nq + jnp.maximum(i - 1, 0), 0))
    kv_next = pl.BlockSpec((None, B_TQ, TN), lambda b, i: (2, b * nq + jnp.minimum(i + 1, nq - 1), 0))
    width = B_Q_HEADS * HEAD_DIM
    return pl.pallas_call(
        functools.partial(_attn_b_kernel, seq=seq),
        grid=(batch, nq),
        in_specs=[pl.BlockSpec(memory_space=pltpu.SMEM),
                  row(0), row(1), kv_prev, row(2), kv_next, row(1), row(2)],
        out_specs=pl.BlockSpec((B_TQ, width), lambda b, i: (b * nq + i, 0)),
        out_shape=jax.ShapeDtypeStruct((batch * seq, width), jnp.bfloat16),
        compiler_params=_params(("parallel", "parallel")),
        name="attn_b",
    )(sink, pb, pb, pb, pb, pb, pz, pz)


M_TQ = 256


def _attn_m_kernel(q_ref, k_ref, v_ref, z_ref, u_ref):
    for hh in range(M_HEADS):
        sl = slice(hh * M_HEAD_DIM, (hh + 1) * M_HEAD_DIM)
        s = lax.dot_general(q_ref[:, sl], k_ref[:, sl], (((1,), (1,)), ((), ())),
                            preferred_element_type=jnp.float32)
        m = jnp.max(s, axis=1, keepdims=True)
        e = jnp.exp2(s - m)
        den = jnp.sum(e, axis=1, keepdims=True)
        o = jnp.dot(e.astype(jnp.bfloat16), v_ref[:, sl], preferred_element_type=jnp.float32) * (1.0 / den)
        u_ref[:, sl] = (o * z_ref[:, sl].astype(jnp.float32)).astype(u_ref.dtype)


def _attn_m(pb, kv_mem, pz, batch, seq):
    nq = seq // M_TQ
    width = M_HEADS * M_HEAD_DIM
    return pl.pallas_call(
        _attn_m_kernel,
        grid=(batch, nq),
        in_specs=[pl.BlockSpec((None, M_TQ, TN), lambda b, i: (3, b * nq + i, 0)),
                  pl.BlockSpec((None, MEM_LEN, TN), lambda b, i: (0, b, 0)),
                  pl.BlockSpec((None, MEM_LEN, TN), lambda b, i: (1, b, 0)),
                  pl.BlockSpec((None, M_TQ, TN), lambda b, i: (3, b * nq + i, 0))],
        out_specs=pl.BlockSpec((M_TQ, width), lambda b, i: (b * nq + i, 0)),
        out_shape=jax.ShapeDtypeStruct((batch * seq, width), jnp.bfloat16),
        compiler_params=_params(("parallel", "parallel")),
        name="attn_m",
    )(pb, kv_mem, kv_mem, pz)


def _branch_kernel(ua_ref, ub_ref, um_ref, w_ref, ga_ref, gb_ref, gm_ref, u_ref):
    a_hi = A_WIDTH
    b_hi = A_WIDTH + B_Q_HEADS * HEAD_DIM
    dot = functools.partial(jnp.dot, preferred_element_type=jnp.float32)
    acc = ga_ref[...].astype(jnp.float32) * dot(ua_ref[...], w_ref[:a_hi, :])
    acc += gb_ref[...].astype(jnp.float32) * dot(ub_ref[...], w_ref[a_hi:b_hi, :])
    acc += gm_ref[...].astype(jnp.float32) * dot(um_ref[...], w_ref[b_hi:, :])
    u_ref[...] = acc.astype(u_ref.dtype)


def _branch(ua, ub, um, w_branch, pg, tm=512):
    m = ua.shape[0]
    k = w_branch.shape[0]
    n_tiles = D_MODEL // TN
    act = lambda width: pl.BlockSpec((tm, width), lambda i, j: (i, 0))
    gate = lambda br: pl.BlockSpec((None, tm, TN), lambda i, j: (br * n_tiles + j, i, 0))
    return pl.pallas_call(
        _branch_kernel,
        grid=(m // tm, n_tiles),
        in_specs=[act(ua.shape[1]), act(ub.shape[1]), act(um.shape[1]),
                  pl.BlockSpec((k, TN), lambda i, j: (0, j)),
                  gate(0), gate(1), gate(2)],
        out_specs=pl.BlockSpec((tm, TN), lambda i, j: (i, j)),
        out_shape=jax.ShapeDtypeStruct((m, D_MODEL), jnp.bfloat16),
        compiler_params=_params(("parallel", "arbitrary")),
        name="branch_proj",
    )(ua, ub, um, w_branch, pg, pg, pg)


def _out_kernel(u_ref, w_ref, x_ref, g_ref, y_ref, ssq_ref, *, tn):
    j = pl.program_id(1)
    n_tiles = y_ref.shape[1] // tn
    y = x_ref[...] + jnp.dot(u_ref[...], w_ref[...], preferred_element_type=jnp.float32)
    part = jnp.sum(y * y, axis=1, keepdims=True)

    @pl.when(j == 0)
    def _():
        ssq_ref[...] = part

    @pl.when(j > 0)
    def _():
        ssq_ref[...] += part

    for t in range(n_tiles):
        @pl.when(j == t)
        def _():
            y_ref[:, t * tn:(t + 1) * tn] = y

    @pl.when(j == n_tiles - 1)
    def _():
        inv = lax.rsqrt(ssq_ref[...] / y_ref.shape[1] + NORM_EPS)
        for t in range(n_tiles):
            sl = slice(t * tn, (t + 1) * tn)
            y_ref[:, sl] = (y_ref[:, sl] * inv) * g_ref[:, sl]


def _out_proj(u, w_out, x, g_final, tm=512):
    m, k = u.shape
    n = x.shape[1]
    tn = min(TN, n)
    return pl.pallas_call(
        functools.partial(_out_kernel, tn=tn),
        grid=(m // tm, n // tn),
        in_specs=[pl.BlockSpec((tm, k), lambda i, j: (i, 0)),
                  pl.BlockSpec((k, tn), lambda i, j: (0, j)),
                  pl.BlockSpec((tm, tn), lambda i, j: (i, j)),
                  pl.BlockSpec((1, n), lambda i, j: (0, 0))],
        out_specs=pl.BlockSpec((tm, n), lambda i, j: (i, 0)),
        out_shape=jax.ShapeDtypeStruct((m, n), jnp.float32),
        scratch_shapes=[pltpu.VMEM((tm, 1), jnp.float32)],
        compiler_params=_params(("parallel", "arbitrary")),
        name="out_proj",
    )(u, w_out, x, g_final.reshape(1, n))


def _rope_tables(seq):
    inv_freq = ROPE_THETA ** (-jnp.arange(0, HEAD_DIM, 2, dtype=jnp.float32) / HEAD_DIM)
    ang = jnp.arange(seq, dtype=jnp.float32)[:, None] * inv_freq[None, :]
    cos, sin = jnp.cos(ang), jnp.sin(ang)
    cos, sin = jnp.concatenate([cos, cos], axis=1), jnp.concatenate([-sin, sin], axis=1)
    q_scale = HEAD_DIM ** -0.5 * LOG2E
    return cos, sin, cos * q_scale, sin * q_scale


def _trunk(x, mem, g_norm, w_in, sink, g_mem, w_mem_kv, w_branch, w_out, g_final):
    batch, seq, d = x.shape
    rows = batch * seq
    x2 = x.reshape(rows, d)
    rope = _rope_tables(seq)

    h = _rmsnorm(x2, g_norm, jnp.bfloat16)
    flat = lambda p: p.reshape(p.shape[0], rows, TN)
    pb = flat(_proj(h, w_in, B_TILES, ["rope_q", "rope_q", "rope_half", "scale_m"], batch=batch, tm=1024,
                    rope=rope, name="proj_b"))
    pz = flat(_proj(h, w_in, Z_TILES, ["silu"] * 4, batch=batch, tm=1024, name="proj_z"))
    pg = flat(_proj(h, w_in, G_TILES, ["sigmoid"] * 12, batch=batch, tm=1024, name="proj_g"))

    hm = _rmsnorm(mem.reshape(batch * MEM_LEN, d), g_mem, jnp.bfloat16)
    kv_mem = _proj(hm, w_mem_kv, (0, 2), ["none"] * 2, batch=1, tm=batch * MEM_LEN, name="proj_mem")
    kv_mem = kv_mem.reshape(2, batch * MEM_LEN, TN)

    outs, lses = [], []
    for gi, (_, dil) in enumerate(DIL_CONFIGS):
        sd = seq // dil
        n = batch * dil
        tiles = (A_TILES[0] + gi, A_TILES[1], len(DIL_CONFIGS))
        qkv = _proj(h, w_in, tiles, ["rope_q", "rope", "none"], batch=batch, tm=1024, dil=dil,
                    rope=rope, name=f"proj_a{gi}")
        o, lse = _attn_a(qkv.reshape(3 * n, sd, A_WIDTH), n, sd, (0, n, 2 * n))
        outs.append(o.reshape(batch, dil, sd, A_WIDTH))
        lses.append(lse.reshape(batch, dil, sd, LSE_LANES))
    ua = _merge_a(outs, lses, pz, batch, seq)

    ub = _attn_b(sink, pb, pz, batch, seq)
    um = _attn_m(pb, kv_mem, pz, batch, seq)

    u = _branch(ua, ub, um, w_branch, pg)
    return _out_proj(u, w_out, x2, g_final).reshape(batch, seq, d)


def kernel(x_prompt, x_sample, mem_prompt, mem_sample, g_norm, w_in, attn_sink, g_mem, w_mem_kv, w_branch, w_out, g_final):
    bf = jnp.bfloat16
    w = (g_norm[0], w_in[0].astype(bf), attn_sink[0], g_mem[0], w_mem_kv[0].astype(bf),
         w_branch[0].astype(bf), w_out[0].astype(bf), g_final)
    return (_trunk(x_prompt, mem_prompt, *w), _trunk(x_sample, mem_sample, *w))
```

```python
import functools

import jax
import jax.numpy as jnp
from jax import lax
from jax.experimental import pallas as pl
from jax.experimental.pallas import tpu as pltpu

D_MODEL = 4096
HEAD_DIM = 128
ROPE_THETA = 10000.0
NORM_EPS = 1e-6
NEG_INF = -1e30
LOG2E = 1.4426950408889634
LN2 = 0.6931471805599453

DIL_CONFIGS = ((128, 1), (512, 4), (2048, 16))
A_HEADS = 8
A_WIDTH = A_HEADS * HEAD_DIM
A_RADIUS = 64

B_Q_HEADS = 16
B_KV_HEADS = 4
B_GROUP = B_Q_HEADS // B_KV_HEADS
B_RADIUS = 128

MEM_LEN = 256
M_HEADS = 4
M_HEAD_DIM = 256

TN = 1024
A_TILES = (0, 9)
B_TILES = (9, 13)
Z_TILES = (13, 17)
G_TILES = (17, 29)
ROT_Q, ROT_K, ROT_ID, ROT_M = range(4)

VMEM_LIMIT = 56 * 1024 * 1024


def _params(sem, vmem=VMEM_LIMIT):
    return pltpu.CompilerParams(dimension_semantics=sem, vmem_limit_bytes=vmem)


def _rmsnorm_kernel(x_ref, g_ref, o_ref):
    x = x_ref[...]
    ms = jnp.mean(x * x, axis=-1, keepdims=True)
    o_ref[...] = ((x * lax.rsqrt(ms + NORM_EPS)) * g_ref[...]).astype(o_ref.dtype)


def _rmsnorm(x, g, out_dtype, tm=256):
    m, d = x.shape
    return pl.pallas_call(
        _rmsnorm_kernel,
        grid=(m // tm,),
        in_specs=[pl.BlockSpec((tm, d), lambda i: (i, 0)),
                  pl.BlockSpec((1, d), lambda i: (0, 0))],
        out_specs=pl.BlockSpec((tm, d), lambda i: (i, 0)),
        out_shape=jax.ShapeDtypeStruct((m, d), out_dtype),
        compiler_params=_params(("parallel",)),
        name="rmsnorm",
    )(x, g.reshape(1, d))


def _sigmoid(x):
    return 0.5 * jnp.tanh(0.5 * x) + 0.5


def _proj_kernel(*refs, epi, dil):
    refs = list(refs)
    nat_ref = refs.pop() if dil > 1 else None
    o_ref = refs.pop()
    h_ref, w_ref = refs[:2]
    tm = h_ref.shape[0]
    acc = jnp.dot(h_ref[...], w_ref[...], preferred_element_type=jnp.float32)
    if epi == "none":
        o_ref[0] = acc.astype(o_ref.dtype)
    elif epi == "silu":
        o_ref[0] = (acc * _sigmoid(acc)).astype(o_ref.dtype)
    elif epi == "sigmoid":
        o_ref[0] = _sigmoid(acc).astype(o_ref.dtype)
    else:
        cos_lo, sin_lo, cos_hi, sin_hi = refs[2:]
        heads = [slice(hh * HEAD_DIM, (hh + 1) * HEAD_DIM) for hh in range(TN // HEAD_DIM)]
        for hh, sl in enumerate(heads):
            cos_ref, sin_ref = (cos_lo, sin_lo) if hh < len(heads) // 2 else (cos_hi, sin_hi)
            x = acc[:, sl]
            x = x * cos_ref[...] + pltpu.roll(x, HEAD_DIM // 2, 1) * sin_ref[...]
            if dil > 1:
                nat_ref[hh] = x
            else:
                o_ref[0, :, sl] = x.astype(o_ref.dtype)
        for r in range(dil if dil > 1 else 0):
            for hh, sl in enumerate(heads):
                o_ref[r, :, sl] = nat_ref[hh, pl.ds(r, tm // dil, stride=dil), :].astype(o_ref.dtype)


def _select(j, values):
    if len(set(values)) == 1:
        return values[0]
    out = values[-1]
    for t in range(len(values) - 2, -1, -1):
        out = jnp.where(j == t, values[t], out)
    return out


def _proj(h, w, tiles, epi, *, batch, tm, dil=1, tabs=None, sets=None, name):
    m, k = h.shape
    seq = m // batch
    col_tiles = range(*tiles)
    lo, step, nt = col_tiles.start, col_tiles.step, len(col_tiles)
    assert seq % tm == 0 and tm % dil == 0 and (dil == 1 or epi == "rot")
    bps = seq // tm
    in_specs = [pl.BlockSpec((tm, k), lambda i, j: (i, 0)),
                pl.BlockSpec((k, TN), lambda i, j: (0, lo + step * j))]
    args = [h, w]
    if epi == "rot":
        for half in sets:
            assert len(half) == nt
            spec = pl.BlockSpec((None, tm, HEAD_DIM),
                                lambda i, j, half=half: (_select(j, half), i % bps, 0))
            in_specs += [spec, spec]
            args += list(tabs)
    return pl.pallas_call(
        functools.partial(_proj_kernel, epi=epi, dil=dil),
        grid=(m // tm, nt),
        in_specs=in_specs,
        out_specs=pl.BlockSpec((None, None, dil, tm // dil, TN),
                               lambda i, j: (j, i // bps, 0, i % bps, 0)),
        out_shape=jax.ShapeDtypeStruct((nt, batch, dil, seq // dil, TN), jnp.bfloat16),
        scratch_shapes=[pltpu.VMEM((TN // HEAD_DIM, tm, HEAD_DIM), jnp.float32)] if dil > 1 else [],
        compiler_params=_params(("parallel", "arbitrary")),
        name=name,
    )(*args)


A_TQ = 128
A_STEP = 256
A_HALO = 64
LSE_LANES = 128


def _window_rows(prev_ref, cur_ref, next_ref, lo, hi, cols):
    n = cur_ref.shape[0]
    parts = []
    if lo < 0:
        parts.append(prev_ref[:, cols])
    parts.append(cur_ref[max(lo, 0):min(hi, n), cols])
    if hi > n:
        parts.append(next_ref[:, cols])
    return jnp.concatenate(parts, axis=0) if len(parts) > 1 else parts[0]


def _band_bias(tq, halo, radius, start, seq):
    nk = tq + 2 * halo
    r = lax.broadcasted_iota(jnp.int32, (tq, nk), 0)
    c = lax.broadcasted_iota(jnp.int32, (tq, nk), 1)
    kpos = start - halo + c
    valid = (jnp.abs(c - halo - r) <= radius) & (kpos >= 0) & (kpos < seq)
    return jnp.where(valid, 0.0, NEG_INF)


def _attn_a_kernel(q_ref, kp_ref, kc_ref, kn_ref, vp_ref, vc_ref, vn_ref, o_ref, lse_ref, *, seq):
    lane = lax.broadcasted_iota(jnp.int32, (A_TQ, LSE_LANES), 1)
    for b in range(A_STEP // A_TQ):
        rows = slice(b * A_TQ, (b + 1) * A_TQ)
        lo, hi = b * A_TQ - A_HALO, (b + 1) * A_TQ + A_HALO
        bias = _band_bias(A_TQ, A_HALO, A_RADIUS, pl.program_id(1) * A_STEP + b * A_TQ, seq)
        lse_all = jnp.zeros((A_TQ, LSE_LANES), jnp.float32)
        for hh in range(A_HEADS):
            sl = slice(hh * HEAD_DIM, (hh + 1) * HEAD_DIM)
            k = _window_rows(kp_ref, kc_ref, kn_ref, lo, hi, sl)
            v = _window_rows(vp_ref, vc_ref, vn_ref, lo, hi, sl)
            s = lax.dot_general(q_ref[rows, sl], k, (((1,), (1,)), ((), ())),
                                preferred_element_type=jnp.float32) + bias
            m = jnp.max(s, axis=1, keepdims=True)
            e = jnp.exp2(s - m)
            den = jnp.sum(e, axis=1, keepdims=True)
            o = jnp.dot(e.astype(jnp.bfloat16), v, preferred_element_type=jnp.float32)
            o_ref[rows, sl] = o * (1.0 / den)
            lse_all = jnp.where(lane == hh, m * LN2 + jnp.log(den), lse_all)
        lse_ref[rows, :] = lse_all


def _attn_a(qkv, n, seq, offs):
    qo, ko, vo = offs
    nh = seq // A_HALO
    ratio = A_STEP // A_HALO
    cur = lambda off: pl.BlockSpec((None, A_STEP, A_WIDTH), lambda s, i: (off + s, i, 0))
    prev = lambda off: pl.BlockSpec(
        (None, A_HALO, A_WIDTH), lambda s, i: (off + s, jnp.maximum(ratio * i - 1, 0), 0))
    nxt = lambda off: pl.BlockSpec(
        (None, A_HALO, A_WIDTH), lambda s, i: (off + s, jnp.minimum(ratio * (i + 1), nh - 1), 0))
    return pl.pallas_call(
        functools.partial(_attn_a_kernel, seq=seq),
        grid=(n, seq // A_STEP),
        in_specs=[cur(qo), prev(ko), cur(ko), nxt(ko), prev(vo), cur(vo), nxt(vo)],
        out_specs=[pl.BlockSpec((None, A_STEP, A_WIDTH), lambda s, i: (s, i, 0)),
                   pl.BlockSpec((None, A_STEP, LSE_LANES), lambda s, i: (s, i, 0))],
        out_shape=[jax.ShapeDtypeStruct((n, seq, A_WIDTH), jnp.float32),
                   jax.ShapeDtypeStruct((n, seq, LSE_LANES), jnp.float32)],
        compiler_params=_params(("parallel", "parallel")),
        name="attn_a",
    )(qkv, qkv, qkv, qkv, qkv, qkv, qkv)


def _merge_a_kernel(c0_ref, c1_ref, c2_ref, cl0_ref, cl1_ref, cl2_ref, z_ref, u_ref,
                    o1_ref, o2_ref, l1_ref, l2_ref):
    tm = u_ref.shape[0]
    o0_ref, l0_ref = c0_ref.at[0], cl0_ref.at[0]
    for c_ref, cl_ref, o_ref, l_ref in ((c1_ref, cl1_ref, o1_ref, l1_ref), (c2_ref, cl2_ref, o2_ref, l2_ref)):
        dil = c_ref.shape[0]
        for r in range(dil):
            rows = pl.ds(r, tm // dil, stride=dil)
            for hh in range(A_HEADS):
                o_ref[hh, rows, :] = c_ref[r, :, hh * HEAD_DIM:(hh + 1) * HEAD_DIM]
            l_ref[rows, :] = cl_ref[r]
    l0, l1, l2 = l0_ref[...], l1_ref[...], l2_ref[...]
    m = jnp.maximum(jnp.maximum(l0, l1), l2)
    e0, e1, e2 = jnp.exp(l0 - m), jnp.exp(l1 - m), jnp.exp(l2 - m)
    inv = 1.0 / (e0 + e1 + e2)
    a0, a1, a2 = e0 * inv, e1 * inv, e2 * inv
    for hh in range(A_HEADS):
        sl = slice(hh * HEAD_DIM, (hh + 1) * HEAD_DIM)
        col = slice(hh, hh + 1)
        o = a0[:, col] * o0_ref[:, sl] + a1[:, col] * o1_ref[hh] + a2[:, col] * o2_ref[hh]
        u_ref[:, sl] = (o * z_ref[:, sl].astype(jnp.float32)).astype(u_ref.dtype)


def _merge_a(outs, lses, pz, batch, seq, tm=512):
    bps = seq // tm
    cls = lambda dil, width: pl.BlockSpec((None, dil, tm // dil, width), lambda i: (i // bps, 0, i % bps, 0))
    dils = [d for _, d in DIL_CONFIGS]
    return pl.pallas_call(
        _merge_a_kernel,
        grid=(batch * bps,),
        in_specs=([cls(d, A_WIDTH) for d in dils] + [cls(d, LSE_LANES) for d in dils]
                  + [pl.BlockSpec((None, tm, TN), lambda i: (0, i, 0))]),
        out_specs=pl.BlockSpec((tm, A_WIDTH), lambda i: (i, 0)),
        out_shape=jax.ShapeDtypeStruct((batch * seq, A_WIDTH), jnp.bfloat16),
        scratch_shapes=[pltpu.VMEM((A_HEADS, tm, HEAD_DIM), jnp.float32)] * 2
                       + [pltpu.VMEM((tm, LSE_LANES), jnp.float32)] * 2,
        compiler_params=_params(("parallel",)),
        name="merge_a",
    )(*outs, *lses, pz)


B_TQ = 128
B_STEP = 256


def _attn_b_kernel(sink_ref, q0_ref, q1_ref, kvp_ref, kvc_ref, kvn_ref, z0_ref, z1_ref, u_ref, *, seq):
    ones = jnp.ones((3 * B_TQ, HEAD_DIM), jnp.bfloat16)
    kv_half = B_KV_HEADS * HEAD_DIM
    for b in range(B_STEP // B_TQ):
        rows = slice(b * B_TQ, (b + 1) * B_TQ)
        lo, hi = (b - 1) * B_TQ, (b + 2) * B_TQ
        bias = _band_bias(B_TQ, B_TQ, B_RADIUS, pl.program_id(1) * B_STEP + b * B_TQ, seq)
        for kh in range(B_KV_HEADS):
            q_ref = q0_ref if kh < 2 else q1_ref
            z_ref = z0_ref if kh < 2 else z1_ref
            base = (kh % 2) * B_GROUP * HEAD_DIM
            heads = [slice(base + g * HEAD_DIM, base + (g + 1) * HEAD_DIM) for g in range(B_GROUP)]
            k = _window_rows(kvp_ref, kvc_ref, kvn_ref, lo, hi, slice(kh * HEAD_DIM, (kh + 1) * HEAD_DIM))
            v = _window_rows(kvp_ref, kvc_ref, kvn_ref, lo, hi,
                             slice(kv_half + kh * HEAD_DIM, kv_half + (kh + 1) * HEAD_DIM))
            q = jnp.concatenate([q_ref[rows, sl] for sl in heads], axis=0)
            s = lax.dot_general(q, k, (((1,), (1,)), ((), ())), preferred_element_type=jnp.float32)
            es, ms, sks = [], [], []
            for g in range(B_GROUP):
                sg = s[g * B_TQ:(g + 1) * B_TQ] + bias
                sk = sink_ref[kh * B_GROUP + g] * LOG2E
                m = jnp.maximum(jnp.max(sg, axis=1, keepdims=True), sk)
                es.append(jnp.exp2(sg - m).astype(jnp.bfloat16))
                ms.append(m)
                sks.append(sk)
            ov = jnp.dot(jnp.concatenate(es, axis=0), jnp.concatenate([v, ones], axis=1),
                         preferred_element_type=jnp.float32)
            for g in range(B_GROUP):
                og = ov[g * B_TQ:(g + 1) * B_TQ]
                den = og[:, HEAD_DIM:HEAD_DIM + 1] + jnp.exp2(sks[g] - ms[g])
                col = (kh * B_GROUP + g) * HEAD_DIM
                u_ref[rows, col:col + HEAD_DIM] = (
                    og[:, :HEAD_DIM] * (1.0 / den) * z_ref[rows, heads[g]].astype(jnp.float32)
                ).astype(u_ref.dtype)


def _attn_b(sink, pb, pz, batch, seq):
    nq = seq // B_STEP
    nh = seq // B_TQ
    ratio = B_STEP // B_TQ
    row = lambda slot: pl.BlockSpec((None, B_STEP, TN), lambda b, i: (slot, b * nq + i, 0))
    kv_prev = pl.BlockSpec((None, B_TQ, TN), lambda b, i: (2, b * nh + jnp.maximum(ratio * i - 1, 0), 0))
    kv_next = pl.BlockSpec((None, B_TQ, TN),
                           lambda b, i: (2, b * nh + jnp.minimum(ratio * (i + 1), nh - 1), 0))
    width = B_Q_HEADS * HEAD_DIM
    return pl.pallas_call(
        functools.partial(_attn_b_kernel, seq=seq),
        grid=(batch, nq),
        in_specs=[pl.BlockSpec(memory_space=pltpu.SMEM),
                  row(0), row(1), kv_prev, row(2), kv_next, row(1), row(2)],
        out_specs=pl.BlockSpec((B_STEP, width), lambda b, i: (b * nq + i, 0)),
        out_shape=jax.ShapeDtypeStruct((batch * seq, width), jnp.bfloat16),
        compiler_params=_params(("parallel", "parallel")),
        name="attn_b",
    )(sink, pb, pb, pb, pb, pb, pz, pz)


M_TQ = 256


def _attn_m_kernel(q_ref, k_ref, v_ref, z_ref, u_ref):
    for hh in range(M_HEADS):
        sl = slice(hh * M_HEAD_DIM, (hh + 1) * M_HEAD_DIM)
        s = lax.dot_general(q_ref[:, sl], k_ref[:, sl], (((1,), (1,)), ((), ())),
                            preferred_element_type=jnp.float32)
        m = jnp.max(s, axis=1, keepdims=True)
        e = jnp.exp2(s - m)
        den = jnp.sum(e, axis=1, keepdims=True)
        o = jnp.dot(e.astype(jnp.bfloat16), v_ref[:, sl], preferred_element_type=jnp.float32) * (1.0 / den)
        u_ref[:, sl] = (o * z_ref[:, sl].astype(jnp.float32)).astype(u_ref.dtype)


def _attn_m(pb, kv_mem, pz, batch, seq):
    nq = seq // M_TQ
    width = M_HEADS * M_HEAD_DIM
    return pl.pallas_call(
        _attn_m_kernel,
        grid=(batch, nq),
        in_specs=[pl.BlockSpec((None, M_TQ, TN), lambda b, i: (3, b * nq + i, 0)),
                  pl.BlockSpec((None, MEM_LEN, TN), lambda b, i: (0, b, 0)),
                  pl.BlockSpec((None, MEM_LEN, TN), lambda b, i: (1, b, 0)),
                  pl.BlockSpec((None, M_TQ, TN), lambda b, i: (3, b * nq + i, 0))],
        out_specs=pl.BlockSpec((M_TQ, width), lambda b, i: (b * nq + i, 0)),
        out_shape=jax.ShapeDtypeStruct((batch * seq, width), jnp.bfloat16),
        compiler_params=_params(("parallel", "parallel")),
        name="attn_m",
    )(pb, kv_mem, kv_mem, pz)


def _branch_kernel(ua_ref, ub_ref, um_ref, w_ref, ga_ref, gb_ref, gm_ref, u_ref):
    a_hi = A_WIDTH
    b_hi = A_WIDTH + B_Q_HEADS * HEAD_DIM
    dot = functools.partial(jnp.dot, preferred_element_type=jnp.float32)
    acc = ga_ref[...].astype(jnp.float32) * dot(ua_ref[...], w_ref[:a_hi, :])
    acc += gb_ref[...].astype(jnp.float32) * dot(ub_ref[...], w_ref[a_hi:b_hi, :])
    acc += gm_ref[...].astype(jnp.float32) * dot(um_ref[...], w_ref[b_hi:, :])
    u_ref[...] = acc.astype(u_ref.dtype)


def _branch(ua, ub, um, w_branch, pg, tm=512):
    m = ua.shape[0]
    k = w_branch.shape[0]
    n_tiles = D_MODEL // TN
    act = lambda width: pl.BlockSpec((tm, width), lambda j, i: (i, 0))
    gate = lambda br: pl.BlockSpec((None, tm, TN), lambda j, i: (br * n_tiles + j, i, 0))
    return pl.pallas_call(
        _branch_kernel,
        grid=(n_tiles, m // tm),
        in_specs=[act(ua.shape[1]), act(ub.shape[1]), act(um.shape[1]),
                  pl.BlockSpec((k, TN), lambda j, i: (0, j)),
                  gate(0), gate(1), gate(2)],
        out_specs=pl.BlockSpec((tm, TN), lambda j, i: (i, j)),
        out_shape=jax.ShapeDtypeStruct((m, D_MODEL), jnp.bfloat16),
        compiler_params=_params(("parallel", "parallel")),
        name="branch_proj",
    )(ua, ub, um, w_branch, pg, pg, pg)


def _out_kernel(u_ref, w_ref, x_ref, g_ref, y_ref, ssq_ref, *, tn):
    j = pl.program_id(1)
    n_tiles = y_ref.shape[1] // tn
    y = x_ref[...] + jnp.dot(u_ref[...], w_ref[...], preferred_element_type=jnp.float32)
    part = jnp.sum(y * y, axis=1, keepdims=True)

    @pl.when(j == 0)
    def _():
        ssq_ref[...] = part

    @pl.when(j > 0)
    def _():
        ssq_ref[...] += part

    for t in range(n_tiles):
        @pl.when(j == t)
        def _():
            y_ref[:, t * tn:(t + 1) * tn] = y

    @pl.when(j == n_tiles - 1)
    def _():
        inv = lax.rsqrt(ssq_ref[...] / y_ref.shape[1] + NORM_EPS)
        for t in range(n_tiles):
            sl = slice(t * tn, (t + 1) * tn)
            y_ref[:, sl] = (y_ref[:, sl] * inv) * g_ref[:, sl]


def _out_proj(u, w_out, x, g_final, tm=512):
    m, k = u.shape
    n = x.shape[1]
    tn = min(TN, n)
    return pl.pallas_call(
        functools.partial(_out_kernel, tn=tn),
        grid=(m // tm, n // tn),
        in_specs=[pl.BlockSpec((tm, k), lambda i, j: (i, 0)),
                  pl.BlockSpec((k, tn), lambda i, j: (0, j)),
                  pl.BlockSpec((tm, tn), lambda i, j: (i, j)),
                  pl.BlockSpec((1, n), lambda i, j: (0, 0))],
        out_specs=pl.BlockSpec((tm, n), lambda i, j: (i, 0)),
        out_shape=jax.ShapeDtypeStruct((m, n), jnp.float32),
        scratch_shapes=[pltpu.VMEM((tm, 1), jnp.float32)],
        compiler_params=_params(("parallel", "arbitrary")),
        name="out_proj",
    )(u, w_out, x, g_final.reshape(1, n))


def _rope_tables(seq):
    inv_freq = ROPE_THETA ** (-jnp.arange(0, HEAD_DIM, 2, dtype=jnp.float32) / HEAD_DIM)
    ang = jnp.arange(seq, dtype=jnp.float32)[:, None] * inv_freq[None, :]
    cos, sin = jnp.cos(ang), jnp.sin(ang)
    cos, sin = jnp.concatenate([cos, cos], axis=1), jnp.concatenate([-sin, sin], axis=1)
    one, zero = jnp.ones_like(cos), jnp.zeros_like(sin)
    q_scale = HEAD_DIM ** -0.5 * LOG2E
    m_scale = M_HEAD_DIM ** -0.5 * LOG2E
    return (jnp.stack([cos * q_scale, cos, one, one * m_scale]),
            jnp.stack([sin * q_scale, sin, zero, zero]))


def _trunk(x, mem, g_norm, w_in, sink, g_mem, w_mem_kv, w_branch, w_out, g_final):
    batch, seq, d = x.shape
    rows = batch * seq
    x2 = x.reshape(rows, d)
    tabs = _rope_tables(seq)

    h = _rmsnorm(x2, g_norm, jnp.bfloat16)
    flat = lambda p: p.reshape(p.shape[0], rows, TN)
    b_sets = ((ROT_Q, ROT_Q, ROT_K, ROT_M), (ROT_Q, ROT_Q, ROT_ID, ROT_M))
    pb = flat(_proj(h, w_in, B_TILES, "rot", batch=batch, tm=1024, tabs=tabs, sets=b_sets, name="proj_b"))
    pz = flat(_proj(h, w_in, Z_TILES, "silu", batch=batch, tm=1024, name="proj_z"))
    pg = flat(_proj(h, w_in, G_TILES, "sigmoid", batch=batch, tm=1024, name="proj_g"))

    hm = _rmsnorm(mem.reshape(batch * MEM_LEN, d), g_mem, jnp.bfloat16)
    kv_mem = _proj(hm, w_mem_kv, (0, 2), "none", batch=1, tm=batch * MEM_LEN, name="proj_mem")
    kv_mem = kv_mem.reshape(2, batch * MEM_LEN, TN)

    outs, lses = [], []
    for gi, (_, dil) in enumerate(DIL_CONFIGS):
        sd = seq // dil
        n = batch * dil
        tiles = (A_TILES[0] + gi, A_TILES[1], len(DIL_CONFIGS))
        a_sets = ((ROT_Q, ROT_K, ROT_ID),) * 2
        qkv = _proj(h, w_in, tiles, "rot", batch=batch, tm=1024, dil=dil, tabs=tabs, sets=a_sets,
                    name=f"proj_a{gi}")
        o, lse = _attn_a(qkv.reshape(3 * n, sd, A_WIDTH), n, sd, (0, n, 2 * n))
        outs.append(o.reshape(batch, dil, sd, A_WIDTH))
        lses.append(lse.reshape(batch, dil, sd, LSE_LANES))
    ua = _merge_a(outs, lses, pz, batch, seq)

    ub = _attn_b(sink, pb, pz, batch, seq)
    um = _attn_m(pb, kv_mem, pz, batch, seq)

    u = _branch(ua, ub, um, w_branch, pg)
    return _out_proj(u, w_out, x2, g_final).reshape(batch, seq, d)


def kernel(x_prompt, x_sample, mem_prompt, mem_sample, g_norm, w_in, attn_sink, g_mem, w_mem_kv, w_branch, w_out, g_final):
    bf = jnp.bfloat16
    w = (g_norm[0], w_in[0].astype(bf), attn_sink[0], g_mem[0], w_mem_kv[0].astype(bf),
         w_branch[0].astype(bf), w_out[0].astype(bf), g_final)
    return (_trunk(x_prompt, mem_prompt, *w), _trunk(x_sample, mem_sample, *w))
```

```python
import functools

import jax
import jax.numpy as jnp
from jax import lax
from jax.experimental import pallas as pl
from jax.experimental.pallas import tpu as pltpu

D_MODEL = 4096
HEAD_DIM = 128
ROPE_THETA = 10000.0
NORM_EPS = 1e-6
NEG_INF = -1e30
LOG2E = 1.4426950408889634
LN2 = 0.6931471805599453

DIL_CONFIGS = ((128, 1), (512, 4), (2048, 16))
A_HEADS = 8
A_WIDTH = A_HEADS * HEAD_DIM
A_RADIUS = 64

B_Q_HEADS = 16
B_KV_HEADS = 4
B_GROUP = B_Q_HEADS // B_KV_HEADS
B_RADIUS = 128

MEM_LEN = 256
M_HEADS = 4
M_HEAD_DIM = 256

TN = 1024
A_TILES = (0, 9)
B_TILES = (9, 13)
Z_TILES = (13, 17)
G_TILES = (17, 29)
ROT_Q, ROT_K, ROT_ID, ROT_M = range(4)

VMEM_LIMIT = 56 * 1024 * 1024


def _params(sem, vmem=VMEM_LIMIT):
    return pltpu.CompilerParams(dimension_semantics=sem, vmem_limit_bytes=vmem)


def _rmsnorm_kernel(x_ref, g_ref, o_ref):
    x = x_ref[...]
    ms = jnp.mean(x * x, axis=-1, keepdims=True)
    o_ref[...] = ((x * lax.rsqrt(ms + NORM_EPS)) * g_ref[...]).astype(o_ref.dtype)


def _rmsnorm(x, g, out_dtype, tm=256):
    m, d = x.shape
    return pl.pallas_call(
        _rmsnorm_kernel,
        grid=(m // tm,),
        in_specs=[pl.BlockSpec((tm, d), lambda i: (i, 0)),
                  pl.BlockSpec((1, d), lambda i: (0, 0))],
        out_specs=pl.BlockSpec((tm, d), lambda i: (i, 0)),
        out_shape=jax.ShapeDtypeStruct((m, d), out_dtype),
        compiler_params=_params(("parallel",)),
        name="rmsnorm",
    )(x, g.reshape(1, d))


def _sigmoid(x):
    return 0.5 * jnp.tanh(0.5 * x) + 0.5


def _proj_kernel(*refs, epi, dil):
    refs = list(refs)
    nat_ref = refs.pop() if dil > 1 else None
    o_ref = refs.pop()
    h_ref, w_ref = refs[:2]
    tm = h_ref.shape[0]
    acc = jnp.dot(h_ref[...], w_ref[...], preferred_element_type=jnp.float32)
    if epi == "none":
        o_ref[0] = acc.astype(o_ref.dtype)
    elif epi == "silu":
        o_ref[0] = (acc * _sigmoid(acc)).astype(o_ref.dtype)
    elif epi == "sigmoid":
        o_ref[0] = _sigmoid(acc).astype(o_ref.dtype)
    else:
        cos_lo, sin_lo, cos_hi, sin_hi = refs[2:]
        heads = [slice(hh * HEAD_DIM, (hh + 1) * HEAD_DIM) for hh in range(TN // HEAD_DIM)]
        for hh, sl in enumerate(heads):
            cos_ref, sin_ref = (cos_lo, sin_lo) if hh < len(heads) // 2 else (cos_hi, sin_hi)
            x = acc[:, sl]
            x = x * cos_ref[...] + pltpu.roll(x, HEAD_DIM // 2, 1) * sin_ref[...]
            if dil > 1:
                nat_ref[hh] = x
            else:
                o_ref[0, :, sl] = x.astype(o_ref.dtype)
        for r in range(dil if dil > 1 else 0):
            for hh, sl in enumerate(heads):
                o_ref[r, :, sl] = nat_ref[hh, pl.ds(r, tm // dil, stride=dil), :].astype(o_ref.dtype)


def _select(j, values):
    if len(set(values)) == 1:
        return values[0]
    out = values[-1]
    for t in range(len(values) - 2, -1, -1):
        out = jnp.where(j == t, values[t], out)
    return out


def _proj(h, w, tiles, epi, *, batch, tm, dil=1, tabs=None, sets=None, name):
    m, k = h.shape
    seq = m // batch
    col_tiles = range(*tiles)
    lo, step, nt = col_tiles.start, col_tiles.step, len(col_tiles)
    assert seq % tm == 0 and tm % dil == 0 and (dil == 1 or epi == "rot")
    bps = seq // tm
    in_specs = [pl.BlockSpec((tm, k), lambda i, j: (i, 0)),
                pl.BlockSpec((k, TN), lambda i, j: (0, lo + step * j))]
    args = [h, w]
    if epi == "rot":
        for half in sets:
            assert len(half) == nt
            spec = pl.BlockSpec((None, tm, HEAD_DIM),
                                lambda i, j, half=half: (_select(j, half), i % bps, 0))
            in_specs += [spec, spec]
            args += list(tabs)
    return pl.pallas_call(
        functools.partial(_proj_kernel, epi=epi, dil=dil),
        grid=(m // tm, nt),
        in_specs=in_specs,
        out_specs=pl.BlockSpec((None, None, dil, tm // dil, TN),
                               lambda i, j: (j, i // bps, 0, i % bps, 0)),
        out_shape=jax.ShapeDtypeStruct((nt, batch, dil, seq // dil, TN), jnp.bfloat16),
        scratch_shapes=[pltpu.VMEM((TN // HEAD_DIM, tm, HEAD_DIM), jnp.float32)] if dil > 1 else [],
        compiler_params=_params(("parallel", "arbitrary")),
        name=name,
    )(*args)


A_TQ = 128
A_STEP = 256
A_HALO = 64
LSE_LANES = 128


def _window_rows(prev_ref, cur_ref, next_ref, lo, hi, cols):
    n = cur_ref.shape[0]
    parts = []
    if lo < 0:
        parts.append(prev_ref[:, cols])
    parts.append(cur_ref[max(lo, 0):min(hi, n), cols])
    if hi > n:
        parts.append(next_ref[:, cols])
    return jnp.concatenate(parts, axis=0) if len(parts) > 1 else parts[0]


def _band_bias(tq, halo, radius, start, seq):
    nk = tq + 2 * halo
    r = lax.broadcasted_iota(jnp.int32, (tq, nk), 0)
    c = lax.broadcasted_iota(jnp.int32, (tq, nk), 1)
    kpos = start - halo + c
    valid = (jnp.abs(c - halo - r) <= radius) & (kpos >= 0) & (kpos < seq)
    return jnp.where(valid, 0.0, NEG_INF)


def _attn_a_kernel(q_ref, kp_ref, kc_ref, kn_ref, vp_ref, vc_ref, vn_ref, o_ref, lse_ref, *, seq):
    lane = lax.broadcasted_iota(jnp.int32, (A_TQ, LSE_LANES), 1)
    for b in range(A_STEP // A_TQ):
        rows = slice(b * A_TQ, (b + 1) * A_TQ)
        lo, hi = b * A_TQ - A_HALO, (b + 1) * A_TQ + A_HALO
        bias = _band_bias(A_TQ, A_HALO, A_RADIUS, pl.program_id(1) * A_STEP + b * A_TQ, seq)
        lse_all = jnp.zeros((A_TQ, LSE_LANES), jnp.float32)
        for hh in range(A_HEADS):
            sl = slice(hh * HEAD_DIM, (hh + 1) * HEAD_DIM)
            k = _window_rows(kp_ref, kc_ref, kn_ref, lo, hi, sl)
            v = _window_rows(vp_ref, vc_ref, vn_ref, lo, hi, sl)
            s = lax.dot_general(q_ref[rows, sl], k, (((1,), (1,)), ((), ())),
                                preferred_element_type=jnp.float32) + bias
            m = jnp.max(s, axis=1, keepdims=True)
            e = jnp.exp2(s - m)
            den = jnp.sum(e, axis=1, keepdims=True)
            o = jnp.dot(e.astype(jnp.bfloat16), v, preferred_element_type=jnp.float32)
            o_ref[rows, sl] = o * (1.0 / den)
            lse_all = jnp.where(lane == hh, m * LN2 + jnp.log(den), lse_all)
        lse_ref[rows, :] = lse_all


def _attn_a(qkv, n, seq, offs):
    qo, ko, vo = offs
    nh = seq // A_HALO
    ratio = A_STEP // A_HALO
    cur = lambda off: pl.BlockSpec((None, A_STEP, A_WIDTH), lambda s, i: (off + s, i, 0))
    prev = lambda off: pl.BlockSpec(
        (None, A_HALO, A_WIDTH), lambda s, i: (off + s, jnp.maximum(ratio * i - 1, 0), 0))
    nxt = lambda off: pl.BlockSpec(
        (None, A_HALO, A_WIDTH), lambda s, i: (off + s, jnp.minimum(ratio * (i + 1), nh - 1), 0))
    return pl.pallas_call(
        functools.partial(_attn_a_kernel, seq=seq),
        grid=(n, seq // A_STEP),
        in_specs=[cur(qo), prev(ko), cur(ko), nxt(ko), prev(vo), cur(vo), nxt(vo)],
        out_specs=[pl.BlockSpec((None, A_STEP, A_WIDTH), lambda s, i: (s, i, 0)),
                   pl.BlockSpec((None, A_STEP, LSE_LANES), lambda s, i: (s, i, 0))],
        out_shape=[jax.ShapeDtypeStruct((n, seq, A_WIDTH), jnp.float32),
                   jax.ShapeDtypeStruct((n, seq, LSE_LANES), jnp.float32)],
        compiler_params=_params(("parallel", "parallel")),
        name="attn_a",
    )(qkv, qkv, qkv, qkv, qkv, qkv, qkv)


def _merge_a_kernel(c0_ref, c1_ref, c2_ref, cl0_ref, cl1_ref, cl2_ref, z_ref, u_ref,
                    o1_ref, o2_ref, l1_ref, l2_ref):
    tm = u_ref.shape[0]
    o0_ref, l0_ref = c0_ref.at[0], cl0_ref.at[0]
    for c_ref, cl_ref, o_ref, l_ref in ((c1_ref, cl1_ref, o1_ref, l1_ref), (c2_ref, cl2_ref, o2_ref, l2_ref)):
        dil = c_ref.shape[0]
        for r in range(dil):
            rows = pl.ds(r, tm // dil, stride=dil)
            for hh in range(A_HEADS):
                o_ref[hh, rows, :] = c_ref[r, :, hh * HEAD_DIM:(hh + 1) * HEAD_DIM]
            l_ref[rows, :] = cl_ref[r]
    l0, l1, l2 = l0_ref[...], l1_ref[...], l2_ref[...]
    m = jnp.maximum(jnp.maximum(l0, l1), l2)
    e0, e1, e2 = jnp.exp(l0 - m), jnp.exp(l1 - m), jnp.exp(l2 - m)
    inv = 1.0 / (e0 + e1 + e2)
    a0, a1, a2 = e0 * inv, e1 * inv, e2 * inv
    for hh in range(A_HEADS):
        sl = slice(hh * HEAD_DIM, (hh + 1) * HEAD_DIM)
        col = slice(hh, hh + 1)
        o = a0[:, col] * o0_ref[:, sl] + a1[:, col] * o1_ref[hh] + a2[:, col] * o2_ref[hh]
        u_ref[:, sl] = (o * z_ref[:, sl].astype(jnp.float32)).astype(u_ref.dtype)


def _merge_a(outs, lses, pz, batch, seq, tm=512):
    bps = seq // tm
    cls = lambda dil, width: pl.BlockSpec((None, dil, tm // dil, width), lambda i: (i // bps, 0, i % bps, 0))
    dils = [d for _, d in DIL_CONFIGS]
    return pl.pallas_call(
        _merge_a_kernel,
        grid=(batch * bps,),
        in_specs=([cls(d, A_WIDTH) for d in dils] + [cls(d, LSE_LANES) for d in dils]
                  + [pl.BlockSpec((None, tm, TN), lambda i: (0, i, 0))]),
        out_specs=pl.BlockSpec((tm, A_WIDTH), lambda i: (i, 0)),
        out_shape=jax.ShapeDtypeStruct((batch * seq, A_WIDTH), jnp.bfloat16),
        scratch_shapes=[pltpu.VMEM((A_HEADS, tm, HEAD_DIM), jnp.float32)] * 2
                       + [pltpu.VMEM((tm, LSE_LANES), jnp.float32)] * 2,
        compiler_params=_params(("parallel",)),
        name="merge_a",
    )(*outs, *lses, pz)


B_TQ = 128
B_STEP = 256


def _attn_b_kernel(sink_ref, q0_ref, q1_ref, kvp_ref, kvc_ref, kvn_ref, z0_ref, z1_ref, u_ref, *, seq):
    ones = jnp.ones((3 * B_TQ, HEAD_DIM), jnp.bfloat16)
    kv_half = B_KV_HEADS * HEAD_DIM
    for b in range(B_STEP // B_TQ):
        rows = slice(b * B_TQ, (b + 1) * B_TQ)
        lo, hi = (b - 1) * B_TQ, (b + 2) * B_TQ
        bias = _band_bias(B_TQ, B_TQ, B_RADIUS, pl.program_id(1) * B_STEP + b * B_TQ, seq)
        for kh in range(B_KV_HEADS):
            q_ref = q0_ref if kh < 2 else q1_ref
            z_ref = z0_ref if kh < 2 else z1_ref
            base = (kh % 2) * B_GROUP * HEAD_DIM
            heads = [slice(base + g * HEAD_DIM, base + (g + 1) * HEAD_DIM) for g in range(B_GROUP)]
            k = _window_rows(kvp_ref, kvc_ref, kvn_ref, lo, hi, slice(kh * HEAD_DIM, (kh + 1) * HEAD_DIM))
            v = _window_rows(kvp_ref, kvc_ref, kvn_ref, lo, hi,
                             slice(kv_half + kh * HEAD_DIM, kv_half + (kh + 1) * HEAD_DIM))
            q = jnp.concatenate([q_ref[rows, sl] for sl in heads], axis=0)
            s = lax.dot_general(q, k, (((1,), (1,)), ((), ())), preferred_element_type=jnp.float32)
            es, ms, sks = [], [], []
            for g in range(B_GROUP):
                sg = s[g * B_TQ:(g + 1) * B_TQ] + bias
                sk = sink_ref[kh * B_GROUP + g] * LOG2E
                m = jnp.maximum(jnp.max(sg, axis=1, keepdims=True), sk)
                es.append(jnp.exp2(sg - m).astype(jnp.bfloat16))
                ms.append(m)
                sks.append(sk)
            ov = jnp.dot(jnp.concatenate(es, axis=0), jnp.concatenate([v, ones], axis=1),
                         preferred_element_type=jnp.float32)
            for g in range(B_GROUP):
                og = ov[g * B_TQ:(g + 1) * B_TQ]
                den = og[:, HEAD_DIM:HEAD_DIM + 1] + jnp.exp2(sks[g] - ms[g])
                col = (kh * B_GROUP + g) * HEAD_DIM
                u_ref[rows, col:col + HEAD_DIM] = (
                    og[:, :HEAD_DIM] * (1.0 / den) * z_ref[rows, heads[g]].astype(jnp.float32)
                ).astype(u_ref.dtype)


def _attn_b(sink, pb, pz, batch, seq):
    nq = seq // B_STEP
    nh = seq // B_TQ
    ratio = B_STEP // B_TQ
    row = lambda slot: pl.BlockSpec((None, B_STEP, TN), lambda b, i: (slot, b * nq + i, 0))
    kv_prev = pl.BlockSpec((None, B_TQ, TN), lambda b, i: (2, b * nh + jnp.maximum(ratio * i - 1, 0), 0))
    kv_next = pl.BlockSpec((None, B_TQ, TN),
                           lambda b, i: (2, b * nh + jnp.minimum(ratio * (i + 1), nh - 1), 0))
    width = B_Q_HEADS * HEAD_DIM
    return pl.pallas_call(
        functools.partial(_attn_b_kernel, seq=seq),
        grid=(batch, nq),
        in_specs=[pl.BlockSpec(memory_space=pltpu.SMEM),
                  row(0), row(1), kv_prev, row(2), kv_next, row(1), row(2)],
        out_specs=pl.BlockSpec((B_STEP, width), lambda b, i: (b * nq + i, 0)),
        out_shape=jax.ShapeDtypeStruct((batch * seq, width), jnp.bfloat16),
        compiler_params=_params(("parallel", "parallel")),
        name="attn_b",
    )(sink, pb, pb, pb, pb, pb, pz, pz)


M_TQ = 256


def _attn_m_kernel(q_ref, k_ref, v_ref, z_ref, u_ref):
    for hh in range(M_HEADS):
        sl = slice(hh * M_HEAD_DIM, (hh + 1) * M_HEAD_DIM)
        s = lax.dot_general(q_ref[:, sl], k_ref[:, sl], (((1,), (1,)), ((), ())),
                            preferred_element_type=jnp.float32)
        m = jnp.max(s, axis=1, keepdims=True)
        e = jnp.exp2(s - m)
        den = jnp.sum(e, axis=1, keepdims=True)
        o = jnp.dot(e.astype(jnp.bfloat16), v_ref[:, sl], preferred_element_type=jnp.float32) * (1.0 / den)
        u_ref[:, sl] = (o * z_ref[:, sl].astype(jnp.float32)).astype(u_ref.dtype)


def _attn_m(pb, kv_mem, pz, batch, seq):
    nq = seq // M_TQ
    width = M_HEADS * M_HEAD_DIM
    return pl.pallas_call(
        _attn_m_kernel,
        grid=(batch, nq),
        in_specs=[pl.BlockSpec((None, M_TQ, TN), lambda b, i: (3, b * nq + i, 0)),
                  pl.BlockSpec((None, MEM_LEN, TN), lambda b, i: (0, b, 0)),
                  pl.BlockSpec((None, MEM_LEN, TN), lambda b, i: (1, b, 0)),
                  pl.BlockSpec((None, M_TQ, TN), lambda b, i: (3, b * nq + i, 0))],
        out_specs=pl.BlockSpec((M_TQ, width), lambda b, i: (b * nq + i, 0)),
        out_shape=jax.ShapeDtypeStruct((batch * seq, width), jnp.bfloat16),
        compiler_params=_params(("parallel", "parallel")),
        name="attn_m",
    )(pb, kv_mem, kv_mem, pz)


def _branch_kernel(ua_ref, ub_ref, um_ref, w_ref, ga_ref, gb_ref, gm_ref, u_ref):
    a_hi = A_WIDTH
    b_hi = A_WIDTH + B_Q_HEADS * HEAD_DIM
    dot = functools.partial(jnp.dot, preferred_element_type=jnp.float32)
    acc = ga_ref[...].astype(jnp.float32) * dot(ua_ref[...], w_ref[:a_hi, :])
    acc += gb_ref[...].astype(jnp.float32) * dot(ub_ref[...], w_ref[a_hi:b_hi, :])
    acc += gm_ref[...].astype(jnp.float32) * dot(um_ref[...], w_ref[b_hi:, :])
    u_ref[...] = acc.astype(u_ref.dtype)


def _branch(ua, ub, um, w_branch, pg, tm=512):
    m = ua.shape[0]
    k = w_branch.shape[0]
    n_tiles = D_MODEL // TN
    act = lambda width: pl.BlockSpec((tm, width), lambda j, i: (i, 0))
    gate = lambda br: pl.BlockSpec((None, tm, TN), lambda j, i: (br * n_tiles + j, i, 0))
    return pl.pallas_call(
        _branch_kernel,
        grid=(n_tiles, m // tm),
        in_specs=[act(ua.shape[1]), act(ub.shape[1]), act(um.shape[1]),
                  pl.BlockSpec((k, TN), lambda j, i: (0, j)),
                  gate(0), gate(1), gate(2)],
        out_specs=pl.BlockSpec((tm, TN), lambda j, i: (i, j)),
        out_shape=jax.ShapeDtypeStruct((m, D_MODEL), jnp.bfloat16),
        compiler_params=_params(("parallel", "parallel")),
        name="branch_proj",
    )(ua, ub, um, w_branch, pg, pg, pg)


def _out_kernel(u_ref, w_ref, x_ref, g_ref, y_ref, ssq_ref, *, tn):
    j = pl.program_id(1)
    n_tiles = y_ref.shape[1] // tn
    y = x_ref[...] + jnp.dot(u_ref[...], w_ref[...], preferred_element_type=jnp.float32)
    part = jnp.sum(y * y, axis=1, keepdims=True)

    @pl.when(j == 0)
    def _():
        ssq_ref[...] = part

    @pl.when(j > 0)
    def _():
        ssq_ref[...] += part

    for t in range(n_tiles):
        @pl.when(j == t)
        def _():
            y_ref[:, t * tn:(t + 1) * tn] = y

    @pl.when(j == n_tiles - 1)
    def _():
        inv = lax.rsqrt(ssq_ref[...] / y_ref.shape[1] + NORM_EPS)
        for t in range(n_tiles):
            sl = slice(t * tn, (t + 1) * tn)
            y_ref[:, sl] = (y_ref[:, sl] * inv) * g_ref[:, sl]


def _out_proj(u, w_out, x, g_final, tm=1024, tn=512):
    m, k = u.shape
    n = x.shape[1]
    tn = min(tn, n)
    return pl.pallas_call(
        functools.partial(_out_kernel, tn=tn),
        grid=(m // tm, n // tn),
        in_specs=[pl.BlockSpec((tm, k), lambda i, j: (i, 0)),
                  pl.BlockSpec((k, tn), lambda i, j: (0, j)),
                  pl.BlockSpec((tm, tn), lambda i, j: (i, j)),
                  pl.BlockSpec((1, n), lambda i, j: (0, 0))],
        out_specs=pl.BlockSpec((tm, n), lambda i, j: (i, 0), pipeline_mode=pl.Buffered(1)),
        out_shape=jax.ShapeDtypeStruct((m, n), jnp.float32),
        scratch_shapes=[pltpu.VMEM((tm, 1), jnp.float32)],
        compiler_params=_params(("parallel", "arbitrary")),
        name="out_proj",
    )(u, w_out, x, g_final.reshape(1, n))


def _rope_tables(seq):
    inv_freq = ROPE_THETA ** (-jnp.arange(0, HEAD_DIM, 2, dtype=jnp.float32) / HEAD_DIM)
    ang = jnp.arange(seq, dtype=jnp.float32)[:, None] * inv_freq[None, :]
    cos, sin = jnp.cos(ang), jnp.sin(ang)
    cos, sin = jnp.concatenate([cos, cos], axis=1), jnp.concatenate([-sin, sin], axis=1)
    one, zero = jnp.ones_like(cos), jnp.zeros_like(sin)
    q_scale = HEAD_DIM ** -0.5 * LOG2E
    m_scale = M_HEAD_DIM ** -0.5 * LOG2E
    return (jnp.stack([cos * q_scale, cos, one, one * m_scale]),
            jnp.stack([sin * q_scale, sin, zero, zero]))


def _trunk(x, mem, g_norm, w_in, sink, g_mem, w_mem_kv, w_branch, w_out, g_final):
    batch, seq, d = x.shape
    rows = batch * seq
    x2 = x.reshape(rows, d)
    tabs = _rope_tables(seq)

    h = _rmsnorm(x2, g_norm, jnp.bfloat16)
    flat = lambda p: p.reshape(p.shape[0], rows, TN)
    b_sets = ((ROT_Q, ROT_Q, ROT_K, ROT_M), (ROT_Q, ROT_Q, ROT_ID, ROT_M))
    pb = flat(_proj(h, w_in, B_TILES, "rot", batch=batch, tm=1024, tabs=tabs, sets=b_sets, name="proj_b"))
    pz = flat(_proj(h, w_in, Z_TILES, "silu", batch=batch, tm=1024, name="proj_z"))
    pg = flat(_proj(h, w_in, G_TILES, "sigmoid", batch=batch, tm=1024, name="proj_g"))

    hm = _rmsnorm(mem.reshape(batch * MEM_LEN, d), g_mem, jnp.bfloat16)
    kv_mem = _proj(hm, w_mem_kv, (0, 2), "none", batch=1, tm=batch * MEM_LEN, name="proj_mem")
    kv_mem = kv_mem.reshape(2, batch * MEM_LEN, TN)

    outs, lses = [], []
    for gi, (_, dil) in enumerate(DIL_CONFIGS):
        sd = seq // dil
        n = batch * dil
        tiles = (A_TILES[0] + gi, A_TILES[1], len(DIL_CONFIGS))
        a_sets = ((ROT_Q, ROT_K, ROT_ID),) * 2
        qkv = _proj(h, w_in, tiles, "rot", batch=batch, tm=1024, dil=dil, tabs=tabs, sets=a_sets,
                    name=f"proj_a{gi}")
        o, lse = _attn_a(qkv.reshape(3 * n, sd, A_WIDTH), n, sd, (0, n, 2 * n))
        outs.append(o.reshape(batch, dil, sd, A_WIDTH))
        lses.append(lse.reshape(batch, dil, sd, LSE_LANES))
    ua = _merge_a(outs, lses, pz, batch, seq)

    ub = _attn_b(sink, pb, pz, batch, seq)
    um = _attn_m(pb, kv_mem, pz, batch, seq)

    u = _branch(ua, ub, um, w_branch, pg)
    return _out_proj(u, w_out, x2, g_final).reshape(batch, seq, d)


def kernel(x_prompt, x_sample, mem_prompt, mem_sample, g_norm, w_in, attn_sink, g_mem, w_mem_kv, w_branch, w_out, g_final):
    bf = jnp.bfloat16
    w = (g_norm[0], w_in[0].astype(bf), attn_sink[0], g_mem[0], w_mem_kv[0].astype(bf),
         w_branch[0].astype(bf), w_out[0].astype(bf), g_final)
    return (_trunk(x_prompt, mem_prompt, *w), _trunk(x_sample, mem_sample, *w))
```

```python
import functools

import jax
import jax.numpy as jnp
from jax import lax
from jax.experimental import pallas as pl
from jax.experimental.pallas import tpu as pltpu

D_MODEL = 4096
HEAD_DIM = 128
ROPE_THETA = 10000.0
NORM_EPS = 1e-6
NEG_INF = -1e30
LOG2E = 1.4426950408889634
LN2 = 0.6931471805599453

DIL_CONFIGS = ((128, 1), (512, 4), (2048, 16))
A_HEADS = 8
A_WIDTH = A_HEADS * HEAD_DIM
A_RADIUS = 64

B_Q_HEADS = 16
B_KV_HEADS = 4
B_GROUP = B_Q_HEADS // B_KV_HEADS
B_RADIUS = 128

MEM_LEN = 256
M_HEADS = 4
M_HEAD_DIM = 256

TN = 1024
A_TILES = (0, 9)
B_TILES = (9, 13)
Z_TILES = (13, 17)
G_TILES = (17, 29)
ROT_Q, ROT_K, ROT_ID, ROT_M = range(4)

VMEM_LIMIT = 56 * 1024 * 1024


def _params(sem, vmem=VMEM_LIMIT):
    return pltpu.CompilerParams(dimension_semantics=sem, vmem_limit_bytes=vmem)


def _rmsnorm_kernel(x_ref, g_ref, o_ref):
    x = x_ref[...]
    ms = jnp.mean(x * x, axis=-1, keepdims=True)
    o_ref[...] = ((x * lax.rsqrt(ms + NORM_EPS)) * g_ref[...]).astype(o_ref.dtype)


def _rmsnorm(x, g, out_dtype, tm=256):
    m, d = x.shape
    return pl.pallas_call(
        _rmsnorm_kernel,
        grid=(m // tm,),
        in_specs=[pl.BlockSpec((tm, d), lambda i: (i, 0)),
                  pl.BlockSpec((1, d), lambda i: (0, 0))],
        out_specs=pl.BlockSpec((tm, d), lambda i: (i, 0)),
        out_shape=jax.ShapeDtypeStruct((m, d), out_dtype),
        compiler_params=_params(("parallel",)),
        name="rmsnorm",
    )(x, g.reshape(1, d))


def _sigmoid(x):
    return 0.5 * jnp.tanh(0.5 * x) + 0.5


def _proj_kernel(*refs, epi, dil, cast):
    refs = list(refs)
    nat_ref = refs.pop() if dil > 1 else None
    if cast:
        cast_out_ref = refs.pop()
        o_ref = refs.pop()
        cast_out_ref[...] = refs.pop()[...].astype(cast_out_ref.dtype)
    else:
        o_ref = refs.pop()
    h_ref, w_ref = refs[:2]
    tm = h_ref.shape[0]
    acc = jnp.dot(h_ref[...], w_ref[...], preferred_element_type=jnp.float32)
    if epi == "none":
        o_ref[0] = acc.astype(o_ref.dtype)
    elif epi == "silu":
        o_ref[0] = (acc * _sigmoid(acc)).astype(o_ref.dtype)
    elif epi == "sigmoid":
        o_ref[0] = _sigmoid(acc).astype(o_ref.dtype)
    else:
        cos_lo, sin_lo, cos_hi, sin_hi = refs[2:6]
        heads = [slice(hh * HEAD_DIM, (hh + 1) * HEAD_DIM) for hh in range(TN // HEAD_DIM)]
        for hh, sl in enumerate(heads):
            cos_ref, sin_ref = (cos_lo, sin_lo) if hh < len(heads) // 2 else (cos_hi, sin_hi)
            x = acc[:, sl]
            x = x * cos_ref[...] + pltpu.roll(x, HEAD_DIM // 2, 1) * sin_ref[...]
            if dil > 1:
                nat_ref[hh] = x
            else:
                o_ref[0, :, sl] = x.astype(o_ref.dtype)
        for r in range(dil if dil > 1 else 0):
            for hh, sl in enumerate(heads):
                o_ref[r, :, sl] = nat_ref[hh, pl.ds(r, tm // dil, stride=dil), :].astype(o_ref.dtype)


def _select(j, values):
    if len(set(values)) == 1:
        return values[0]
    out = values[-1]
    for t in range(len(values) - 2, -1, -1):
        out = jnp.where(j == t, values[t], out)
    return out


CAST_ROWS = 512


def _proj(h, w, tiles, epi, *, batch, tm, dil=1, tabs=None, sets=None, cast=None, name):
    m, k = h.shape
    seq = m // batch
    col_tiles = range(*tiles)
    lo, step, nt = col_tiles.start, col_tiles.step, len(col_tiles)
    assert seq % tm == 0 and tm % dil == 0 and (dil == 1 or epi == "rot")
    bps = seq // tm
    in_specs = [pl.BlockSpec((tm, k), lambda i, j: (i, 0)),
                pl.BlockSpec((k, TN), lambda i, j: (0, lo + step * j))]
    args = [h, w]
    if epi == "rot":
        for half in sets:
            assert len(half) == nt
            spec = pl.BlockSpec((None, tm, HEAD_DIM),
                                lambda i, j, half=half: (_select(j, half), i % bps, 0))
            in_specs += [spec, spec]
            args += list(tabs)
    out_specs = [pl.BlockSpec((None, None, dil, tm // dil, TN), lambda i, j: (j, i // bps, 0, i % bps, 0))]
    out_shape = [jax.ShapeDtypeStruct((nt, batch, dil, seq // dil, TN), jnp.bfloat16)]
    if cast is not None:
        src, src_tiles = cast
        src_tiles = range(*src_tiles)
        cr = min(CAST_ROWS, src.shape[0])
        cw = min(TN, src.shape[1])
        row_blocks = src.shape[0] // cr
        n_chunks = row_blocks * len(src_tiles)
        assert n_chunks <= (m // tm) * nt, "not enough grid steps to cast this weight group"
        chunk = lambda i, j: jnp.minimum(i * nt + j, n_chunks - 1)
        in_specs.append(pl.BlockSpec(
            (cr, cw), lambda i, j: (chunk(i, j) % row_blocks,
                                    src_tiles.start + src_tiles.step * (chunk(i, j) // row_blocks))))
        args.append(src)
        out_specs.append(pl.BlockSpec(
            (cr, cw), lambda i, j: (chunk(i, j) % row_blocks, chunk(i, j) // row_blocks)))
        out_shape.append(jax.ShapeDtypeStruct((src.shape[0], len(src_tiles) * cw), jnp.bfloat16))
    outs = pl.pallas_call(
        functools.partial(_proj_kernel, epi=epi, dil=dil, cast=cast is not None),
        grid=(m // tm, nt),
        in_specs=in_specs,
        out_specs=out_specs,
        out_shape=out_shape,
        scratch_shapes=[pltpu.VMEM((TN // HEAD_DIM, tm, HEAD_DIM), jnp.float32)] if dil > 1 else [],
        compiler_params=_params(("arbitrary", "arbitrary")),
        name=name,
    )(*args)
    return outs if cast is not None else outs[0]


A_TQ = 128
A_STEP = 256
A_HALO = 64
LSE_LANES = 128


def _window_rows(prev_ref, cur_ref, next_ref, lo, hi, cols):
    n = cur_ref.shape[0]
    parts = []
    if lo < 0:
        parts.append(prev_ref[:, cols])
    parts.append(cur_ref[max(lo, 0):min(hi, n), cols])
    if hi > n:
        parts.append(next_ref[:, cols])
    return jnp.concatenate(parts, axis=0) if len(parts) > 1 else parts[0]


def _band_bias(tq, halo, radius, start, seq):
    nk = tq + 2 * halo
    r = lax.broadcasted_iota(jnp.int32, (tq, nk), 0)
    c = lax.broadcasted_iota(jnp.int32, (tq, nk), 1)
    kpos = start - halo + c
    valid = (jnp.abs(c - halo - r) <= radius) & (kpos >= 0) & (kpos < seq)
    return jnp.where(valid, 0.0, NEG_INF)


def _attn_a_kernel(q_ref, kp_ref, kc_ref, kn_ref, vp_ref, vc_ref, vn_ref, o_ref, lse_ref, *, seq):
    lane = lax.broadcasted_iota(jnp.int32, (A_TQ, LSE_LANES), 1)
    for b in range(A_STEP // A_TQ):
        rows = slice(b * A_TQ, (b + 1) * A_TQ)
        lo, hi = b * A_TQ - A_HALO, (b + 1) * A_TQ + A_HALO
        bias = _band_bias(A_TQ, A_HALO, A_RADIUS, pl.program_id(1) * A_STEP + b * A_TQ, seq)
        lse_all = jnp.zeros((A_TQ, LSE_LANES), jnp.float32)
        for hh in range(A_HEADS):
            sl = slice(hh * HEAD_DIM, (hh + 1) * HEAD_DIM)
            k = _window_rows(kp_ref, kc_ref, kn_ref, lo, hi, sl)
            v = _window_rows(vp_ref, vc_ref, vn_ref, lo, hi, sl)
            s = lax.dot_general(q_ref[rows, sl], k, (((1,), (1,)), ((), ())),
                                preferred_element_type=jnp.float32) + bias
            m = jnp.max(s, axis=1, keepdims=True)
            e = jnp.exp2(s - m)
            den = jnp.sum(e, axis=1, keepdims=True)
            o = jnp.dot(e.astype(jnp.bfloat16), v, preferred_element_type=jnp.float32)
            o_ref[rows, sl] = o * (1.0 / den)
            lse_all = jnp.where(lane == hh, m * LN2 + jnp.log(den), lse_all)
        lse_ref[rows, :] = lse_all


def _attn_a(qkv, n, seq, offs):
    qo, ko, vo = offs
    nh = seq // A_HALO
    ratio = A_STEP // A_HALO
    cur = lambda off: pl.BlockSpec((None, A_STEP, A_WIDTH), lambda s, i: (off + s, i, 0))
    prev = lambda off: pl.BlockSpec(
        (None, A_HALO, A_WIDTH), lambda s, i: (off + s, jnp.maximum(ratio * i - 1, 0), 0))
    nxt = lambda off: pl.BlockSpec(
        (None, A_HALO, A_WIDTH), lambda s, i: (off + s, jnp.minimum(ratio * (i + 1), nh - 1), 0))
    return pl.pallas_call(
        functools.partial(_attn_a_kernel, seq=seq),
        grid=(n, seq // A_STEP),
        in_specs=[cur(qo), prev(ko), cur(ko), nxt(ko), prev(vo), cur(vo), nxt(vo)],
        out_specs=[pl.BlockSpec((None, A_STEP, A_WIDTH), lambda s, i: (s, i, 0)),
                   pl.BlockSpec((None, A_STEP, LSE_LANES), lambda s, i: (s, i, 0))],
        out_shape=[jax.ShapeDtypeStruct((n, seq, A_WIDTH), jnp.float32),
                   jax.ShapeDtypeStruct((n, seq, LSE_LANES), jnp.float32)],
        compiler_params=_params(("parallel", "parallel")),
        name="attn_a",
    )(qkv, qkv, qkv, qkv, qkv, qkv, qkv)


def _merge_a_kernel(c0_ref, c1_ref, c2_ref, cl0_ref, cl1_ref, cl2_ref, z_ref, u_ref,
                    o1_ref, o2_ref, l1_ref, l2_ref):
    tm = u_ref.shape[0]
    o0_ref, l0_ref = c0_ref.at[0], cl0_ref.at[0]
    for c_ref, cl_ref, o_ref, l_ref in ((c1_ref, cl1_ref, o1_ref, l1_ref), (c2_ref, cl2_ref, o2_ref, l2_ref)):
        dil = c_ref.shape[0]
        for r in range(dil):
            rows = pl.ds(r, tm // dil, stride=dil)
            for hh in range(A_HEADS):
                o_ref[hh, rows, :] = c_ref[r, :, hh * HEAD_DIM:(hh + 1) * HEAD_DIM]
            l_ref[rows, :] = cl_ref[r]
    l0, l1, l2 = l0_ref[...], l1_ref[...], l2_ref[...]
    m = jnp.maximum(jnp.maximum(l0, l1), l2)
    e0, e1, e2 = jnp.exp(l0 - m), jnp.exp(l1 - m), jnp.exp(l2 - m)
    inv = 1.0 / (e0 + e1 + e2)
    a0, a1, a2 = e0 * inv, e1 * inv, e2 * inv
    for hh in range(A_HEADS):
        sl = slice(hh * HEAD_DIM, (hh + 1) * HEAD_DIM)
        col = slice(hh, hh + 1)
        o = a0[:, col] * o0_ref[:, sl] + a1[:, col] * o1_ref[hh] + a2[:, col] * o2_ref[hh]
        u_ref[:, sl] = (o * z_ref[:, sl].astype(jnp.float32)).astype(u_ref.dtype)


def _merge_a(outs, lses, pz, batch, seq, tm=512):
    bps = seq // tm
    cls = lambda dil, width: pl.BlockSpec((None, dil, tm // dil, width), lambda i: (i // bps, 0, i % bps, 0))
    dils = [d for _, d in DIL_CONFIGS]
    return pl.pallas_call(
        _merge_a_kernel,
        grid=(batch * bps,),
        in_specs=([cls(d, A_WIDTH) for d in dils] + [cls(d, LSE_LANES) for d in dils]
                  + [pl.BlockSpec((None, tm, TN), lambda i: (0, i, 0))]),
        out_specs=pl.BlockSpec((tm, A_WIDTH), lambda i: (i, 0)),
        out_shape=jax.ShapeDtypeStruct((batch * seq, A_WIDTH), jnp.bfloat16),
        scratch_shapes=[pltpu.VMEM((A_HEADS, tm, HEAD_DIM), jnp.float32)] * 2
                       + [pltpu.VMEM((tm, LSE_LANES), jnp.float32)] * 2,
        compiler_params=_params(("parallel",)),
        name="merge_a",
    )(*outs, *lses, pz)


B_TQ = 128
B_STEP = 256


def _attn_b_kernel(sink_ref, q0_ref, q1_ref, kvp_ref, kvc_ref, kvn_ref, z0_ref, z1_ref, u_ref, *, seq):
    ones = jnp.ones((3 * B_TQ, HEAD_DIM), jnp.bfloat16)
    kv_half = B_KV_HEADS * HEAD_DIM
    for b in range(B_STEP // B_TQ):
        rows = slice(b * B_TQ, (b + 1) * B_TQ)
        lo, hi = (b - 1) * B_TQ, (b + 2) * B_TQ
        bias = _band_bias(B_TQ, B_TQ, B_RADIUS, pl.program_id(1) * B_STEP + b * B_TQ, seq)
        for kh in range(B_KV_HEADS):
            q_ref = q0_ref if kh < 2 else q1_ref
            z_ref = z0_ref if kh < 2 else z1_ref
            base = (kh % 2) * B_GROUP * HEAD_DIM
            heads = [slice(base + g * HEAD_DIM, base + (g + 1) * HEAD_DIM) for g in range(B_GROUP)]
            k = _window_rows(kvp_ref, kvc_ref, kvn_ref, lo, hi, slice(kh * HEAD_DIM, (kh + 1) * HEAD_DIM))
            v = _window_rows(kvp_ref, kvc_ref, kvn_ref, lo, hi,
                             slice(kv_half + kh * HEAD_DIM, kv_half + (kh + 1) * HEAD_DIM))
            q = jnp.concatenate([q_ref[rows, sl] for sl in heads], axis=0)
            s = lax.dot_general(q, k, (((1,), (1,)), ((), ())), preferred_element_type=jnp.float32)
            es, ms, sks = [], [], []
            for g in range(B_GROUP):
                sg = s[g * B_TQ:(g + 1) * B_TQ] + bias
                sk = sink_ref[kh * B_GROUP + g] * LOG2E
                m = jnp.maximum(jnp.max(sg, axis=1, keepdims=True), sk)
                es.append(jnp.exp2(sg - m).astype(jnp.bfloat16))
                ms.append(m)
                sks.append(sk)
            ov = jnp.dot(jnp.concatenate(es, axis=0), jnp.concatenate([v, ones], axis=1),
                         preferred_element_type=jnp.float32)
            for g in range(B_GROUP):
                og = ov[g * B_TQ:(g + 1) * B_TQ]
                den = og[:, HEAD_DIM:HEAD_DIM + 1] + jnp.exp2(sks[g] - ms[g])
                col = (kh * B_GROUP + g) * HEAD_DIM
                u_ref[rows, col:col + HEAD_DIM] = (
                    og[:, :HEAD_DIM] * (1.0 / den) * z_ref[rows, heads[g]].astype(jnp.float32)
                ).astype(u_ref.dtype)


def _attn_b(sink, pb, pz, batch, seq):
    nq = seq // B_STEP
    nh = seq // B_TQ
    ratio = B_STEP // B_TQ
    row = lambda slot: pl.BlockSpec((None, B_STEP, TN), lambda b, i: (slot, b * nq + i, 0))
    kv_prev = pl.BlockSpec((None, B_TQ, TN), lambda b, i: (2, b * nh + jnp.maximum(ratio * i - 1, 0), 0))
    kv_next = pl.BlockSpec((None, B_TQ, TN),
                           lambda b, i: (2, b * nh + jnp.minimum(ratio * (i + 1), nh - 1), 0))
    width = B_Q_HEADS * HEAD_DIM
    return pl.pallas_call(
        functools.partial(_attn_b_kernel, seq=seq),
        grid=(batch, nq),
        in_specs=[pl.BlockSpec(memory_space=pltpu.SMEM),
                  row(0), row(1), kv_prev, row(2), kv_next, row(1), row(2)],
        out_specs=pl.BlockSpec((B_STEP, width), lambda b, i: (b * nq + i, 0)),
        out_shape=jax.ShapeDtypeStruct((batch * seq, width), jnp.bfloat16),
        compiler_params=_params(("parallel", "parallel")),
        name="attn_b",
    )(sink, pb, pb, pb, pb, pb, pz, pz)


M_TQ = 256


def _attn_m_kernel(q_ref, k_ref, v_ref, z_ref, u_ref):
    for hh in range(M_HEADS):
        sl = slice(hh * M_HEAD_DIM, (hh + 1) * M_HEAD_DIM)
        s = lax.dot_general(q_ref[:, sl], k_ref[:, sl], (((1,), (1,)), ((), ())),
                            preferred_element_type=jnp.float32)
        m = jnp.max(s, axis=1, keepdims=True)
        e = jnp.exp2(s - m)
        den = jnp.sum(e, axis=1, keepdims=True)
        o = jnp.dot(e.astype(jnp.bfloat16), v_ref[:, sl], preferred_element_type=jnp.float32) * (1.0 / den)
        u_ref[:, sl] = (o * z_ref[:, sl].astype(jnp.float32)).astype(u_ref.dtype)


def _attn_m(pb, kv_mem, pz, batch, seq):
    nq = seq // M_TQ
    width = M_HEADS * M_HEAD_DIM
    return pl.pallas_call(
        _attn_m_kernel,
        grid=(batch, nq),
        in_specs=[pl.BlockSpec((None, M_TQ, TN), lambda b, i: (3, b * nq + i, 0)),
                  pl.BlockSpec((None, MEM_LEN, TN), lambda b, i: (0, b, 0)),
                  pl.BlockSpec((None, MEM_LEN, TN), lambda b, i: (1, b, 0)),
                  pl.BlockSpec((None, M_TQ, TN), lambda b, i: (3, b * nq + i, 0))],
        out_specs=pl.BlockSpec((M_TQ, width), lambda b, i: (b * nq + i, 0)),
        out_shape=jax.ShapeDtypeStruct((batch * seq, width), jnp.bfloat16),
        compiler_params=_params(("parallel", "parallel")),
        name="attn_m",
    )(pb, kv_mem, kv_mem, pz)


def _branch_kernel(ua_ref, ub_ref, um_ref, w_ref, ga_ref, gb_ref, gm_ref, u_ref):
    a_hi = A_WIDTH
    b_hi = A_WIDTH + B_Q_HEADS * HEAD_DIM
    dot = functools.partial(jnp.dot, preferred_element_type=jnp.float32)
    acc = ga_ref[...].astype(jnp.float32) * dot(ua_ref[...], w_ref[:a_hi, :])
    acc += gb_ref[...].astype(jnp.float32) * dot(ub_ref[...], w_ref[a_hi:b_hi, :])
    acc += gm_ref[...].astype(jnp.float32) * dot(um_ref[...], w_ref[b_hi:, :])
    u_ref[...] = acc.astype(u_ref.dtype)


def _branch(ua, ub, um, w_branch, gates, tm=512):
    m = ua.shape[0]
    k = w_branch.shape[0]
    n_tiles = D_MODEL // TN
    act = lambda width: pl.BlockSpec((tm, width), lambda j, i: (i, 0))
    gate = pl.BlockSpec((None, tm, TN), lambda j, i: (j, i, 0))
    return pl.pallas_call(
        _branch_kernel,
        grid=(n_tiles, m // tm),
        in_specs=[act(ua.shape[1]), act(ub.shape[1]), act(um.shape[1]),
                  pl.BlockSpec((k, TN), lambda j, i: (0, j)),
                  gate, gate, gate],
        out_specs=pl.BlockSpec((tm, TN), lambda j, i: (i, j)),
        out_shape=jax.ShapeDtypeStruct((m, D_MODEL), jnp.bfloat16),
        compiler_params=_params(("parallel", "parallel")),
        name="branch_proj",
    )(ua, ub, um, w_branch, *gates)


def _out_kernel(u_ref, w_ref, x_ref, g_ref, y_ref, ssq_ref, *, tn):
    j = pl.program_id(1)
    n_tiles = y_ref.shape[1] // tn
    y = x_ref[...] + jnp.dot(u_ref[...], w_ref[...], preferred_element_type=jnp.float32)
    part = jnp.sum(y * y, axis=1, keepdims=True)

    @pl.when(j == 0)
    def _():
        ssq_ref[...] = part

    @pl.when(j > 0)
    def _():
        ssq_ref[...] += part

    for t in range(n_tiles):
        @pl.when(j == t)
        def _():
            y_ref[:, t * tn:(t + 1) * tn] = y

    @pl.when(j == n_tiles - 1)
    def _():
        inv = lax.rsqrt(ssq_ref[...] / y_ref.shape[1] + NORM_EPS)
        for t in range(n_tiles):
            sl = slice(t * tn, (t + 1) * tn)
            y_ref[:, sl] = (y_ref[:, sl] * inv) * g_ref[:, sl]


def _out_proj(u, w_out, x, g_final, tm=512):
    m, k = u.shape
    n = x.shape[1]
    tn = min(TN, n)
    return pl.pallas_call(
        functools.partial(_out_kernel, tn=tn),
        grid=(m // tm, n // tn),
        in_specs=[pl.BlockSpec((tm, k), lambda i, j: (i, 0)),
                  pl.BlockSpec((k, tn), lambda i, j: (0, j)),
                  pl.BlockSpec((tm, tn), lambda i, j: (i, j)),
                  pl.BlockSpec((1, n), lambda i, j: (0, 0))],
        out_specs=pl.BlockSpec((tm, n), lambda i, j: (i, 0)),
        out_shape=jax.ShapeDtypeStruct((m, n), jnp.float32),
        scratch_shapes=[pltpu.VMEM((tm, 1), jnp.float32)],
        compiler_params=_params(("parallel", "arbitrary")),
        name="out_proj",
    )(u, w_out, x, g_final.reshape(1, n))


def _rope_tables(seq):
    inv_freq = ROPE_THETA ** (-jnp.arange(0, HEAD_DIM, 2, dtype=jnp.float32) / HEAD_DIM)
    ang = jnp.arange(seq, dtype=jnp.float32)[:, None] * inv_freq[None, :]
    cos, sin = jnp.cos(ang), jnp.sin(ang)
    cos, sin = jnp.concatenate([cos, cos], axis=1), jnp.concatenate([-sin, sin], axis=1)
    one, zero = jnp.ones_like(cos), jnp.zeros_like(sin)
    q_scale = HEAD_DIM ** -0.5 * LOG2E
    m_scale = M_HEAD_DIM ** -0.5 * LOG2E
    return (jnp.stack([cos * q_scale, cos, one, one * m_scale]),
            jnp.stack([sin * q_scale, sin, zero, zero]))


def _mix(h, x2, hm, pb, pz, gates, w, tabs, sink, g_final, batch, seq):
    rows = batch * seq
    kv_mem = _proj(hm, w["mem"], (0, 2), "none", batch=1, tm=batch * MEM_LEN, name="proj_mem")
    kv_mem = kv_mem.reshape(2, batch * MEM_LEN, TN)

    outs, lses = [], []
    for gi, (_, dil) in enumerate(DIL_CONFIGS):
        sd = seq // dil
        n = batch * dil
        a_sets = ((ROT_Q, ROT_K, ROT_ID),) * 2
        qkv = _proj(h, w["a"][gi], (0, 3), "rot", batch=batch, tm=1024, dil=dil, tabs=tabs, sets=a_sets,
                    name=f"proj_a{gi}")
        o, lse = _attn_a(qkv.reshape(3 * n, sd, A_WIDTH), n, sd, (0, n, 2 * n))
        outs.append(o.reshape(batch, dil, sd, A_WIDTH))
        lses.append(lse.reshape(batch, dil, sd, LSE_LANES))
    ua = _merge_a(outs, lses, pz, batch, seq)

    ub = _attn_b(sink, pb, pz, batch, seq)
    um = _attn_m(pb, kv_mem, pz, batch, seq)

    u = _branch(ua, ub, um, w["branch"], gates)
    return _out_proj(u, w["out"], x2, g_final).reshape(batch, seq, x2.shape[1])


def kernel(x_prompt, x_sample, mem_prompt, mem_sample, g_norm, w_in, attn_sink, g_mem, w_mem_kv, w_branch, w_out, g_final):
    bf = jnp.bfloat16
    w_in, w_mem_kv, w_branch, w_out = w_in[0], w_mem_kv[0], w_branch[0], w_out[0]
    trunks = []
    for x, mem in ((x_prompt, mem_prompt), (x_sample, mem_sample)):
        batch, seq, d = x.shape
        x2 = x.reshape(batch * seq, d)
        trunks.append(dict(
            batch=batch, seq=seq, x2=x2, tabs=_rope_tables(seq),
            h=_rmsnorm(x2, g_norm[0], bf),
            hm=_rmsnorm(mem.reshape(batch * MEM_LEN, d), g_mem[0], bf)))

    w_b = w_in[:, B_TILES[0] * TN:B_TILES[1] * TN].astype(bf)
    b_sets = ((ROT_Q, ROT_Q, ROT_K, ROT_M), (ROT_Q, ROT_Q, ROT_ID, ROT_M))
    g_lo = G_TILES[0]
    n_dil = len(DIL_CONFIGS)
    out_tiles = -(-w_out.shape[1] // TN)

    def proj(t, wts, epi, cast, name, **kw):
        out, w_next = _proj(t["h"], wts, (0, 4), epi, batch=t["batch"], tm=1024, cast=cast, name=name, **kw)
        return out.reshape(4, t["batch"] * t["seq"], TN), w_next

    p, s_ = trunks
    pb_p, w_z = proj(p, w_b, "rot", (w_in, Z_TILES), "proj_b", tabs=p["tabs"], sets=b_sets)
    pb_s, w_ga = proj(s_, w_b, "rot", (w_in, (g_lo, g_lo + 4)), "proj_b", tabs=s_["tabs"], sets=b_sets)
    pz_p, w_gb = proj(p, w_z, "silu", (w_in, (g_lo + 4, g_lo + 8)), "proj_z")
    pz_s, w_gm = proj(s_, w_z, "silu", (w_in, (g_lo + 8, g_lo + 12)), "proj_z")
    ga_p, w_a0 = proj(p, w_ga, "sigmoid", (w_in, (0, A_TILES[1], n_dil)), "proj_ga")
    ga_s, w_a1 = proj(s_, w_ga, "sigmoid", (w_in, (1, A_TILES[1], n_dil)), "proj_ga")
    gb_p, w_a2 = proj(p, w_gb, "sigmoid", (w_in, (2, A_TILES[1], n_dil)), "proj_gb")
    gb_s, w_br = proj(s_, w_gb, "sigmoid", (w_branch, (0, D_MODEL // TN)), "proj_gb")
    gm_p, w_o = proj(p, w_gm, "sigmoid", (w_out, (0, out_tiles)), "proj_gm")
    gm_s, w_m = proj(s_, w_gm, "sigmoid", (w_mem_kv, (0, 2)), "proj_gm")

    w = dict(a=(w_a0, w_a1, w_a2), branch=w_br, out=w_o, mem=w_m)
    outs = []
    for t, pb, pz, gates in ((p, pb_p, pz_p, (ga_p, gb_p, gm_p)), (s_, pb_s, pz_s, (ga_s, gb_s, gm_s))):
        outs.append(_mix(t["h"], t["x2"], t["hm"], pb, pz, gates, w, t["tabs"], attn_sink[0], g_final,
                         t["batch"], t["seq"]))
    return tuple(outs)
```

```python
import functools

import jax
import jax.numpy as jnp
from jax import lax
from jax.experimental import pallas as pl
from jax.experimental.pallas import tpu as pltpu

D_MODEL = 4096
HEAD_DIM = 128
ROPE_THETA = 10000.0
NORM_EPS = 1e-6
NEG_INF = -1e30
LOG2E = 1.4426950408889634
LN2 = 0.6931471805599453

DIL_CONFIGS = ((128, 1), (512, 4), (2048, 16))
A_HEADS = 8
A_WIDTH = A_HEADS * HEAD_DIM
A_RADIUS = 64

B_Q_HEADS = 16
B_KV_HEADS = 4
B_GROUP = B_Q_HEADS // B_KV_HEADS
B_RADIUS = 128

MEM_LEN = 256
M_HEADS = 4
M_HEAD_DIM = 256

TN = 1024
A_TILES = (0, 9)
B_TILES = (9, 13)
Z_TILES = (13, 17)
G_TILES = (17, 29)
ROT_Q, ROT_K, ROT_ID, ROT_M = range(4)

VMEM_LIMIT = 56 * 1024 * 1024


def _params(sem, vmem=VMEM_LIMIT):
    return pltpu.CompilerParams(dimension_semantics=sem, vmem_limit_bytes=vmem)


def _rmsnorm_kernel(x_ref, g_ref, o_ref):
    x = x_ref[...]
    ms = jnp.mean(x * x, axis=-1, keepdims=True)
    o_ref[...] = ((x * lax.rsqrt(ms + NORM_EPS)) * g_ref[...]).astype(o_ref.dtype)


def _rmsnorm(x, g, out_dtype, tm=256):
    m, d = x.shape
    return pl.pallas_call(
        _rmsnorm_kernel,
        grid=(m // tm,),
        in_specs=[pl.BlockSpec((tm, d), lambda i: (i, 0)),
                  pl.BlockSpec((1, d), lambda i: (0, 0))],
        out_specs=pl.BlockSpec((tm, d), lambda i: (i, 0)),
        out_shape=jax.ShapeDtypeStruct((m, d), out_dtype),
        compiler_params=_params(("parallel",)),
        name="rmsnorm",
    )(x, g.reshape(1, d))


def _sigmoid(x):
    return 0.5 * jnp.tanh(0.5 * x) + 0.5


def _proj_kernel(*refs, epi, dil, cast):
    refs = list(refs)
    nat_ref = refs.pop() if dil > 1 else None
    if cast:
        cast_out_ref = refs.pop()
        o_ref = refs.pop()
        cast_out_ref[...] = refs.pop()[...].astype(cast_out_ref.dtype)
    else:
        o_ref = refs.pop()
    h_ref, w_ref = refs[:2]
    tm = h_ref.shape[0]
    acc = jnp.dot(h_ref[...], w_ref[...], preferred_element_type=jnp.float32)
    if epi == "none":
        o_ref[0] = acc.astype(o_ref.dtype)
    elif epi == "silu":
        o_ref[0] = (acc * _sigmoid(acc)).astype(o_ref.dtype)
    elif epi == "sigmoid":
        o_ref[0] = _sigmoid(acc).astype(o_ref.dtype)
    else:
        cos_lo, sin_lo, cos_hi, sin_hi = refs[2:6]
        heads = [slice(hh * HEAD_DIM, (hh + 1) * HEAD_DIM) for hh in range(TN // HEAD_DIM)]
        for hh, sl in enumerate(heads):
            cos_ref, sin_ref = (cos_lo, sin_lo) if hh < len(heads) // 2 else (cos_hi, sin_hi)
            x = acc[:, sl]
            x = x * cos_ref[...] + pltpu.roll(x, HEAD_DIM // 2, 1) * sin_ref[...]
            if dil > 1:
                nat_ref[hh] = x
            else:
                o_ref[0, :, sl] = x.astype(o_ref.dtype)
        for r in range(dil if dil > 1 else 0):
            for hh, sl in enumerate(heads):
                o_ref[r, :, sl] = nat_ref[hh, pl.ds(r, tm // dil, stride=dil), :].astype(o_ref.dtype)


def _select(j, values):
    if len(set(values)) == 1:
        return values[0]
    out = values[-1]
    for t in range(len(values) - 2, -1, -1):
        out = jnp.where(j == t, values[t], out)
    return out


CAST_ROWS = 512


def _proj(h, w, tiles, epi, *, batch, tm, dil=1, tabs=None, sets=None, cast=None, name):
    m, k = h.shape
    seq = m // batch
    col_tiles = range(*tiles)
    lo, step, nt = col_tiles.start, col_tiles.step, len(col_tiles)
    assert seq % tm == 0 and tm % dil == 0 and (dil == 1 or epi == "rot")
    bps = seq // tm
    in_specs = [pl.BlockSpec((tm, k), lambda i, j: (i, 0)),
                pl.BlockSpec((k, TN), lambda i, j: (0, lo + step * j))]
    args = [h, w]
    if epi == "rot":
        for half in sets:
            assert len(half) == nt
            spec = pl.BlockSpec((None, tm, HEAD_DIM),
                                lambda i, j, half=half: (_select(j, half), i % bps, 0))
            in_specs += [spec, spec]
            args += list(tabs)
    out_specs = [pl.BlockSpec((None, None, dil, tm // dil, TN), lambda i, j: (j, i // bps, 0, i % bps, 0))]
    out_shape = [jax.ShapeDtypeStruct((nt, batch, dil, seq // dil, TN), jnp.bfloat16)]
    if cast is not None:
        src, src_tiles = cast
        src_tiles = range(*src_tiles)
        cr = min(CAST_ROWS, src.shape[0])
        cw = min(TN, src.shape[1])
        row_blocks = src.shape[0] // cr
        n_chunks = row_blocks * len(src_tiles)
        assert n_chunks <= (m // tm) * nt, "not enough grid steps to cast this weight group"
        chunk = lambda i, j: jnp.minimum(i * nt + j, n_chunks - 1)
        in_specs.append(pl.BlockSpec(
            (cr, cw), lambda i, j: (chunk(i, j) % row_blocks,
                                    src_tiles.start + src_tiles.step * (chunk(i, j) // row_blocks))))
        args.append(src)
        out_specs.append(pl.BlockSpec(
            (cr, cw), lambda i, j: (chunk(i, j) % row_blocks, chunk(i, j) // row_blocks)))
        out_shape.append(jax.ShapeDtypeStruct((src.shape[0], len(src_tiles) * cw), jnp.bfloat16))
    outs = pl.pallas_call(
        functools.partial(_proj_kernel, epi=epi, dil=dil, cast=cast is not None),
        grid=(m // tm, nt),
        in_specs=in_specs,
        out_specs=out_specs,
        out_shape=out_shape,
        scratch_shapes=[pltpu.VMEM((TN // HEAD_DIM, tm, HEAD_DIM), jnp.float32)] if dil > 1 else [],
        compiler_params=_params(("arbitrary", "arbitrary")),
        name=name,
    )(*args)
    return outs if cast is not None else outs[0]


A_TQ = 128
A_STEP = 512
A_HALO = 64
LSE_LANES = 128


def _window_rows(prev_ref, cur_ref, next_ref, lo, hi, cols):
    n = cur_ref.shape[0]
    parts = []
    if lo < 0:
        parts.append(prev_ref[:, cols])
    parts.append(cur_ref[max(lo, 0):min(hi, n), cols])
    if hi > n:
        parts.append(next_ref[:, cols])
    return jnp.concatenate(parts, axis=0) if len(parts) > 1 else parts[0]


def _band_bias(tq, halo, radius, start, seq):
    nk = tq + 2 * halo
    r = lax.broadcasted_iota(jnp.int32, (tq, nk), 0)
    c = lax.broadcasted_iota(jnp.int32, (tq, nk), 1)
    kpos = start - halo + c
    valid = (jnp.abs(c - halo - r) <= radius) & (kpos >= 0) & (kpos < seq)
    return jnp.where(valid, 0.0, NEG_INF)


def _attn_a_kernel(q_ref, kp_ref, kc_ref, kn_ref, vp_ref, vc_ref, vn_ref, o_ref, lse_ref, *, seq):
    step = q_ref.shape[0]
    lane = lax.broadcasted_iota(jnp.int32, (A_TQ, LSE_LANES), 1)
    for b in range(step // A_TQ):
        rows = slice(b * A_TQ, (b + 1) * A_TQ)
        lo, hi = b * A_TQ - A_HALO, (b + 1) * A_TQ + A_HALO
        bias = _band_bias(A_TQ, A_HALO, A_RADIUS, pl.program_id(1) * step + b * A_TQ, seq)
        lse_all = jnp.zeros((A_TQ, LSE_LANES), jnp.float32)
        for hh in range(A_HEADS):
            sl = slice(hh * HEAD_DIM, (hh + 1) * HEAD_DIM)
            k = _window_rows(kp_ref, kc_ref, kn_ref, lo, hi, sl)
            v = _window_rows(vp_ref, vc_ref, vn_ref, lo, hi, sl)
            s = lax.dot_general(q_ref[rows, sl], k, (((1,), (1,)), ((), ())),
                                preferred_element_type=jnp.float32) + bias
            m = jnp.max(s, axis=1, keepdims=True)
            e = jnp.exp2(s - m)
            den = jnp.sum(e, axis=1, keepdims=True)
            o = jnp.dot(e.astype(jnp.bfloat16), v, preferred_element_type=jnp.float32)
            o_ref[rows, sl] = (o * (1.0 / den)).astype(o_ref.dtype)
            lse_all = jnp.where(lane == hh, m * LN2 + jnp.log(den), lse_all)
        lse_ref[rows, :] = lse_all


def _attn_a(qkv, n, seq, offs):
    qo, ko, vo = offs
    step = min(A_STEP, seq)
    nh = seq // A_HALO
    ratio = step // A_HALO
    cur = lambda off: pl.BlockSpec((None, step, A_WIDTH), lambda s, i: (off + s, i, 0))
    prev = lambda off: pl.BlockSpec(
        (None, A_HALO, A_WIDTH), lambda s, i: (off + s, jnp.maximum(ratio * i - 1, 0), 0))
    nxt = lambda off: pl.BlockSpec(
        (None, A_HALO, A_WIDTH), lambda s, i: (off + s, jnp.minimum(ratio * (i + 1), nh - 1), 0))
    return pl.pallas_call(
        functools.partial(_attn_a_kernel, seq=seq),
        grid=(n, seq // step),
        in_specs=[cur(qo), prev(ko), cur(ko), nxt(ko), prev(vo), cur(vo), nxt(vo)],
        out_specs=[pl.BlockSpec((None, step, A_WIDTH), lambda s, i: (s, i, 0)),
                   pl.BlockSpec((None, step, LSE_LANES), lambda s, i: (s, i, 0))],
        out_shape=[jax.ShapeDtypeStruct((n, seq, A_WIDTH), jnp.bfloat16),
                   jax.ShapeDtypeStruct((n, seq, LSE_LANES), jnp.float32)],
        compiler_params=_params(("parallel", "parallel")),
        name="attn_a",
    )(qkv, qkv, qkv, qkv, qkv, qkv, qkv)


def _merge_a_kernel(c0_ref, c1_ref, c2_ref, cl0_ref, cl1_ref, cl2_ref, z_ref, u_ref,
                    o1_ref, o2_ref, l1_ref, l2_ref):
    tm = u_ref.shape[0]
    o0_ref, l0_ref = c0_ref.at[0], cl0_ref.at[0]
    for c_ref, cl_ref, o_ref, l_ref in ((c1_ref, cl1_ref, o1_ref, l1_ref), (c2_ref, cl2_ref, o2_ref, l2_ref)):
        dil = c_ref.shape[0]
        for r in range(dil):
            rows = pl.ds(r, tm // dil, stride=dil)
            for hh in range(A_HEADS):
                o_ref[hh, rows, :] = c_ref[r, :, hh * HEAD_DIM:(hh + 1) * HEAD_DIM].astype(jnp.float32)
            l_ref[rows, :] = cl_ref[r]
    l0, l1, l2 = l0_ref[...], l1_ref[...], l2_ref[...]
    m = jnp.maximum(jnp.maximum(l0, l1), l2)
    e0, e1, e2 = jnp.exp(l0 - m), jnp.exp(l1 - m), jnp.exp(l2 - m)
    inv = 1.0 / (e0 + e1 + e2)
    a0, a1, a2 = e0 * inv, e1 * inv, e2 * inv
    for hh in range(A_HEADS):
        sl = slice(hh * HEAD_DIM, (hh + 1) * HEAD_DIM)
        col = slice(hh, hh + 1)
        o = a0[:, col] * o0_ref[:, sl] + a1[:, col] * o1_ref[hh] + a2[:, col] * o2_ref[hh]
        u_ref[:, sl] = (o * z_ref[:, sl].astype(jnp.float32)).astype(u_ref.dtype)


def _merge_a(outs, lses, pz, batch, seq, tm=512):
    bps = seq // tm
    cls = lambda dil, width: pl.BlockSpec((None, dil, tm // dil, width), lambda i: (i // bps, 0, i % bps, 0))
    dils = [d for _, d in DIL_CONFIGS]
    return pl.pallas_call(
        _merge_a_kernel,
        grid=(batch * bps,),
        in_specs=([cls(d, A_WIDTH) for d in dils] + [cls(d, LSE_LANES) for d in dils]
                  + [pl.BlockSpec((None, tm, TN), lambda i: (0, i, 0))]),
        out_specs=pl.BlockSpec((tm, A_WIDTH), lambda i: (i, 0)),
        out_shape=jax.ShapeDtypeStruct((batch * seq, A_WIDTH), jnp.bfloat16),
        scratch_shapes=[pltpu.VMEM((A_HEADS, tm, HEAD_DIM), jnp.float32)] * 2
                       + [pltpu.VMEM((tm, LSE_LANES), jnp.float32)] * 2,
        compiler_params=_params(("parallel",)),
        name="merge_a",
    )(*outs, *lses, pz)


B_TQ = 128
B_STEP = 512


def _attn_b_kernel(sink_ref, q0_ref, q1_ref, kvp_ref, kvc_ref, kvn_ref, z0_ref, z1_ref, u_ref, *, seq):
    ones = jnp.ones((3 * B_TQ, HEAD_DIM), jnp.bfloat16)
    kv_half = B_KV_HEADS * HEAD_DIM
    for b in range(B_STEP // B_TQ):
        rows = slice(b * B_TQ, (b + 1) * B_TQ)
        lo, hi = (b - 1) * B_TQ, (b + 2) * B_TQ
        bias = _band_bias(B_TQ, B_TQ, B_RADIUS, pl.program_id(1) * B_STEP + b * B_TQ, seq)
        for kh in range(B_KV_HEADS):
            q_ref = q0_ref if kh < 2 else q1_ref
            z_ref = z0_ref if kh < 2 else z1_ref
            base = (kh % 2) * B_GROUP * HEAD_DIM
            heads = [slice(base + g * HEAD_DIM, base + (g + 1) * HEAD_DIM) for g in range(B_GROUP)]
            k = _window_rows(kvp_ref, kvc_ref, kvn_ref, lo, hi, slice(kh * HEAD_DIM, (kh + 1) * HEAD_DIM))
            v = _window_rows(kvp_ref, kvc_ref, kvn_ref, lo, hi,
                             slice(kv_half + kh * HEAD_DIM, kv_half + (kh + 1) * HEAD_DIM))
            q = jnp.concatenate([q_ref[rows, sl] for sl in heads], axis=0)
            s = lax.dot_general(q, k, (((1,), (1,)), ((), ())), preferred_element_type=jnp.float32)
            es, ms, sks = [], [], []
            for g in range(B_GROUP):
                sg = s[g * B_TQ:(g + 1) * B_TQ] + bias
                sk = sink_ref[kh * B_GROUP + g] * LOG2E
                m = jnp.maximum(jnp.max(sg, axis=1, keepdims=True), sk)
                es.append(jnp.exp2(sg - m).astype(jnp.bfloat16))
                ms.append(m)
                sks.append(sk)
            ov = jnp.dot(jnp.concatenate(es, axis=0), jnp.concatenate([v, ones], axis=1),
                         preferred_element_type=jnp.float32)
            for g in range(B_GROUP):
                og = ov[g * B_TQ:(g + 1) * B_TQ]
                den = og[:, HEAD_DIM:HEAD_DIM + 1] + jnp.exp2(sks[g] - ms[g])
                col = (kh * B_GROUP + g) * HEAD_DIM
                u_ref[rows, col:col + HEAD_DIM] = (
                    og[:, :HEAD_DIM] * (1.0 / den) * z_ref[rows, heads[g]].astype(jnp.float32)
                ).astype(u_ref.dtype)


def _attn_b(sink, pb, pz, batch, seq):
    nq = seq // B_STEP
    nh = seq // B_TQ
    ratio = B_STEP // B_TQ
    row = lambda slot: pl.BlockSpec((None, B_STEP, TN), lambda b, i: (slot, b * nq + i, 0))
    kv_prev = pl.BlockSpec((None, B_TQ, TN), lambda b, i: (2, b * nh + jnp.maximum(ratio * i - 1, 0), 0))
    kv_next = pl.BlockSpec((None, B_TQ, TN),
                           lambda b, i: (2, b * nh + jnp.minimum(ratio * (i + 1), nh - 1), 0))
    width = B_Q_HEADS * HEAD_DIM
    return pl.pallas_call(
        functools.partial(_attn_b_kernel, seq=seq),
        grid=(batch, nq),
        in_specs=[pl.BlockSpec(memory_space=pltpu.SMEM),
                  row(0), row(1), kv_prev, row(2), kv_next, row(1), row(2)],
        out_specs=pl.BlockSpec((B_STEP, width), lambda b, i: (b * nq + i, 0)),
        out_shape=jax.ShapeDtypeStruct((batch * seq, width), jnp.bfloat16),
        compiler_params=_params(("parallel", "parallel")),
        name="attn_b",
    )(sink, pb, pb, pb, pb, pb, pz, pz)


M_TQ = 512


def _attn_m_kernel(q_ref, k_ref, v_ref, z_ref, u_ref):
    for hh in range(M_HEADS):
        sl = slice(hh * M_HEAD_DIM, (hh + 1) * M_HEAD_DIM)
        s = lax.dot_general(q_ref[:, sl], k_ref[:, sl], (((1,), (1,)), ((), ())),
                            preferred_element_type=jnp.float32)
        m = jnp.max(s, axis=1, keepdims=True)
        e = jnp.exp2(s - m)
        den = jnp.sum(e, axis=1, keepdims=True)
        o = jnp.dot(e.astype(jnp.bfloat16), v_ref[:, sl], preferred_element_type=jnp.float32) * (1.0 / den)
        u_ref[:, sl] = (o * z_ref[:, sl].astype(jnp.float32)).astype(u_ref.dtype)


def _attn_m(pb, kv_mem, pz, batch, seq):
    nq = seq // M_TQ
    width = M_HEADS * M_HEAD_DIM
    return pl.pallas_call(
        _attn_m_kernel,
        grid=(batch, nq),
        in_specs=[pl.BlockSpec((None, M_TQ, TN), lambda b, i: (3, b * nq + i, 0)),
                  pl.BlockSpec((None, MEM_LEN, TN), lambda b, i: (0, b, 0)),
                  pl.BlockSpec((None, MEM_LEN, TN), lambda b, i: (1, b, 0)),
                  pl.BlockSpec((None, M_TQ, TN), lambda b, i: (3, b * nq + i, 0))],
        out_specs=pl.BlockSpec((M_TQ, width), lambda b, i: (b * nq + i, 0)),
        out_shape=jax.ShapeDtypeStruct((batch * seq, width), jnp.bfloat16),
        compiler_params=_params(("parallel", "parallel")),
        name="attn_m",
    )(pb, kv_mem, kv_mem, pz)


def _branch_kernel(ua_ref, ub_ref, um_ref, w_ref, ga_ref, gb_ref, gm_ref, u_ref):
    a_hi = A_WIDTH
    b_hi = A_WIDTH + B_Q_HEADS * HEAD_DIM
    dot = functools.partial(jnp.dot, preferred_element_type=jnp.float32)
    acc = ga_ref[...].astype(jnp.float32) * dot(ua_ref[...], w_ref[:a_hi, :])
    acc += gb_ref[...].astype(jnp.float32) * dot(ub_ref[...], w_ref[a_hi:b_hi, :])
    acc += gm_ref[...].astype(jnp.float32) * dot(um_ref[...], w_ref[b_hi:, :])
    u_ref[...] = acc.astype(u_ref.dtype)


def _branch(ua, ub, um, w_branch, gates, tm=512):
    m = ua.shape[0]
    k = w_branch.shape[0]
    n_tiles = D_MODEL // TN
    act = lambda width: pl.BlockSpec((tm, width), lambda j, i: (i, 0))
    gate = pl.BlockSpec((None, tm, TN), lambda j, i: (j, i, 0))
    return pl.pallas_call(
        _branch_kernel,
        grid=(n_tiles, m // tm),
        in_specs=[act(ua.shape[1]), act(ub.shape[1]), act(um.shape[1]),
                  pl.BlockSpec((k, TN), lambda j, i: (0, j)),
                  gate, gate, gate],
        out_specs=pl.BlockSpec((tm, TN), lambda j, i: (i, j)),
        out_shape=jax.ShapeDtypeStruct((m, D_MODEL), jnp.bfloat16),
        compiler_params=_params(("parallel", "parallel")),
        name="branch_proj",
    )(ua, ub, um, w_branch, *gates)


def _out_kernel(u_ref, w_ref, x_ref, g_ref, y_ref, ssq_ref, *, tn):
    j = pl.program_id(1)
    n_tiles = y_ref.shape[1] // tn
    y = x_ref[...] + jnp.dot(u_ref[...], w_ref[...], preferred_element_type=jnp.float32)
    part = jnp.sum(y * y, axis=1, keepdims=True)

    @pl.when(j == 0)
    def _():
        ssq_ref[...] = part

    @pl.when(j > 0)
    def _():
        ssq_ref[...] += part

    for t in range(n_tiles):
        @pl.when(j == t)
        def _():
            y_ref[:, t * tn:(t + 1) * tn] = y

    @pl.when(j == n_tiles - 1)
    def _():
        inv = lax.rsqrt(ssq_ref[...] / y_ref.shape[1] + NORM_EPS)
        for t in range(n_tiles):
            sl = slice(t * tn, (t + 1) * tn)
            y_ref[:, sl] = (y_ref[:, sl] * inv) * g_ref[:, sl]


def _out_proj(u, w_out, x, g_final, tm=512):
    m, k = u.shape
    n = x.shape[1]
    tn = min(TN, n)
    return pl.pallas_call(
        functools.partial(_out_kernel, tn=tn),
        grid=(m // tm, n // tn),
        in_specs=[pl.BlockSpec((tm, k), lambda i, j: (i, 0)),
                  pl.BlockSpec((k, tn), lambda i, j: (0, j)),
                  pl.BlockSpec((tm, tn), lambda i, j: (i, j)),
                  pl.BlockSpec((1, n), lambda i, j: (0, 0))],
        out_specs=pl.BlockSpec((tm, n), lambda i, j: (i, 0)),
        out_shape=jax.ShapeDtypeStruct((m, n), jnp.float32),
        scratch_shapes=[pltpu.VMEM((tm, 1), jnp.float32)],
        compiler_params=_params(("parallel", "arbitrary")),
        name="out_proj",
    )(u, w_out, x, g_final.reshape(1, n))


def _rope_tables(seq):
    inv_freq = ROPE_THETA ** (-jnp.arange(0, HEAD_DIM, 2, dtype=jnp.float32) / HEAD_DIM)
    ang = jnp.arange(seq, dtype=jnp.float32)[:, None] * inv_freq[None, :]
    cos, sin = jnp.cos(ang), jnp.sin(ang)
    cos, sin = jnp.concatenate([cos, cos], axis=1), jnp.concatenate([-sin, sin], axis=1)
    one, zero = jnp.ones_like(cos), jnp.zeros_like(sin)
    q_scale = HEAD_DIM ** -0.5 * LOG2E
    m_scale = M_HEAD_DIM ** -0.5 * LOG2E
    return (jnp.stack([cos * q_scale, cos, one, one * m_scale]),
            jnp.stack([sin * q_scale, sin, zero, zero]))


def _mix(h, x2, hm, pb, pz, gates, w, tabs, sink, g_final, batch, seq):
    rows = batch * seq
    kv_mem = _proj(hm, w["mem"], (0, 2), "none", batch=1, tm=batch * MEM_LEN, name="proj_mem")
    kv_mem = kv_mem.reshape(2, batch * MEM_LEN, TN)

    outs, lses = [], []
    for gi, (_, dil) in enumerate(DIL_CONFIGS):
        sd = seq // dil
        n = batch * dil
        a_sets = ((ROT_Q, ROT_K, ROT_ID),) * 2
        qkv = _proj(h, w["a"][gi], (0, 3), "rot", batch=batch, tm=1024, dil=dil, tabs=tabs, sets=a_sets,
                    name=f"proj_a{gi}")
        o, lse = _attn_a(qkv.reshape(3 * n, sd, A_WIDTH), n, sd, (0, n, 2 * n))
        outs.append(o.reshape(batch, dil, sd, A_WIDTH))
        lses.append(lse.reshape(batch, dil, sd, LSE_LANES))
    ua = _merge_a(outs, lses, pz, batch, seq)

    ub = _attn_b(sink, pb, pz, batch, seq)
    um = _attn_m(pb, kv_mem, pz, batch, seq)

    u = _branch(ua, ub, um, w["branch"], gates)
    return _out_proj(u, w["out"], x2, g_final).reshape(batch, seq, x2.shape[1])


def kernel(x_prompt, x_sample, mem_prompt, mem_sample, g_norm, w_in, attn_sink, g_mem, w_mem_kv, w_branch, w_out, g_final):
    bf = jnp.bfloat16
    w_in, w_mem_kv, w_branch, w_out = w_in[0], w_mem_kv[0], w_branch[0], w_out[0]
    trunks = []
    for x, mem in ((x_prompt, mem_prompt), (x_sample, mem_sample)):
        batch, seq, d = x.shape
        x2 = x.reshape(batch * seq, d)
        trunks.append(dict(
            batch=batch, seq=seq, x2=x2, tabs=_rope_tables(seq),
            h=_rmsnorm(x2, g_norm[0], bf),
            hm=_rmsnorm(mem.reshape(batch * MEM_LEN, d), g_mem[0], bf)))

    w_b = w_in[:, B_TILES[0] * TN:B_TILES[1] * TN].astype(bf)
    b_sets = ((ROT_Q, ROT_Q, ROT_K, ROT_M), (ROT_Q, ROT_Q, ROT_ID, ROT_M))
    g_lo = G_TILES[0]
    n_dil = len(DIL_CONFIGS)
    out_tiles = -(-w_out.shape[1] // TN)

    def proj(t, wts, epi, cast, name, **kw):
        out, w_next = _proj(t["h"], wts, (0, 4), epi, batch=t["batch"], tm=1024, cast=cast, name=name, **kw)
        return out.reshape(4, t["batch"] * t["seq"], TN), w_next

    p, s_ = trunks
    pb_p, w_z = proj(p, w_b, "rot", (w_in, Z_TILES), "proj_b", tabs=p["tabs"], sets=b_sets)
    pb_s, w_ga = proj(s_, w_b, "rot", (w_in, (g_lo, g_lo + 4)), "proj_b", tabs=s_["tabs"], sets=b_sets)
    pz_p, w_gb = proj(p, w_z, "silu", (w_in, (g_lo + 4, g_lo + 8)), "proj_z")
    pz_s, w_gm = proj(s_, w_z, "silu", (w_in, (g_lo + 8, g_lo + 12)), "proj_z")
    ga_p, w_a0 = proj(p, w_ga, "sigmoid", (w_in, (0, A_TILES[1], n_dil)), "proj_ga")
    ga_s, w_a1 = proj(s_, w_ga, "sigmoid", (w_in, (1, A_TILES[1], n_dil)), "proj_ga")
    gb_p, w_a2 = proj(p, w_gb, "sigmoid", (w_in, (2, A_TILES[1], n_dil)), "proj_gb")
    gb_s, w_br = proj(s_, w_gb, "sigmoid", (w_branch, (0, D_MODEL // TN)), "proj_gb")
    gm_p, w_o = proj(p, w_gm, "sigmoid", (w_out, (0, out_tiles)), "proj_gm")
    gm_s, w_m = proj(s_, w_gm, "sigmoid", (w_mem_kv, (0, 2)), "proj_gm")

    w = dict(a=(w_a0, w_a1, w_a2), branch=w_br, out=w_o, mem=w_m)
    outs = []
    for t, pb, pz, gates in ((p, pb_p, pz_p, (ga_p, gb_p, gm_p)), (s_, pb_s, pz_s, (ga_s, gb_s, gm_s))):
        outs.append(_mix(t["h"], t["x2"], t["hm"], pb, pz, gates, w, t["tabs"], attn_sink[0], g_final,
                         t["batch"], t["seq"]))
    return tuple(outs)
```

```python
import functools

import jax
import jax.numpy as jnp
from jax import lax
from jax.experimental import pallas as pl
from jax.experimental.pallas import tpu as pltpu

D_MODEL = 4096
HEAD_DIM = 128
ROPE_THETA = 10000.0
NORM_EPS = 1e-6
NEG_INF = -1e30
LOG2E = 1.4426950408889634
LN2 = 0.6931471805599453

DIL_CONFIGS = ((128, 1), (512, 4), (2048, 16))
A_HEADS = 8
A_WIDTH = A_HEADS * HEAD_DIM
A_RADIUS = 64

B_Q_HEADS = 16
B_KV_HEADS = 4
B_GROUP = B_Q_HEADS // B_KV_HEADS
B_RADIUS = 128

MEM_LEN = 256
M_HEADS = 4
M_HEAD_DIM = 256

TN = 1024
A_TILES = (0, 9)
B_TILES = (9, 13)
Z_TILES = (13, 17)
G_TILES = (17, 29)
ROT_Q, ROT_K, ROT_ID, ROT_M = range(4)

VMEM_LIMIT = 56 * 1024 * 1024


def _params(sem, vmem=VMEM_LIMIT):
    return pltpu.CompilerParams(dimension_semantics=sem, vmem_limit_bytes=vmem)


def _launch(host, guest=None):
    parts = [host] if guest is None else [host, guest]
    grid = host["grid"]
    if guest is not None:
        g0, g1 = guest["grid"]
        assert g0 * g1 == grid[0] * grid[1], (guest["grid"], grid)
        to_guest = lambda i, j: divmod(i * grid[1] + j, g1)
        remap = lambda spec: pl.BlockSpec(
            spec.block_shape, (lambda i, j, f=spec.index_map: f(*to_guest(i, j))),
            memory_space=spec.memory_space) if spec.index_map is not None else spec
        guest = dict(guest, in_specs=[remap(sp) for sp in guest["in_specs"]],
                     out_specs=[remap(sp) for sp in guest["out_specs"]])
        parts = [host, guest]
    n_in = [len(p["in_specs"]) for p in parts]
    n_out = [len(p["out_specs"]) for p in parts]
    n_scr = [len(p.get("scratch", [])) for p in parts]

    def body(*refs):
        refs = list(refs)
        take = lambda n: [refs.pop(0) for _ in range(n)]
        ins = [take(n) for n in n_in]
        outs = [take(n) for n in n_out]
        scr = [take(n) for n in n_scr]
        ids = (pl.program_id(0), pl.program_id(1)) if len(grid) == 2 else (pl.program_id(0),)
        gens = [host["stages"](ids, ins[0], outs[0], scr[0])]
        if guest is not None:
            gens.insert(0, guest["stages"](to_guest(*ids), ins[1], outs[1], scr[1]))
        while gens:
            gens = [g for g in gens if next(g, "done") != "done"]

    res = pl.pallas_call(
        body,
        grid=grid,
        in_specs=[sp for p in parts for sp in p["in_specs"]],
        out_specs=[sp for p in parts for sp in p["out_specs"]],
        out_shape=[sh for p in parts for sh in p["out_shape"]],
        scratch_shapes=[sc for p in parts for sc in p.get("scratch", [])],
        compiler_params=_params(("arbitrary",) * len(grid)),
        name=host["name"] if guest is None else host["name"] + "_" + guest["name"],
    )(*[a for p in parts for a in p["args"]])
    if guest is None:
        return res
    return res[:n_out[0]], res[n_out[0]:]


def _rmsnorm_kernel(x_ref, g_ref, o_ref):
    x = x_ref[...]
    ms = jnp.mean(x * x, axis=-1, keepdims=True)
    o_ref[...] = ((x * lax.rsqrt(ms + NORM_EPS)) * g_ref[...]).astype(o_ref.dtype)


def _rmsnorm(x, g, out_dtype, tm=256):
    m, d = x.shape
    return pl.pallas_call(
        _rmsnorm_kernel,
        grid=(m // tm,),
        in_specs=[pl.BlockSpec((tm, d), lambda i: (i, 0)),
                  pl.BlockSpec((1, d), lambda i: (0, 0))],
        out_specs=pl.BlockSpec((tm, d), lambda i: (i, 0)),
        out_shape=jax.ShapeDtypeStruct((m, d), out_dtype),
        compiler_params=_params(("parallel",)),
        name="rmsnorm",
    )(x, g.reshape(1, d))


def _sigmoid(x):
    return 0.5 * jnp.tanh(0.5 * x) + 0.5


def _proj_stages(ids, ins, outs, scratch, *, epi, dil, cast):
    del ids
    h_ref, w_ref = ins[:2]
    o_ref = outs[0]
    tm = h_ref.shape[0]
    if cast:
        outs[1][...] = ins[-1][...].astype(outs[1].dtype)
    if epi == "sigmoid_parts":
        nq = 4
        wq = TN // nq
        for q in range(nq):
            cols = slice(q * wq, (q + 1) * wq)
            acc = jnp.dot(h_ref[...], w_ref[:, cols], preferred_element_type=jnp.float32)
            o_ref[0, :, cols] = _sigmoid(acc).astype(o_ref.dtype)
            yield
        return
    yield
    acc = jnp.dot(h_ref[...], w_ref[...], preferred_element_type=jnp.float32)
    if epi == "none":
        o_ref[0] = acc.astype(o_ref.dtype)
    elif epi == "silu":
        o_ref[0] = (acc * _sigmoid(acc)).astype(o_ref.dtype)
    elif epi == "sigmoid":
        o_ref[0] = _sigmoid(acc).astype(o_ref.dtype)
    else:
        cos_lo, sin_lo, cos_hi, sin_hi = ins[2:6]
        nat_ref = scratch[0] if dil > 1 else None
        heads = [slice(hh * HEAD_DIM, (hh + 1) * HEAD_DIM) for hh in range(TN // HEAD_DIM)]
        for hh, sl in enumerate(heads):
            cos_ref, sin_ref = (cos_lo, sin_lo) if hh < len(heads) // 2 else (cos_hi, sin_hi)
            x = acc[:, sl]
            x = x * cos_ref[...] + pltpu.roll(x, HEAD_DIM // 2, 1) * sin_ref[...]
            if dil > 1:
                nat_ref[hh] = x
            else:
                o_ref[0, :, sl] = x.astype(o_ref.dtype)
        for r in range(dil if dil > 1 else 0):
            for hh, sl in enumerate(heads):
                o_ref[r, :, sl] = nat_ref[hh, pl.ds(r, tm // dil, stride=dil), :].astype(o_ref.dtype)


def _select(j, values):
    if len(set(values)) == 1:
        return values[0]
    out = values[-1]
    for t in range(len(values) - 2, -1, -1):
        out = jnp.where(j == t, values[t], out)
    return out


CAST_ROWS = 512


def _proj(h, w, tiles, epi, *, batch, tm, dil=1, tabs=None, sets=None, cast=None, guest=None, name):
    m, k = h.shape
    seq = m // batch
    col_tiles = range(*tiles)
    lo, step, nt = col_tiles.start, col_tiles.step, len(col_tiles)
    assert seq % tm == 0 and tm % dil == 0 and (dil == 1 or epi == "rot")
    bps = seq // tm
    in_specs = [pl.BlockSpec((tm, k), lambda i, j: (i, 0)),
                pl.BlockSpec((k, TN), lambda i, j: (0, lo + step * j))]
    args = [h, w]
    if epi == "rot":
        for half in sets:
            assert len(half) == nt
            spec = pl.BlockSpec((None, tm, HEAD_DIM),
                                lambda i, j, half=half: (_select(j, half), i % bps, 0))
            in_specs += [spec, spec]
            args += list(tabs)
    out_specs = [pl.BlockSpec((None, None, dil, tm // dil, TN), lambda i, j: (j, i // bps, 0, i % bps, 0))]
    out_shape = [jax.ShapeDtypeStruct((nt, batch, dil, seq // dil, TN), jnp.bfloat16)]
    if cast is not None:
        src, src_tiles = cast
        src_tiles = range(*src_tiles)
        cr = min(CAST_ROWS, src.shape[0])
        cw = min(TN, src.shape[1])
        row_blocks = src.shape[0] // cr
        n_chunks = row_blocks * len(src_tiles)
        assert n_chunks <= (m // tm) * nt, "not enough grid steps to cast this weight group"
        chunk = lambda i, j: jnp.minimum(i * nt + j, n_chunks - 1)
        in_specs.append(pl.BlockSpec(
            (cr, cw), lambda i, j: (chunk(i, j) % row_blocks,
                                    src_tiles.start + src_tiles.step * (chunk(i, j) // row_blocks))))
        args.append(src)
        out_specs.append(pl.BlockSpec(
            (cr, cw), lambda i, j: (chunk(i, j) % row_blocks, chunk(i, j) // row_blocks)))
        out_shape.append(jax.ShapeDtypeStruct((src.shape[0], len(src_tiles) * cw), jnp.bfloat16))
    host = dict(
        stages=functools.partial(_proj_stages, epi=epi, dil=dil, cast=cast is not None),
        grid=(m // tm, nt), in_specs=in_specs, args=args, out_specs=out_specs, out_shape=out_shape,
        scratch=[pltpu.VMEM((TN // HEAD_DIM, tm, HEAD_DIM), jnp.float32)] if dil > 1 else [],
        name=name)
    if guest is not None:
        outs, guest_outs = _launch(host, guest)
        return (outs if cast is not None else outs[0]), guest_outs
    outs = _launch(host)
    return outs if cast is not None else outs[0]


A_TQ = 128
A_STEP = 512
A_HALO = 64
LSE_LANES = 128


def _window_rows(prev_ref, cur_ref, next_ref, lo, hi, cols):
    n = cur_ref.shape[0]
    parts = []
    if lo < 0:
        parts.append(prev_ref[:, cols])
    parts.append(cur_ref[max(lo, 0):min(hi, n), cols])
    if hi > n:
        parts.append(next_ref[:, cols])
    return jnp.concatenate(parts, axis=0) if len(parts) > 1 else parts[0]


def _band_bias(tq, halo, radius, start, seq):
    nk = tq + 2 * halo
    r = lax.broadcasted_iota(jnp.int32, (tq, nk), 0)
    c = lax.broadcasted_iota(jnp.int32, (tq, nk), 1)
    kpos = start - halo + c
    valid = (jnp.abs(c - halo - r) <= radius) & (kpos >= 0) & (kpos < seq)
    return jnp.where(valid, 0.0, NEG_INF)


def _attn_a_kernel(q_ref, kp_ref, kc_ref, kn_ref, vp_ref, vc_ref, vn_ref, o_ref, lse_ref, *, seq):
    step = q_ref.shape[0]
    lane = lax.broadcasted_iota(jnp.int32, (A_TQ, LSE_LANES), 1)
    for b in range(step // A_TQ):
        rows = slice(b * A_TQ, (b + 1) * A_TQ)
        lo, hi = b * A_TQ - A_HALO, (b + 1) * A_TQ + A_HALO
        bias = _band_bias(A_TQ, A_HALO, A_RADIUS, pl.program_id(1) * step + b * A_TQ, seq)
        lse_all = jnp.zeros((A_TQ, LSE_LANES), jnp.float32)
        for hh in range(A_HEADS):
            sl = slice(hh * HEAD_DIM, (hh + 1) * HEAD_DIM)
            k = _window_rows(kp_ref, kc_ref, kn_ref, lo, hi, sl)
            v = _window_rows(vp_ref, vc_ref, vn_ref, lo, hi, sl)
            s = lax.dot_general(q_ref[rows, sl], k, (((1,), (1,)), ((), ())),
                                preferred_element_type=jnp.float32) + bias
            m = jnp.max(s, axis=1, keepdims=True)
            e = jnp.exp2(s - m)
            den = jnp.sum(e, axis=1, keepdims=True)
            o = jnp.dot(e.astype(jnp.bfloat16), v, preferred_element_type=jnp.float32)
            o_ref[rows, sl] = (o * (1.0 / den)).astype(o_ref.dtype)
            lse_all = jnp.where(lane == hh, m * LN2 + jnp.log(den), lse_all)
        lse_ref[rows, :] = lse_all


def _attn_a(qkv, n, seq, offs):
    qo, ko, vo = offs
    step = min(A_STEP, seq)
    nh = seq // A_HALO
    ratio = step // A_HALO
    cur = lambda off: pl.BlockSpec((None, step, A_WIDTH), lambda s, i: (off + s, i, 0))
    prev = lambda off: pl.BlockSpec(
        (None, A_HALO, A_WIDTH), lambda s, i: (off + s, jnp.maximum(ratio * i - 1, 0), 0))
    nxt = lambda off: pl.BlockSpec(
        (None, A_HALO, A_WIDTH), lambda s, i: (off + s, jnp.minimum(ratio * (i + 1), nh - 1), 0))
    return pl.pallas_call(
        functools.partial(_attn_a_kernel, seq=seq),
        grid=(n, seq // step),
        in_specs=[cur(qo), prev(ko), cur(ko), nxt(ko), prev(vo), cur(vo), nxt(vo)],
        out_specs=[pl.BlockSpec((None, step, A_WIDTH), lambda s, i: (s, i, 0)),
                   pl.BlockSpec((None, step, LSE_LANES), lambda s, i: (s, i, 0))],
        out_shape=[jax.ShapeDtypeStruct((n, seq, A_WIDTH), jnp.bfloat16),
                   jax.ShapeDtypeStruct((n, seq, LSE_LANES), jnp.float32)],
        compiler_params=_params(("parallel", "parallel")),
        name="attn_a",
    )(qkv, qkv, qkv, qkv, qkv, qkv, qkv)


def _merge_a_kernel(c0_ref, c1_ref, c2_ref, cl0_ref, cl1_ref, cl2_ref, z_ref, u_ref,
                    o1_ref, o2_ref, l1_ref, l2_ref):
    tm = u_ref.shape[0]
    o0_ref, l0_ref = c0_ref.at[0], cl0_ref.at[0]
    for c_ref, cl_ref, o_ref, l_ref in ((c1_ref, cl1_ref, o1_ref, l1_ref), (c2_ref, cl2_ref, o2_ref, l2_ref)):
        dil = c_ref.shape[0]
        for r in range(dil):
            rows = pl.ds(r, tm // dil, stride=dil)
            for hh in range(A_HEADS):
                o_ref[hh, rows, :] = c_ref[r, :, hh * HEAD_DIM:(hh + 1) * HEAD_DIM].astype(jnp.float32)
            l_ref[rows, :] = cl_ref[r]
    l0, l1, l2 = l0_ref[...], l1_ref[...], l2_ref[...]
    m = jnp.maximum(jnp.maximum(l0, l1), l2)
    e0, e1, e2 = jnp.exp(l0 - m), jnp.exp(l1 - m), jnp.exp(l2 - m)
    inv = 1.0 / (e0 + e1 + e2)
    a0, a1, a2 = e0 * inv, e1 * inv, e2 * inv
    for hh in range(A_HEADS):
        sl = slice(hh * HEAD_DIM, (hh + 1) * HEAD_DIM)
        col = slice(hh, hh + 1)
        o = a0[:, col] * o0_ref[:, sl] + a1[:, col] * o1_ref[hh] + a2[:, col] * o2_ref[hh]
        u_ref[:, sl] = (o * z_ref[:, sl].astype(jnp.float32)).astype(u_ref.dtype)


def _merge_a(outs, lses, pz, batch, seq, tm=512):
    bps = seq // tm
    cls = lambda dil, width: pl.BlockSpec((None, dil, tm // dil, width), lambda i: (i // bps, 0, i % bps, 0))
    dils = [d for _, d in DIL_CONFIGS]
    return pl.pallas_call(
        _merge_a_kernel,
        grid=(batch * bps,),
        in_specs=([cls(d, A_WIDTH) for d in dils] + [cls(d, LSE_LANES) for d in dils]
                  + [pl.BlockSpec((None, tm, TN), lambda i: (0, i, 0))]),
        out_specs=pl.BlockSpec((tm, A_WIDTH), lambda i: (i, 0)),
        out_shape=jax.ShapeDtypeStruct((batch * seq, A_WIDTH), jnp.bfloat16),
        scratch_shapes=[pltpu.VMEM((A_HEADS, tm, HEAD_DIM), jnp.float32)] * 2
                       + [pltpu.VMEM((tm, LSE_LANES), jnp.float32)] * 2,
        compiler_params=_params(("parallel",)),
        name="merge_a",
    )(*outs, *lses, pz)


B_TQ = 128


def _attn_b_stages(ids, ins, outs, scratch, *, seq):
    sink_ref, q0_ref, q1_ref, kvp_ref, kvc_ref, kvn_ref, z0_ref, z1_ref = ins
    u_ref, = outs
    step = u_ref.shape[0]
    ones = jnp.ones((3 * B_TQ, HEAD_DIM), jnp.bfloat16)
    kv_half = B_KV_HEADS * HEAD_DIM
    units = [(b, kh) for b in range(step // B_TQ) for kh in range(B_KV_HEADS)]

    def scores(b, kh):
        rows = slice(b * B_TQ, (b + 1) * B_TQ)
        lo, hi = (b - 1) * B_TQ, (b + 2) * B_TQ
        q_ref = q0_ref if kh < 2 else q1_ref
        base = (kh % 2) * B_GROUP * HEAD_DIM
        heads = [slice(base + g * HEAD_DIM, base + (g + 1) * HEAD_DIM) for g in range(B_GROUP)]
        k = _window_rows(kvp_ref, kvc_ref, kvn_ref, lo, hi, slice(kh * HEAD_DIM, (kh + 1) * HEAD_DIM))
        q = jnp.concatenate([q_ref[rows, sl] for sl in heads], axis=0)
        return lax.dot_general(q, k, (((1,), (1,)), ((), ())), preferred_element_type=jnp.float32)

    def finish(b, kh, s):
        rows = slice(b * B_TQ, (b + 1) * B_TQ)
        lo, hi = (b - 1) * B_TQ, (b + 2) * B_TQ
        z_ref = z0_ref if kh < 2 else z1_ref
        base = (kh % 2) * B_GROUP * HEAD_DIM
        heads = [slice(base + g * HEAD_DIM, base + (g + 1) * HEAD_DIM) for g in range(B_GROUP)]
        bias = _band_bias(B_TQ, B_TQ, B_RADIUS, ids[1] * step + b * B_TQ, seq)
        v = _window_rows(kvp_ref, kvc_ref, kvn_ref, lo, hi,
                         slice(kv_half + kh * HEAD_DIM, kv_half + (kh + 1) * HEAD_DIM))
        es, ms, sks = [], [], []
        for g in range(B_GROUP):
            sg = s[g * B_TQ:(g + 1) * B_TQ] + bias
            sk = sink_ref[kh * B_GROUP + g] * LOG2E
            m = jnp.maximum(jnp.max(sg, axis=1, keepdims=True), sk)
            es.append(jnp.exp2(sg - m).astype(jnp.bfloat16))
            ms.append(m)
            sks.append(sk)
        ov = jnp.dot(jnp.concatenate(es, axis=0), jnp.concatenate([v, ones], axis=1),
                     preferred_element_type=jnp.float32)
        for g in range(B_GROUP):
            og = ov[g * B_TQ:(g + 1) * B_TQ]
            den = og[:, HEAD_DIM:HEAD_DIM + 1] + jnp.exp2(sks[g] - ms[g])
            col = (kh * B_GROUP + g) * HEAD_DIM
            u_ref[rows, col:col + HEAD_DIM] = (
                og[:, :HEAD_DIM] * (1.0 / den) * z_ref[rows, heads[g]].astype(jnp.float32)
            ).astype(u_ref.dtype)

    pending = [scores(*u) for u in units[:2]]
    yield
    for r in range(0, len(units), 2):
        for u, sc in zip(units[r:r + 2], pending):
            finish(*u, sc)
        pending = [scores(*u) for u in units[r + 2:r + 4]]
        yield


def _attn_b(sink, pb, pz, batch, seq, step):
    nq = seq // step
    nh = seq // B_TQ
    ratio = step // B_TQ
    row = lambda slot: pl.BlockSpec((None, step, TN), lambda b, i: (slot, b * nq + i, 0))
    kv_prev = pl.BlockSpec((None, B_TQ, TN), lambda b, i: (2, b * nh + jnp.maximum(ratio * i - 1, 0), 0))
    kv_next = pl.BlockSpec((None, B_TQ, TN),
                           lambda b, i: (2, b * nh + jnp.minimum(ratio * (i + 1), nh - 1), 0))
    width = B_Q_HEADS * HEAD_DIM
    return dict(
        stages=functools.partial(_attn_b_stages, seq=seq),
        grid=(batch, nq),
        in_specs=[pl.BlockSpec(memory_space=pltpu.SMEM),
                  row(0), row(1), kv_prev, row(2), kv_next, row(1), row(2)],
        args=[sink, pb, pb, pb, pb, pb, pz, pz],
        out_specs=[pl.BlockSpec((step, width), lambda b, i: (b * nq + i, 0))],
        out_shape=[jax.ShapeDtypeStruct((batch * seq, width), jnp.bfloat16)],
        name="attn_b")


M_TQ = 512


def _attn_m_kernel(q_ref, k_ref, v_ref, z_ref, u_ref):
    for hh in range(M_HEADS):
        sl = slice(hh * M_HEAD_DIM, (hh + 1) * M_HEAD_DIM)
        s = lax.dot_general(q_ref[:, sl], k_ref[:, sl], (((1,), (1,)), ((), ())),
                            preferred_element_type=jnp.float32)
        m = jnp.max(s, axis=1, keepdims=True)
        e = jnp.exp2(s - m)
        den = jnp.sum(e, axis=1, keepdims=True)
        o = jnp.dot(e.astype(jnp.bfloat16), v_ref[:, sl], preferred_element_type=jnp.float32) * (1.0 / den)
        u_ref[:, sl] = (o * z_ref[:, sl].astype(jnp.float32)).astype(u_ref.dtype)


def _attn_m(pb, kv_mem, pz, batch, seq):
    nq = seq // M_TQ
    width = M_HEADS * M_HEAD_DIM
    return pl.pallas_call(
        _attn_m_kernel,
        grid=(batch, nq),
        in_specs=[pl.BlockSpec((None, M_TQ, TN), lambda b, i: (3, b * nq + i, 0)),
                  pl.BlockSpec((None, MEM_LEN, TN), lambda b, i: (0, b, 0)),
                  pl.BlockSpec((None, MEM_LEN, TN), lambda b, i: (1, b, 0)),
                  pl.BlockSpec((None, M_TQ, TN), lambda b, i: (3, b * nq + i, 0))],
        out_specs=pl.BlockSpec((M_TQ, width), lambda b, i: (b * nq + i, 0)),
        out_shape=jax.ShapeDtypeStruct((batch * seq, width), jnp.bfloat16),
        compiler_params=_params(("parallel", "parallel")),
        name="attn_m",
    )(pb, kv_mem, kv_mem, pz)


def _branch_kernel(ua_ref, ub_ref, um_ref, w_ref, ga_ref, gb_ref, gm_ref, u_ref):
    a_hi = A_WIDTH
    b_hi = A_WIDTH + B_Q_HEADS * HEAD_DIM
    dot = functools.partial(jnp.dot, preferred_element_type=jnp.float32)
    acc = ga_ref[...].astype(jnp.float32) * dot(ua_ref[...], w_ref[:a_hi, :])
    acc += gb_ref[...].astype(jnp.float32) * dot(ub_ref[...], w_ref[a_hi:b_hi, :])
    acc += gm_ref[...].astype(jnp.float32) * dot(um_ref[...], w_ref[b_hi:, :])
    u_ref[...] = acc.astype(u_ref.dtype)


def _branch(ua, ub, um, w_branch, gates, tm=512):
    m = ua.shape[0]
    k = w_branch.shape[0]
    n_tiles = D_MODEL // TN
    act = lambda width: pl.BlockSpec((tm, width), lambda j, i: (i, 0))
    gate = pl.BlockSpec((None, tm, TN), lambda j, i: (j, i, 0))
    return pl.pallas_call(
        _branch_kernel,
        grid=(n_tiles, m // tm),
        in_specs=[act(ua.shape[1]), act(ub.shape[1]), act(um.shape[1]),
                  pl.BlockSpec((k, TN), lambda j, i: (0, j)),
                  gate, gate, gate],
        out_specs=pl.BlockSpec((tm, TN), lambda j, i: (i, j)),
        out_shape=jax.ShapeDtypeStruct((m, D_MODEL), jnp.bfloat16),
        compiler_params=_params(("parallel", "parallel")),
        name="branch_proj",
    )(ua, ub, um, w_branch, *gates)


def _out_kernel(u_ref, w_ref, x_ref, g_ref, y_ref, ssq_ref, *, tn):
    j = pl.program_id(1)
    n_tiles = y_ref.shape[1] // tn
    y = x_ref[...] + jnp.dot(u_ref[...], w_ref[...], preferred_element_type=jnp.float32)
    part = jnp.sum(y * y, axis=1, keepdims=True)

    @pl.when(j == 0)
    def _():
        ssq_ref[...] = part

    @pl.when(j > 0)
    def _():
        ssq_ref[...] += part

    for t in range(n_tiles):
        @pl.when(j == t)
        def _():
            y_ref[:, t * tn:(t + 1) * tn] = y

    @pl.when(j == n_tiles - 1)
    def _():
        inv = lax.rsqrt(ssq_ref[...] / y_ref.shape[1] + NORM_EPS)
        for t in range(n_tiles):
            sl = slice(t * tn, (t + 1) * tn)
            y_ref[:, sl] = (y_ref[:, sl] * inv) * g_ref[:, sl]


def _out_proj(u, w_out, x, g_final, tm=512):
    m, k = u.shape
    n = x.shape[1]
    tn = min(TN, n)
    return pl.pallas_call(
        functools.partial(_out_kernel, tn=tn),
        grid=(m // tm, n // tn),
        in_specs=[pl.BlockSpec((tm, k), lambda i, j: (i, 0)),
                  pl.BlockSpec((k, tn), lambda i, j: (0, j)),
                  pl.BlockSpec((tm, tn), lambda i, j: (i, j)),
                  pl.BlockSpec((1, n), lambda i, j: (0, 0))],
        out_specs=pl.BlockSpec((tm, n), lambda i, j: (i, 0)),
        out_shape=jax.ShapeDtypeStruct((m, n), jnp.float32),
        scratch_shapes=[pltpu.VMEM((tm, 1), jnp.float32)],
        compiler_params=_params(("parallel", "arbitrary")),
        name="out_proj",
    )(u, w_out, x, g_final.reshape(1, n))


def _rope_tables(seq):
    inv_freq = ROPE_THETA ** (-jnp.arange(0, HEAD_DIM, 2, dtype=jnp.float32) / HEAD_DIM)
    ang = jnp.arange(seq, dtype=jnp.float32)[:, None] * inv_freq[None, :]
    cos, sin = jnp.cos(ang), jnp.sin(ang)
    cos, sin = jnp.concatenate([cos, cos], axis=1), jnp.concatenate([-sin, sin], axis=1)
    one, zero = jnp.ones_like(cos), jnp.zeros_like(sin)
    q_scale = HEAD_DIM ** -0.5 * LOG2E
    m_scale = M_HEAD_DIM ** -0.5 * LOG2E
    return (jnp.stack([cos * q_scale, cos, one, one * m_scale]),
            jnp.stack([sin * q_scale, sin, zero, zero]))


def _mix(h, x2, hm, pb, pz, ub, gates, w, tabs, g_final, batch, seq):
    kv_mem = _proj(hm, w["mem"], (0, 2), "none", batch=1, tm=batch * MEM_LEN, name="proj_mem")
    kv_mem = kv_mem.reshape(2, batch * MEM_LEN, TN)

    outs, lses = [], []
    for gi, (_, dil) in enumerate(DIL_CONFIGS):
        sd = seq // dil
        n = batch * dil
        a_sets = ((ROT_Q, ROT_K, ROT_ID),) * 2
        qkv = _proj(h, w["a"][gi], (0, 3), "rot", batch=batch, tm=1024, dil=dil, tabs=tabs, sets=a_sets,
                    name=f"proj_a{gi}")
        o, lse = _attn_a(qkv.reshape(3 * n, sd, A_WIDTH), n, sd, (0, n, 2 * n))
        outs.append(o.reshape(batch, dil, sd, A_WIDTH))
        lses.append(lse.reshape(batch, dil, sd, LSE_LANES))
    ua = _merge_a(outs, lses, pz, batch, seq)

    um = _attn_m(pb, kv_mem, pz, batch, seq)

    u = _branch(ua, ub, um, w["branch"], gates)
    return _out_proj(u, w["out"], x2, g_final).reshape(batch, seq, x2.shape[1])


def kernel(x_prompt, x_sample, mem_prompt, mem_sample, g_norm, w_in, attn_sink, g_mem, w_mem_kv, w_branch, w_out, g_final):
    bf = jnp.bfloat16
    w_in, w_mem_kv, w_branch, w_out = w_in[0], w_mem_kv[0], w_branch[0], w_out[0]
    trunks = []
    for x, mem in ((x_prompt, mem_prompt), (x_sample, mem_sample)):
        batch, seq, d = x.shape
        x2 = x.reshape(batch * seq, d)
        trunks.append(dict(
            batch=batch, seq=seq, x2=x2, tabs=_rope_tables(seq),
            h=_rmsnorm(x2, g_norm[0], bf),
            hm=_rmsnorm(mem.reshape(batch * MEM_LEN, d), g_mem[0], bf)))

    w_b = w_in[:, B_TILES[0] * TN:B_TILES[1] * TN].astype(bf)
    b_sets = ((ROT_Q, ROT_Q, ROT_K, ROT_M), (ROT_Q, ROT_Q, ROT_ID, ROT_M))
    g_lo = G_TILES[0]
    n_dil = len(DIL_CONFIGS)
    out_tiles = -(-w_out.shape[1] // TN)

    def proj(t, wts, epi, cast, name, **kw):
        out, w_next = _proj(t["h"], wts, (0, 4), epi, batch=t["batch"], tm=1024, cast=cast, name=name, **kw)
        return out.reshape(4, t["batch"] * t["seq"], TN), w_next

    p, s_ = trunks
    pb_p, w_z = proj(p, w_b, "rot", (w_in, Z_TILES), "proj_b", tabs=p["tabs"], sets=b_sets)
    pb_s, w_ga = proj(s_, w_b, "rot", (w_in, (g_lo, g_lo + 4)), "proj_b", tabs=s_["tabs"], sets=b_sets)
    pz_p, w_gb = proj(p, w_z, "silu", (w_in, (g_lo + 4, g_lo + 8)), "proj_z")
    pz_s, w_gm = proj(s_, w_z, "silu", (w_in, (g_lo + 8, g_lo + 12)), "proj_z")
    ga_p, w_a0 = proj(p, w_ga, "sigmoid", (w_in, (0, A_TILES[1], n_dil)), "proj_ga")
    ga_s, w_a1 = proj(s_, w_ga, "sigmoid", (w_in, (1, A_TILES[1], n_dil)), "proj_ga")
    gb_p, w_a2 = proj(p, w_gb, "sigmoid", (w_in, (2, A_TILES[1], n_dil)), "proj_gb")
    gb_s, w_br = proj(s_, w_gb, "sigmoid", (w_branch, (0, D_MODEL // TN)), "proj_gb")

    def proj_with_attn_b(t, wts, cast, pb, pz):
        rows = t["batch"] * t["seq"]
        guest = _attn_b(attn_sink[0], pb, pz, t["batch"], t["seq"], step=rows // (rows // 1024 * 4))
        (out, w_next), (ub,) = _proj(t["h"], wts, (0, 4), "sigmoid_parts", batch=t["batch"], tm=1024,
                                     cast=cast, guest=guest, name="proj_gm")
        return out.reshape(4, rows, TN), w_next, ub

    gm_p, w_o, ub_p = proj_with_attn_b(p, w_gm, (w_out, (0, out_tiles)), pb_p, pz_p)
    gm_s, w_m, ub_s = proj_with_attn_b(s_, w_gm, (w_mem_kv, (0, 2)), pb_s, pz_s)

    w = dict(a=(w_a0, w_a1, w_a2), branch=w_br, out=w_o, mem=w_m)
    outs = []
    for t, pb, pz, gates, ub in ((p, pb_p, pz_p, (ga_p, gb_p, gm_p), ub_p),
                                 (s_, pb_s, pz_s, (ga_s, gb_s, gm_s), ub_s)):
        outs.append(_mix(t["h"], t["x2"], t["hm"], pb, pz, ub, gates, w, t["tabs"], g_final,
                         t["batch"], t["seq"]))
    return tuple(outs)
```

```python
import functools

import jax
import jax.numpy as jnp
from jax import lax
from jax.experimental import pallas as pl
from jax.experimental.pallas import tpu as pltpu

D_MODEL = 4096
HEAD_DIM = 128
ROPE_THETA = 10000.0
NORM_EPS = 1e-6
NEG_INF = -1e30
LOG2E = 1.4426950408889634
LN2 = 0.6931471805599453

DIL_CONFIGS = ((128, 1), (512, 4), (2048, 16))
A_HEADS = 8
A_WIDTH = A_HEADS * HEAD_DIM
A_RADIUS = 64

B_Q_HEADS = 16
B_KV_HEADS = 4
B_GROUP = B_Q_HEADS // B_KV_HEADS
B_RADIUS = 128

MEM_LEN = 256
M_HEADS = 4
M_HEAD_DIM = 256

TN = 1024
A_TILES = (0, 9)
B_TILES = (9, 13)
Z_TILES = (13, 17)
G_TILES = (17, 29)
ROT_Q, ROT_K, ROT_ID, ROT_M = range(4)

VMEM_LIMIT = 56 * 1024 * 1024


def _params(sem, vmem=VMEM_LIMIT):
    return pltpu.CompilerParams(dimension_semantics=sem, vmem_limit_bytes=vmem)


def _launch(host, guest=None):
    parts = [host] if guest is None else [host, guest]
    grid = host["grid"]
    if guest is not None:
        g0, g1 = guest["grid"]
        assert g0 * g1 == grid[0] * grid[1], (guest["grid"], grid)
        to_guest = lambda i, j: divmod(i * grid[1] + j, g1)
        remap = lambda spec: pl.BlockSpec(
            spec.block_shape, (lambda i, j, f=spec.index_map: f(*to_guest(i, j))),
            memory_space=spec.memory_space) if spec.index_map is not None else spec
        guest = dict(guest, in_specs=[remap(sp) for sp in guest["in_specs"]],
                     out_specs=[remap(sp) for sp in guest["out_specs"]])
        parts = [host, guest]
    n_in = [len(p["in_specs"]) for p in parts]
    n_out = [len(p["out_specs"]) for p in parts]
    n_scr = [len(p.get("scratch", [])) for p in parts]

    def body(*refs):
        refs = list(refs)
        take = lambda n: [refs.pop(0) for _ in range(n)]
        ins = [take(n) for n in n_in]
        outs = [take(n) for n in n_out]
        scr = [take(n) for n in n_scr]
        ids = (pl.program_id(0), pl.program_id(1)) if len(grid) == 2 else (pl.program_id(0),)
        gens = [host["stages"](ids, ins[0], outs[0], scr[0])]
        if guest is not None:
            gens.insert(0, guest["stages"](to_guest(*ids), ins[1], outs[1], scr[1]))
        while gens:
            gens = [g for g in gens if next(g, "done") != "done"]

    res = pl.pallas_call(
        body,
        grid=grid,
        in_specs=[sp for p in parts for sp in p["in_specs"]],
        out_specs=[sp for p in parts for sp in p["out_specs"]],
        out_shape=[sh for p in parts for sh in p["out_shape"]],
        scratch_shapes=[sc for p in parts for sc in p.get("scratch", [])],
        compiler_params=_params(("arbitrary",) * len(grid)),
        name=host["name"] if guest is None else host["name"] + "_" + guest["name"],
    )(*[a for p in parts for a in p["args"]])
    if guest is None:
        return res
    return res[:n_out[0]], res[n_out[0]:]


def _rmsnorm_kernel(x_ref, g_ref, o_ref):
    x = x_ref[...]
    ms = jnp.mean(x * x, axis=-1, keepdims=True)
    o_ref[...] = ((x * lax.rsqrt(ms + NORM_EPS)) * g_ref[...]).astype(o_ref.dtype)


def _rmsnorm(x, g, out_dtype, tm=256):
    m, d = x.shape
    return pl.pallas_call(
        _rmsnorm_kernel,
        grid=(m // tm,),
        in_specs=[pl.BlockSpec((tm, d), lambda i: (i, 0)),
                  pl.BlockSpec((1, d), lambda i: (0, 0))],
        out_specs=pl.BlockSpec((tm, d), lambda i: (i, 0)),
        out_shape=jax.ShapeDtypeStruct((m, d), out_dtype),
        compiler_params=_params(("parallel",)),
        name="rmsnorm",
    )(x, g.reshape(1, d))


def _sigmoid(x):
    return 0.5 * jnp.tanh(0.5 * x) + 0.5


def _proj_stages(ids, ins, outs, scratch, *, epi, dil, cast):
    del ids
    h_ref, w_ref = ins[:2]
    o_ref = outs[0]
    tm = h_ref.shape[0]
    if cast:
        outs[1][...] = ins[-1][...].astype(outs[1].dtype)
    if epi == "sigmoid_parts":
        nq = 4
        wq = TN // nq
        for q in range(nq):
            cols = slice(q * wq, (q + 1) * wq)
            acc = jnp.dot(h_ref[...], w_ref[:, cols], preferred_element_type=jnp.float32)
            o_ref[0, :, cols] = _sigmoid(acc).astype(o_ref.dtype)
            yield
        return
    yield
    acc = jnp.dot(h_ref[...], w_ref[...], preferred_element_type=jnp.float32)
    if epi == "none":
        o_ref[0] = acc.astype(o_ref.dtype)
    elif epi == "silu":
        o_ref[0] = (acc * _sigmoid(acc)).astype(o_ref.dtype)
    elif epi == "sigmoid":
        o_ref[0] = _sigmoid(acc).astype(o_ref.dtype)
    else:
        cos_lo, sin_lo, cos_hi, sin_hi = ins[2:6]
        heads = [slice(hh * HEAD_DIM, (hh + 1) * HEAD_DIM) for hh in range(TN // HEAD_DIM)]
        for hh, sl in enumerate(heads):
            cos_ref, sin_ref = (cos_lo, sin_lo) if hh < len(heads) // 2 else (cos_hi, sin_hi)
            x = acc[:, sl]
            x = x * cos_ref[...] + pltpu.roll(x, HEAD_DIM // 2, 1) * sin_ref[...]
            if dil > 1:
                o_ref[:, :, sl] = jnp.swapaxes(x.reshape(tm // dil, dil, HEAD_DIM), 0, 1).astype(o_ref.dtype)
            else:
                o_ref[0, :, sl] = x.astype(o_ref.dtype)


def _select(j, values):
    if len(set(values)) == 1:
        return values[0]
    out = values[-1]
    for t in range(len(values) - 2, -1, -1):
        out = jnp.where(j == t, values[t], out)
    return out


CAST_ROWS = 512


def _proj(h, w, tiles, epi, *, batch, tm, dil=1, tabs=None, sets=None, cast=None, guest=None, name):
    m, k = h.shape
    seq = m // batch
    col_tiles = range(*tiles)
    lo, step, nt = col_tiles.start, col_tiles.step, len(col_tiles)
    assert seq % tm == 0 and tm % dil == 0 and (dil == 1 or epi == "rot")
    bps = seq // tm
    in_specs = [pl.BlockSpec((tm, k), lambda i, j: (i, 0)),
                pl.BlockSpec((k, TN), lambda i, j: (0, lo + step * j))]
    args = [h, w]
    if epi == "rot":
        for half in sets:
            assert len(half) == nt
            spec = pl.BlockSpec((None, tm, HEAD_DIM),
                                lambda i, j, half=half: (_select(j, half), i % bps, 0))
            in_specs += [spec, spec]
            args += list(tabs)
    out_specs = [pl.BlockSpec((None, None, dil, tm // dil, TN), lambda i, j: (j, i // bps, 0, i % bps, 0))]
    out_shape = [jax.ShapeDtypeStruct((nt, batch, dil, seq // dil, TN), jnp.bfloat16)]
    if cast is not None:
        src, src_tiles = cast
        src_tiles = range(*src_tiles)
        cr = min(CAST_ROWS, src.shape[0])
        cw = min(TN, src.shape[1])
        row_blocks = src.shape[0] // cr
        n_chunks = row_blocks * len(src_tiles)
        assert n_chunks <= (m // tm) * nt, "not enough grid steps to cast this weight group"
        chunk = lambda i, j: jnp.minimum(i * nt + j, n_chunks - 1)
        in_specs.append(pl.BlockSpec(
            (cr, cw), lambda i, j: (chunk(i, j) % row_blocks,
                                    src_tiles.start + src_tiles.step * (chunk(i, j) // row_blocks))))
        args.append(src)
        out_specs.append(pl.BlockSpec(
            (cr, cw), lambda i, j: (chunk(i, j) % row_blocks, chunk(i, j) // row_blocks)))
        out_shape.append(jax.ShapeDtypeStruct((src.shape[0], len(src_tiles) * cw), jnp.bfloat16))
    host = dict(
        stages=functools.partial(_proj_stages, epi=epi, dil=dil, cast=cast is not None),
        grid=(m // tm, nt), in_specs=in_specs, args=args, out_specs=out_specs, out_shape=out_shape,
        name=name)
    if guest is not None:
        outs, guest_outs = _launch(host, guest)
        return (outs if cast is not None else outs[0]), guest_outs
    outs = _launch(host)
    return outs if cast is not None else outs[0]


A_TQ = 128
A_STEP = 512
A_HALO = 64
LSE_LANES = 128


def _window_rows(prev_ref, cur_ref, next_ref, lo, hi, cols):
    n = cur_ref.shape[0]
    parts = []
    if lo < 0:
        parts.append(prev_ref[:, cols])
    parts.append(cur_ref[max(lo, 0):min(hi, n), cols])
    if hi > n:
        parts.append(next_ref[:, cols])
    return jnp.concatenate(parts, axis=0) if len(parts) > 1 else parts[0]


def _band_bias(tq, halo, radius, start, seq):
    nk = tq + 2 * halo
    r = lax.broadcasted_iota(jnp.int32, (tq, nk), 0)
    c = lax.broadcasted_iota(jnp.int32, (tq, nk), 1)
    kpos = start - halo + c
    valid = (jnp.abs(c - halo - r) <= radius) & (kpos >= 0) & (kpos < seq)
    return jnp.where(valid, 0.0, NEG_INF)


def _attn_a_kernel(q_ref, kp_ref, kc_ref, kn_ref, vp_ref, vc_ref, vn_ref, o_ref, lse_ref, *, seq):
    step = q_ref.shape[0]
    lane = lax.broadcasted_iota(jnp.int32, (A_TQ, LSE_LANES), 1)
    for b in range(step // A_TQ):
        rows = slice(b * A_TQ, (b + 1) * A_TQ)
        lo, hi = b * A_TQ - A_HALO, (b + 1) * A_TQ + A_HALO
        bias = _band_bias(A_TQ, A_HALO, A_RADIUS, pl.program_id(1) * step + b * A_TQ, seq)
        lse_all = jnp.zeros((A_TQ, LSE_LANES), jnp.float32)
        for hh in range(A_HEADS):
            sl = slice(hh * HEAD_DIM, (hh + 1) * HEAD_DIM)
            k = _window_rows(kp_ref, kc_ref, kn_ref, lo, hi, sl)
            v = _window_rows(vp_ref, vc_ref, vn_ref, lo, hi, sl)
            s = lax.dot_general(q_ref[rows, sl], k, (((1,), (1,)), ((), ())),
                                preferred_element_type=jnp.float32) + bias
            m = jnp.max(s, axis=1, keepdims=True)
            e = jnp.exp2(s - m)
            den = jnp.sum(e, axis=1, keepdims=True)
            o = jnp.dot(e.astype(jnp.bfloat16), v, preferred_element_type=jnp.float32)
            o_ref[rows, sl] = (o * (1.0 / den)).astype(o_ref.dtype)
            lse_all = jnp.where(lane == hh, m * LN2 + jnp.log(den), lse_all)
        lse_ref[rows, :] = lse_all


def _attn_a(qkv, n, seq, offs):
    qo, ko, vo = offs
    step = min(A_STEP, seq)
    nh = seq // A_HALO
    ratio = step // A_HALO
    cur = lambda off: pl.BlockSpec((None, step, A_WIDTH), lambda s, i: (off + s, i, 0))
    prev = lambda off: pl.BlockSpec(
        (None, A_HALO, A_WIDTH), lambda s, i: (off + s, jnp.maximum(ratio * i - 1, 0), 0))
    nxt = lambda off: pl.BlockSpec(
        (None, A_HALO, A_WIDTH), lambda s, i: (off + s, jnp.minimum(ratio * (i + 1), nh - 1), 0))
    return pl.pallas_call(
        functools.partial(_attn_a_kernel, seq=seq),
        grid=(n, seq // step),
        in_specs=[cur(qo), prev(ko), cur(ko), nxt(ko), prev(vo), cur(vo), nxt(vo)],
        out_specs=[pl.BlockSpec((None, step, A_WIDTH), lambda s, i: (s, i, 0)),
                   pl.BlockSpec((None, step, LSE_LANES), lambda s, i: (s, i, 0))],
        out_shape=[jax.ShapeDtypeStruct((n, seq, A_WIDTH), jnp.bfloat16),
                   jax.ShapeDtypeStruct((n, seq, LSE_LANES), jnp.float32)],
        compiler_params=_params(("parallel", "parallel")),
        name="attn_a",
    )(qkv, qkv, qkv, qkv, qkv, qkv, qkv)


def _merge_a_kernel(c0_ref, c1_ref, c2_ref, cl0_ref, cl1_ref, cl2_ref, z_ref, u_ref,
                    o1_ref, o2_ref, l1_ref, l2_ref):
    tm = u_ref.shape[0]
    o0_ref, l0_ref = c0_ref.at[0], cl0_ref.at[0]
    for c_ref, cl_ref, o_ref, l_ref in ((c1_ref, cl1_ref, o1_ref, l1_ref), (c2_ref, cl2_ref, o2_ref, l2_ref)):
        dil = c_ref.shape[0]
        for r in range(dil):
            rows = pl.ds(r, tm // dil, stride=dil)
            for hh in range(A_HEADS):
                o_ref[hh, rows, :] = c_ref[r, :, hh * HEAD_DIM:(hh + 1) * HEAD_DIM].astype(jnp.float32)
            l_ref[rows, :] = cl_ref[r]
    l0, l1, l2 = l0_ref[...], l1_ref[...], l2_ref[...]
    m = jnp.maximum(jnp.maximum(l0, l1), l2)
    e0, e1, e2 = jnp.exp(l0 - m), jnp.exp(l1 - m), jnp.exp(l2 - m)
    inv = 1.0 / (e0 + e1 + e2)
    a0, a1, a2 = e0 * inv, e1 * inv, e2 * inv
    for hh in range(A_HEADS):
        sl = slice(hh * HEAD_DIM, (hh + 1) * HEAD_DIM)
        col = slice(hh, hh + 1)
        o = a0[:, col] * o0_ref[:, sl] + a1[:, col] * o1_ref[hh] + a2[:, col] * o2_ref[hh]
        u_ref[:, sl] = (o * z_ref[:, sl].astype(jnp.float32)).astype(u_ref.dtype)


def _merge_a(outs, lses, pz, batch, seq, tm=512):
    bps = seq // tm
    cls = lambda dil, width: pl.BlockSpec((None, dil, tm // dil, width), lambda i: (i // bps, 0, i % bps, 0))
    dils = [d for _, d in DIL_CONFIGS]
    return pl.pallas_call(
        _merge_a_kernel,
        grid=(batch * bps,),
        in_specs=([cls(d, A_WIDTH) for d in dils] + [cls(d, LSE_LANES) for d in dils]
                  + [pl.BlockSpec((None, tm, TN), lambda i: (0, i, 0))]),
        out_specs=pl.BlockSpec((tm, A_WIDTH), lambda i: (i, 0)),
        out_shape=jax.ShapeDtypeStruct((batch * seq, A_WIDTH), jnp.bfloat16),
        scratch_shapes=[pltpu.VMEM((A_HEADS, tm, HEAD_DIM), jnp.float32)] * 2
                       + [pltpu.VMEM((tm, LSE_LANES), jnp.float32)] * 2,
        compiler_params=_params(("parallel",)),
        name="merge_a",
    )(*outs, *lses, pz)


B_TQ = 128


def _attn_b_stages(ids, ins, outs, scratch, *, seq):
    sink_ref, q0_ref, q1_ref, kvp_ref, kvc_ref, kvn_ref, z0_ref, z1_ref = ins
    u_ref, = outs
    step = u_ref.shape[0]
    ones = jnp.ones((3 * B_TQ, HEAD_DIM), jnp.bfloat16)
    kv_half = B_KV_HEADS * HEAD_DIM
    units = [(b, kh) for b in range(step // B_TQ) for kh in range(B_KV_HEADS)]

    def scores(b, kh):
        rows = slice(b * B_TQ, (b + 1) * B_TQ)
        lo, hi = (b - 1) * B_TQ, (b + 2) * B_TQ
        q_ref = q0_ref if kh < 2 else q1_ref
        base = (kh % 2) * B_GROUP * HEAD_DIM
        heads = [slice(base + g * HEAD_DIM, base + (g + 1) * HEAD_DIM) for g in range(B_GROUP)]
        k = _window_rows(kvp_ref, kvc_ref, kvn_ref, lo, hi, slice(kh * HEAD_DIM, (kh + 1) * HEAD_DIM))
        q = jnp.concatenate([q_ref[rows, sl] for sl in heads], axis=0)
        return lax.dot_general(q, k, (((1,), (1,)), ((), ())), preferred_element_type=jnp.float32)

    def finish(b, kh, s):
        rows = slice(b * B_TQ, (b + 1) * B_TQ)
        lo, hi = (b - 1) * B_TQ, (b + 2) * B_TQ
        z_ref = z0_ref if kh < 2 else z1_ref
        base = (kh % 2) * B_GROUP * HEAD_DIM
        heads = [slice(base + g * HEAD_DIM, base + (g + 1) * HEAD_DIM) for g in range(B_GROUP)]
        bias = _band_bias(B_TQ, B_TQ, B_RADIUS, ids[1] * step + b * B_TQ, seq)
        v = _window_rows(kvp_ref, kvc_ref, kvn_ref, lo, hi,
                         slice(kv_half + kh * HEAD_DIM, kv_half + (kh + 1) * HEAD_DIM))
        es, ms, sks = [], [], []
        for g in range(B_GROUP):
            sg = s[g * B_TQ:(g + 1) * B_TQ] + bias
            sk = sink_ref[kh * B_GROUP + g] * LOG2E
            m = jnp.maximum(jnp.max(sg, axis=1, keepdims=True), sk)
            es.append(jnp.exp2(sg - m).astype(jnp.bfloat16))
            ms.append(m)
            sks.append(sk)
        ov = jnp.dot(jnp.concatenate(es, axis=0), jnp.concatenate([v, ones], axis=1),
                     preferred_element_type=jnp.float32)
        for g in range(B_GROUP):
            og = ov[g * B_TQ:(g + 1) * B_TQ]
            den = og[:, HEAD_DIM:HEAD_DIM + 1] + jnp.exp2(sks[g] - ms[g])
            col = (kh * B_GROUP + g) * HEAD_DIM
            u_ref[rows, col:col + HEAD_DIM] = (
                og[:, :HEAD_DIM] * (1.0 / den) * z_ref[rows, heads[g]].astype(jnp.float32)
            ).astype(u_ref.dtype)

    pending = [scores(*u) for u in units[:2]]
    yield
    for r in range(0, len(units), 2):
        for u, sc in zip(units[r:r + 2], pending):
            finish(*u, sc)
        pending = [scores(*u) for u in units[r + 2:r + 4]]
        yield


def _attn_b(sink, pb, pz, batch, seq, step):
    nq = seq // step
    nh = seq // B_TQ
    ratio = step // B_TQ
    row = lambda slot: pl.BlockSpec((None, step, TN), lambda b, i: (slot, b * nq + i, 0))
    kv_prev = pl.BlockSpec((None, B_TQ, TN), lambda b, i: (2, b * nh + jnp.maximum(ratio * i - 1, 0), 0))
    kv_next = pl.BlockSpec((None, B_TQ, TN),
                           lambda b, i: (2, b * nh + jnp.minimum(ratio * (i + 1), nh - 1), 0))
    width = B_Q_HEADS * HEAD_DIM
    return dict(
        stages=functools.partial(_attn_b_stages, seq=seq),
        grid=(batch, nq),
        in_specs=[pl.BlockSpec(memory_space=pltpu.SMEM),
                  row(0), row(1), kv_prev, row(2), kv_next, row(1), row(2)],
        args=[sink, pb, pb, pb, pb, pb, pz, pz],
        out_specs=[pl.BlockSpec((step, width), lambda b, i: (b * nq + i, 0))],
        out_shape=[jax.ShapeDtypeStruct((batch * seq, width), jnp.bfloat16)],
        name="attn_b")


M_TQ = 512


def _attn_m_kernel(q_ref, k_ref, v_ref, z_ref, u_ref):
    for hh in range(M_HEADS):
        sl = slice(hh * M_HEAD_DIM, (hh + 1) * M_HEAD_DIM)
        s = lax.dot_general(q_ref[:, sl], k_ref[:, sl], (((1,), (1,)), ((), ())),
                            preferred_element_type=jnp.float32)
        m = jnp.max(s, axis=1, keepdims=True)
        e = jnp.exp2(s - m)
        den = jnp.sum(e, axis=1, keepdims=True)
        o = jnp.dot(e.astype(jnp.bfloat16), v_ref[:, sl], preferred_element_type=jnp.float32) * (1.0 / den)
        u_ref[:, sl] = (o * z_ref[:, sl].astype(jnp.float32)).astype(u_ref.dtype)


def _attn_m(pb, kv_mem, pz, batch, seq):
    nq = seq // M_TQ
    width = M_HEADS * M_HEAD_DIM
    return pl.pallas_call(
        _attn_m_kernel,
        grid=(batch, nq),
        in_specs=[pl.BlockSpec((None, M_TQ, TN), lambda b, i: (3, b * nq + i, 0)),
                  pl.BlockSpec((None, MEM_LEN, TN), lambda b, i: (0, b, 0)),
                  pl.BlockSpec((None, MEM_LEN, TN), lambda b, i: (1, b, 0)),
                  pl.BlockSpec((None, M_TQ, TN), lambda b, i: (3, b * nq + i, 0))],
        out_specs=pl.BlockSpec((M_TQ, width), lambda b, i: (b * nq + i, 0)),
        out_shape=jax.ShapeDtypeStruct((batch * seq, width), jnp.bfloat16),
        compiler_params=_params(("parallel", "parallel")),
        name="attn_m",
    )(pb, kv_mem, kv_mem, pz)


def _branch_kernel(ua_ref, ub_ref, um_ref, w_ref, ga_ref, gb_ref, gm_ref, u_ref):
    a_hi = A_WIDTH
    b_hi = A_WIDTH + B_Q_HEADS * HEAD_DIM
    dot = functools.partial(jnp.dot, preferred_element_type=jnp.float32)
    acc = ga_ref[...].astype(jnp.float32) * dot(ua_ref[...], w_ref[:a_hi, :])
    acc += gb_ref[...].astype(jnp.float32) * dot(ub_ref[...], w_ref[a_hi:b_hi, :])
    acc += gm_ref[...].astype(jnp.float32) * dot(um_ref[...], w_ref[b_hi:, :])
    u_ref[...] = acc.astype(u_ref.dtype)


def _branch(ua, ub, um, w_branch, gates, tm=512):
    m = ua.shape[0]
    k = w_branch.shape[0]
    n_tiles = D_MODEL // TN
    act = lambda width: pl.BlockSpec((tm, width), lambda j, i: (i, 0))
    gate = pl.BlockSpec((None, tm, TN), lambda j, i: (j, i, 0))
    return pl.pallas_call(
        _branch_kernel,
        grid=(n_tiles, m // tm),
        in_specs=[act(ua.shape[1]), act(ub.shape[1]), act(um.shape[1]),
                  pl.BlockSpec((k, TN), lambda j, i: (0, j)),
                  gate, gate, gate],
        out_specs=pl.BlockSpec((tm, TN), lambda j, i: (i, j)),
        out_shape=jax.ShapeDtypeStruct((m, D_MODEL), jnp.bfloat16),
        compiler_params=_params(("parallel", "parallel")),
        name="branch_proj",
    )(ua, ub, um, w_branch, *gates)


def _out_kernel(u_ref, w_ref, x_ref, g_ref, y_ref, ssq_ref, *, tn):
    j = pl.program_id(1)
    n_tiles = y_ref.shape[1] // tn
    y = x_ref[...] + jnp.dot(u_ref[...], w_ref[...], preferred_element_type=jnp.float32)
    part = jnp.sum(y * y, axis=1, keepdims=True)

    @pl.when(j == 0)
    def _():
        ssq_ref[...] = part

    @pl.when(j > 0)
    def _():
        ssq_ref[...] += part

    for t in range(n_tiles):
        @pl.when(j == t)
        def _():
            y_ref[:, t * tn:(t + 1) * tn] = y

    @pl.when(j == n_tiles - 1)
    def _():
        inv = lax.rsqrt(ssq_ref[...] / y_ref.shape[1] + NORM_EPS)
        for t in range(n_tiles):
            sl = slice(t * tn, (t + 1) * tn)
            y_ref[:, sl] = (y_ref[:, sl] * inv) * g_ref[:, sl]


def _out_proj(u, w_out, x, g_final, tm=512):
    m, k = u.shape
    n = x.shape[1]
    tn = min(TN, n)
    return pl.pallas_call(
        functools.partial(_out_kernel, tn=tn),
        grid=(m // tm, n // tn),
        in_specs=[pl.BlockSpec((tm, k), lambda i, j: (i, 0)),
                  pl.BlockSpec((k, tn), lambda i, j: (0, j)),
                  pl.BlockSpec((tm, tn), lambda i, j: (i, j)),
                  pl.BlockSpec((1, n), lambda i, j: (0, 0))],
        out_specs=pl.BlockSpec((tm, n), lambda i, j: (i, 0)),
        out_shape=jax.ShapeDtypeStruct((m, n), jnp.float32),
        scratch_shapes=[pltpu.VMEM((tm, 1), jnp.float32)],
        compiler_params=_params(("parallel", "arbitrary")),
        name="out_proj",
    )(u, w_out, x, g_final.reshape(1, n))


def _rope_tables(seq):
    inv_freq = ROPE_THETA ** (-jnp.arange(0, HEAD_DIM, 2, dtype=jnp.float32) / HEAD_DIM)
    ang = jnp.arange(seq, dtype=jnp.float32)[:, None] * inv_freq[None, :]
    cos, sin = jnp.cos(ang), jnp.sin(ang)
    cos, sin = jnp.concatenate([cos, cos], axis=1), jnp.concatenate([-sin, sin], axis=1)
    one, zero = jnp.ones_like(cos), jnp.zeros_like(sin)
    q_scale = HEAD_DIM ** -0.5 * LOG2E
    m_scale = M_HEAD_DIM ** -0.5 * LOG2E
    return (jnp.stack([cos * q_scale, cos, one, one * m_scale]),
            jnp.stack([sin * q_scale, sin, zero, zero]))


def _mix(h, x2, hm, pb, pz, ub, gates, w, tabs, g_final, batch, seq):
    kv_mem = _proj(hm, w["mem"], (0, 2), "none", batch=1, tm=batch * MEM_LEN, name="proj_mem")
    kv_mem = kv_mem.reshape(2, batch * MEM_LEN, TN)

    outs, lses = [], []
    for gi, (_, dil) in enumerate(DIL_CONFIGS):
        sd = seq // dil
        n = batch * dil
        a_sets = ((ROT_Q, ROT_K, ROT_ID),) * 2
        qkv = _proj(h, w["a"][gi], (0, 3), "rot", batch=batch, tm=1024, dil=dil, tabs=tabs, sets=a_sets,
                    name=f"proj_a{gi}")
        o, lse = _attn_a(qkv.reshape(3 * n, sd, A_WIDTH), n, sd, (0, n, 2 * n))
        outs.append(o.reshape(batch, dil, sd, A_WIDTH))
        lses.append(lse.reshape(batch, dil, sd, LSE_LANES))
    ua = _merge_a(outs, lses, pz, batch, seq)

    um = _attn_m(pb, kv_mem, pz, batch, seq)

    u = _branch(ua, ub, um, w["branch"], gates)
    return _out_proj(u, w["out"], x2, g_final).reshape(batch, seq, x2.shape[1])


def kernel(x_prompt, x_sample, mem_prompt, mem_sample, g_norm, w_in, attn_sink, g_mem, w_mem_kv, w_branch, w_out, g_final):
    bf = jnp.bfloat16
    w_in, w_mem_kv, w_branch, w_out = w_in[0], w_mem_kv[0], w_branch[0], w_out[0]
    trunks = []
    tabs = _rope_tables(max(x_prompt.shape[1], x_sample.shape[1]))
    for x, mem in ((x_prompt, mem_prompt), (x_sample, mem_sample)):
        batch, seq, d = x.shape
        x2 = x.reshape(batch * seq, d)
        trunks.append(dict(
            batch=batch, seq=seq, x2=x2, tabs=tabs,
            h=_rmsnorm(x2, g_norm[0], bf),
            hm=_rmsnorm(mem.reshape(batch * MEM_LEN, d), g_mem[0], bf)))

    w_b = w_in[:, B_TILES[0] * TN:B_TILES[1] * TN].astype(bf)
    b_sets = ((ROT_Q, ROT_Q, ROT_K, ROT_M), (ROT_Q, ROT_Q, ROT_ID, ROT_M))
    g_lo = G_TILES[0]
    n_dil = len(DIL_CONFIGS)
    out_tiles = -(-w_out.shape[1] // TN)

    def proj(t, wts, epi, cast, name, **kw):
        out, w_next = _proj(t["h"], wts, (0, 4), epi, batch=t["batch"], tm=1024, cast=cast, name=name, **kw)
        return out.reshape(4, t["batch"] * t["seq"], TN), w_next

    p, s_ = trunks
    pb_p, w_z = proj(p, w_b, "rot", (w_in, Z_TILES), "proj_b", tabs=p["tabs"], sets=b_sets)
    pb_s, w_ga = proj(s_, w_b, "rot", (w_in, (g_lo, g_lo + 4)), "proj_b", tabs=s_["tabs"], sets=b_sets)
    pz_p, w_gb = proj(p, w_z, "silu", (w_in, (g_lo + 4, g_lo + 8)), "proj_z")
    pz_s, w_gm = proj(s_, w_z, "silu", (w_in, (g_lo + 8, g_lo + 12)), "proj_z")
    ga_p, w_a0 = proj(p, w_ga, "sigmoid", (w_in, (0, A_TILES[1], n_dil)), "proj_ga")
    ga_s, w_a1 = proj(s_, w_ga, "sigmoid", (w_in, (1, A_TILES[1], n_dil)), "proj_ga")
    gb_p, w_a2 = proj(p, w_gb, "sigmoid", (w_in, (2, A_TILES[1], n_dil)), "proj_gb")
    gb_s, w_br = proj(s_, w_gb, "sigmoid", (w_branch, (0, D_MODEL // TN)), "proj_gb")

    def proj_with_attn_b(t, wts, cast, pb, pz):
        rows = t["batch"] * t["seq"]
        guest = _attn_b(attn_sink[0], pb, pz, t["batch"], t["seq"], step=rows // (rows // 1024 * 4))
        (out, w_next), (ub,) = _proj(t["h"], wts, (0, 4), "sigmoid_parts", batch=t["batch"], tm=1024,
                                     cast=cast, guest=guest, name="proj_gm")
        return out.reshape(4, rows, TN), w_next, ub

    gm_p, w_o, ub_p = proj_with_attn_b(p, w_gm, (w_out, (0, out_tiles)), pb_p, pz_p)
    gm_s, w_m, ub_s = proj_with_attn_b(s_, w_gm, (w_mem_kv, (0, 2)), pb_s, pz_s)

    w = dict(a=(w_a0, w_a1, w_a2), branch=w_br, out=w_o, mem=w_m)
    outs = []
    for t, pb, pz, gates, ub in ((p, pb_p, pz_p, (ga_p, gb_p, gm_p), ub_p),
                                 (s_, pb_s, pz_s, (ga_s, gb_s, gm_s), ub_s)):
        outs.append(_mix(t["h"], t["x2"], t["hm"], pb, pz, ub, gates, w, t["tabs"], g_final,
                         t["batch"], t["seq"]))
    return tuple(outs)
```

```python
import functools

import jax
import jax.numpy as jnp
from jax import lax
from jax.experimental import pallas as pl
from jax.experimental.pallas import tpu as pltpu

D_MODEL = 4096
HEAD_DIM = 128
ROPE_THETA = 10000.0
NORM_EPS = 1e-6
NEG_INF = -1e30
LOG2E = 1.4426950408889634
LN2 = 0.6931471805599453

DIL_CONFIGS = ((128, 1), (512, 4), (2048, 16))
A_HEADS = 8
A_WIDTH = A_HEADS * HEAD_DIM
A_RADIUS = 64

B_Q_HEADS = 16
B_KV_HEADS = 4
B_GROUP = B_Q_HEADS // B_KV_HEADS
B_RADIUS = 128

MEM_LEN = 256
M_HEADS = 4
M_HEAD_DIM = 256

TN = 1024
A_TILES = (0, 9)
B_TILES = (9, 13)
Z_TILES = (13, 17)
G_TILES = (17, 29)
ROT_Q, ROT_K, ROT_ID, ROT_M = range(4)

VMEM_LIMIT = 56 * 1024 * 1024
OUT_VMEM = 60 * 1024 * 1024


def _params(sem, vmem=VMEM_LIMIT):
    return pltpu.CompilerParams(dimension_semantics=sem, vmem_limit_bytes=vmem)


def _launch(host, guest=None):
    parts = [host] if guest is None else [host, guest]
    grid = host["grid"]
    if guest is not None:
        g0, g1 = guest["grid"]
        assert g0 * g1 == grid[0] * grid[1], (guest["grid"], grid)
        to_guest = lambda i, j: divmod(i * grid[1] + j, g1)
        remap = lambda spec: pl.BlockSpec(
            spec.block_shape, (lambda i, j, f=spec.index_map: f(*to_guest(i, j))),
            memory_space=spec.memory_space) if spec.index_map is not None else spec
        guest = dict(guest, in_specs=[remap(sp) for sp in guest["in_specs"]],
                     out_specs=[remap(sp) for sp in guest["out_specs"]])
        parts = [host, guest]
    n_in = [len(p["in_specs"]) for p in parts]
    n_out = [len(p["out_specs"]) for p in parts]
    n_scr = [len(p.get("scratch", [])) for p in parts]

    def body(*refs):
        refs = list(refs)
        take = lambda n: [refs.pop(0) for _ in range(n)]
        ins = [take(n) for n in n_in]
        outs = [take(n) for n in n_out]
        scr = [take(n) for n in n_scr]
        ids = (pl.program_id(0), pl.program_id(1)) if len(grid) == 2 else (pl.program_id(0),)
        gens = [host["stages"](ids, ins[0], outs[0], scr[0])]
        if guest is not None:
            gens.insert(0, guest["stages"](to_guest(*ids), ins[1], outs[1], scr[1]))
        while gens:
            gens = [g for g in gens if next(g, "done") != "done"]

    res = pl.pallas_call(
        body,
        grid=grid,
        in_specs=[sp for p in parts for sp in p["in_specs"]],
        out_specs=[sp for p in parts for sp in p["out_specs"]],
        out_shape=[sh for p in parts for sh in p["out_shape"]],
        scratch_shapes=[sc for p in parts for sc in p.get("scratch", [])],
        compiler_params=_params(("arbitrary",) * len(grid)),
        name=host["name"] if guest is None else host["name"] + "_" + guest["name"],
    )(*[a for p in parts for a in p["args"]])
    if guest is None:
        return res
    return res[:n_out[0]], res[n_out[0]:]


def _rmsnorm_kernel(x_ref, g_ref, o_ref):
    x = x_ref[...]
    ms = jnp.mean(x * x, axis=-1, keepdims=True)
    o_ref[...] = ((x * lax.rsqrt(ms + NORM_EPS)) * g_ref[...]).astype(o_ref.dtype)


def _rmsnorm(x, g, out_dtype, tm=256):
    m, d = x.shape
    return pl.pallas_call(
        _rmsnorm_kernel,
        grid=(m // tm,),
        in_specs=[pl.BlockSpec((tm, d), lambda i: (i, 0)),
                  pl.BlockSpec((1, d), lambda i: (0, 0))],
        out_specs=pl.BlockSpec((tm, d), lambda i: (i, 0)),
        out_shape=jax.ShapeDtypeStruct((m, d), out_dtype),
        compiler_params=_params(("parallel",)),
        name="rmsnorm",
    )(x, g.reshape(1, d))


def _sigmoid(x):
    return 0.5 * jnp.tanh(0.5 * x) + 0.5


def _proj_stages(ids, ins, outs, scratch, *, epi, dil, cast):
    del ids
    h_ref, w_ref = ins[:2]
    o_ref = outs[0]
    tm = h_ref.shape[0]
    if cast:
        outs[1][...] = ins[-1][...].astype(outs[1].dtype)
    if epi == "sigmoid_parts":
        nq = 4
        wq = TN // nq
        for q in range(nq):
            cols = slice(q * wq, (q + 1) * wq)
            acc = jnp.dot(h_ref[...], w_ref[:, cols], preferred_element_type=jnp.float32)
            o_ref[0, :, cols] = _sigmoid(acc).astype(o_ref.dtype)
            yield
        return
    yield
    acc = jnp.dot(h_ref[...], w_ref[...], preferred_element_type=jnp.float32)
    if epi == "none":
        o_ref[0] = acc.astype(o_ref.dtype)
    elif epi == "silu":
        o_ref[0] = (acc * _sigmoid(acc)).astype(o_ref.dtype)
    elif epi == "sigmoid":
        o_ref[0] = _sigmoid(acc).astype(o_ref.dtype)
    else:
        cos_lo, sin_lo, cos_hi, sin_hi = ins[2:6]
        heads = [slice(hh * HEAD_DIM, (hh + 1) * HEAD_DIM) for hh in range(TN // HEAD_DIM)]
        for hh, sl in enumerate(heads):
            cos_ref, sin_ref = (cos_lo, sin_lo) if hh < len(heads) // 2 else (cos_hi, sin_hi)
            x = acc[:, sl]
            x = x * cos_ref[...] + pltpu.roll(x, HEAD_DIM // 2, 1) * sin_ref[...]
            if dil > 1:
                o_ref[:, :, sl] = jnp.swapaxes(x.reshape(tm // dil, dil, HEAD_DIM), 0, 1).astype(o_ref.dtype)
            else:
                o_ref[0, :, sl] = x.astype(o_ref.dtype)


def _select(j, values):
    if len(set(values)) == 1:
        return values[0]
    out = values[-1]
    for t in range(len(values) - 2, -1, -1):
        out = jnp.where(j == t, values[t], out)
    return out


CAST_ROWS = 512


def _proj(h, w, tiles, epi, *, batch, tm, dil=1, tabs=None, sets=None, cast=None, guest=None, name):
    m, k = h.shape
    seq = m // batch
    col_tiles = range(*tiles)
    lo, step, nt = col_tiles.start, col_tiles.step, len(col_tiles)
    assert seq % tm == 0 and tm % dil == 0 and (dil == 1 or epi == "rot")
    bps = seq // tm
    in_specs = [pl.BlockSpec((tm, k), lambda i, j: (i, 0)),
                pl.BlockSpec((k, TN), lambda i, j: (0, lo + step * j))]
    args = [h, w]
    if epi == "rot":
        for half in sets:
            assert len(half) == nt
            spec = pl.BlockSpec((None, tm, HEAD_DIM),
                                lambda i, j, half=half: (_select(j, half), i % bps, 0))
            in_specs += [spec, spec]
            args += list(tabs)
    out_specs = [pl.BlockSpec((None, None, dil, tm // dil, TN), lambda i, j: (j, i // bps, 0, i % bps, 0))]
    out_shape = [jax.ShapeDtypeStruct((nt, batch, dil, seq // dil, TN), jnp.bfloat16)]
    if cast is not None:
        src, src_tiles = cast
        src_tiles = range(*src_tiles)
        cr = min(CAST_ROWS, src.shape[0])
        cw = min(TN, src.shape[1])
        row_blocks = src.shape[0] // cr
        n_chunks = row_blocks * len(src_tiles)
        assert n_chunks <= (m // tm) * nt, "not enough grid steps to cast this weight group"
        chunk = lambda i, j: jnp.minimum(i * nt + j, n_chunks - 1)
        in_specs.append(pl.BlockSpec(
            (cr, cw), lambda i, j: (chunk(i, j) % row_blocks,
                                    src_tiles.start + src_tiles.step * (chunk(i, j) // row_blocks))))
        args.append(src)
        out_specs.append(pl.BlockSpec(
            (cr, cw), lambda i, j: (chunk(i, j) % row_blocks, chunk(i, j) // row_blocks)))
        out_shape.append(jax.ShapeDtypeStruct((src.shape[0], len(src_tiles) * cw), jnp.bfloat16))
    host = dict(
        stages=functools.partial(_proj_stages, epi=epi, dil=dil, cast=cast is not None),
        grid=(m // tm, nt), in_specs=in_specs, args=args, out_specs=out_specs, out_shape=out_shape,
        name=name)
    if guest is not None:
        outs, guest_outs = _launch(host, guest)
        return (outs if cast is not None else outs[0]), guest_outs
    outs = _launch(host)
    return outs if cast is not None else outs[0]


A_TQ = 128
A_STEP = 512
A_HALO = 64
LSE_LANES = 128


def _window_rows(prev_ref, cur_ref, next_ref, lo, hi, cols):
    n = cur_ref.shape[0]
    parts = []
    if lo < 0:
        parts.append(prev_ref[:, cols])
    parts.append(cur_ref[max(lo, 0):min(hi, n), cols])
    if hi > n:
        parts.append(next_ref[:, cols])
    return jnp.concatenate(parts, axis=0) if len(parts) > 1 else parts[0]


def _band_bias(tq, halo, radius, start, seq):
    nk = tq + 2 * halo
    r = lax.broadcasted_iota(jnp.int32, (tq, nk), 0)
    c = lax.broadcasted_iota(jnp.int32, (tq, nk), 1)
    kpos = start - halo + c
    valid = (jnp.abs(c - halo - r) <= radius) & (kpos >= 0) & (kpos < seq)
    return jnp.where(valid, 0.0, NEG_INF)


def _attn_a_kernel(q_ref, kp_ref, kc_ref, kn_ref, vp_ref, vc_ref, vn_ref, o_ref, lse_ref, *, seq):
    step = q_ref.shape[0]
    lane = lax.broadcasted_iota(jnp.int32, (A_TQ, LSE_LANES), 1)
    for b in range(step // A_TQ):
        rows = slice(b * A_TQ, (b + 1) * A_TQ)
        lo, hi = b * A_TQ - A_HALO, (b + 1) * A_TQ + A_HALO
        bias = _band_bias(A_TQ, A_HALO, A_RADIUS, pl.program_id(1) * step + b * A_TQ, seq)
        lse_all = jnp.zeros((A_TQ, LSE_LANES), jnp.float32)
        for hh in range(A_HEADS):
            sl = slice(hh * HEAD_DIM, (hh + 1) * HEAD_DIM)
            k = _window_rows(kp_ref, kc_ref, kn_ref, lo, hi, sl)
            v = _window_rows(vp_ref, vc_ref, vn_ref, lo, hi, sl)
            s = lax.dot_general(q_ref[rows, sl], k, (((1,), (1,)), ((), ())),
                                preferred_element_type=jnp.float32) + bias
            m = jnp.max(s, axis=1, keepdims=True)
            e = jnp.exp2(s - m)
            den = jnp.sum(e, axis=1, keepdims=True)
            o = jnp.dot(e.astype(jnp.bfloat16), v, preferred_element_type=jnp.float32)
            o_ref[rows, sl] = (o * (1.0 / den)).astype(o_ref.dtype)
            lse_all = jnp.where(lane == hh, m * LN2 + jnp.log(den), lse_all)
        lse_ref[rows, :] = lse_all


def _attn_a(qkv, n, seq, offs):
    qo, ko, vo = offs
    step = min(A_STEP, seq)
    nh = seq // A_HALO
    ratio = step // A_HALO
    cur = lambda off: pl.BlockSpec((None, step, A_WIDTH), lambda s, i: (off + s, i, 0))
    prev = lambda off: pl.BlockSpec(
        (None, A_HALO, A_WIDTH), lambda s, i: (off + s, jnp.maximum(ratio * i - 1, 0), 0))
    nxt = lambda off: pl.BlockSpec(
        (None, A_HALO, A_WIDTH), lambda s, i: (off + s, jnp.minimum(ratio * (i + 1), nh - 1), 0))
    return pl.pallas_call(
        functools.partial(_attn_a_kernel, seq=seq),
        grid=(n, seq // step),
        in_specs=[cur(qo), prev(ko), cur(ko), nxt(ko), prev(vo), cur(vo), nxt(vo)],
        out_specs=[pl.BlockSpec((None, step, A_WIDTH), lambda s, i: (s, i, 0)),
                   pl.BlockSpec((None, step, LSE_LANES), lambda s, i: (s, i, 0))],
        out_shape=[jax.ShapeDtypeStruct((n, seq, A_WIDTH), jnp.bfloat16),
                   jax.ShapeDtypeStruct((n, seq, LSE_LANES), jnp.float32)],
        compiler_params=_params(("parallel", "parallel")),
        name="attn_a",
    )(qkv, qkv, qkv, qkv, qkv, qkv, qkv)


def _merge_a_kernel(c0_ref, c1_ref, c2_ref, cl0_ref, cl1_ref, cl2_ref, z_ref, u_ref,
                    o1_ref, o2_ref, l1_ref, l2_ref):
    tm = u_ref.shape[0]
    o0_ref, l0_ref = c0_ref.at[0], cl0_ref.at[0]
    for c_ref, cl_ref, o_ref, l_ref in ((c1_ref, cl1_ref, o1_ref, l1_ref), (c2_ref, cl2_ref, o2_ref, l2_ref)):
        dil = c_ref.shape[0]
        for r in range(dil):
            rows = pl.ds(r, tm // dil, stride=dil)
            for hh in range(A_HEADS):
                o_ref[hh, rows, :] = c_ref[r, :, hh * HEAD_DIM:(hh + 1) * HEAD_DIM].astype(jnp.float32)
            l_ref[rows, :] = cl_ref[r]
    l0, l1, l2 = l0_ref[...], l1_ref[...], l2_ref[...]
    m = jnp.maximum(jnp.maximum(l0, l1), l2)
    e0, e1, e2 = jnp.exp(l0 - m), jnp.exp(l1 - m), jnp.exp(l2 - m)
    inv = 1.0 / (e0 + e1 + e2)
    a0, a1, a2 = e0 * inv, e1 * inv, e2 * inv
    for hh in range(A_HEADS):
        sl = slice(hh * HEAD_DIM, (hh + 1) * HEAD_DIM)
        col = slice(hh, hh + 1)
        o = a0[:, col] * o0_ref[:, sl] + a1[:, col] * o1_ref[hh] + a2[:, col] * o2_ref[hh]
        u_ref[:, sl] = (o * z_ref[:, sl].astype(jnp.float32)).astype(u_ref.dtype)


def _merge_a(outs, lses, pz, batch, seq, tm=512):
    bps = seq // tm
    cls = lambda dil, width: pl.BlockSpec((None, dil, tm // dil, width), lambda i: (i // bps, 0, i % bps, 0))
    dils = [d for _, d in DIL_CONFIGS]
    return pl.pallas_call(
        _merge_a_kernel,
        grid=(batch * bps,),
        in_specs=([cls(d, A_WIDTH) for d in dils] + [cls(d, LSE_LANES) for d in dils]
                  + [pl.BlockSpec((None, tm, TN), lambda i: (0, i, 0))]),
        out_specs=pl.BlockSpec((tm, A_WIDTH), lambda i: (i, 0)),
        out_shape=jax.ShapeDtypeStruct((batch * seq, A_WIDTH), jnp.bfloat16),
        scratch_shapes=[pltpu.VMEM((A_HEADS, tm, HEAD_DIM), jnp.float32)] * 2
                       + [pltpu.VMEM((tm, LSE_LANES), jnp.float32)] * 2,
        compiler_params=_params(("parallel",)),
        name="merge_a",
    )(*outs, *lses, pz)


B_TQ = 128


def _attn_b_stages(ids, ins, outs, scratch, *, seq):
    sink_ref, q0_ref, q1_ref, kvp_ref, kvc_ref, kvn_ref, z0_ref, z1_ref = ins
    u_ref, = outs
    step = u_ref.shape[0]
    ones = jnp.ones((3 * B_TQ, HEAD_DIM), jnp.bfloat16)
    kv_half = B_KV_HEADS * HEAD_DIM
    units = [(b, kh) for b in range(step // B_TQ) for kh in range(B_KV_HEADS)]

    def scores(b, kh):
        rows = slice(b * B_TQ, (b + 1) * B_TQ)
        lo, hi = (b - 1) * B_TQ, (b + 2) * B_TQ
        q_ref = q0_ref if kh < 2 else q1_ref
        base = (kh % 2) * B_GROUP * HEAD_DIM
        heads = [slice(base + g * HEAD_DIM, base + (g + 1) * HEAD_DIM) for g in range(B_GROUP)]
        k = _window_rows(kvp_ref, kvc_ref, kvn_ref, lo, hi, slice(kh * HEAD_DIM, (kh + 1) * HEAD_DIM))
        q = jnp.concatenate([q_ref[rows, sl] for sl in heads], axis=0)
        return lax.dot_general(q, k, (((1,), (1,)), ((), ())), preferred_element_type=jnp.float32)

    def finish(b, kh, s):
        rows = slice(b * B_TQ, (b + 1) * B_TQ)
        lo, hi = (b - 1) * B_TQ, (b + 2) * B_TQ
        z_ref = z0_ref if kh < 2 else z1_ref
        base = (kh % 2) * B_GROUP * HEAD_DIM
        heads = [slice(base + g * HEAD_DIM, base + (g + 1) * HEAD_DIM) for g in range(B_GROUP)]
        bias = _band_bias(B_TQ, B_TQ, B_RADIUS, ids[1] * step + b * B_TQ, seq)
        v = _window_rows(kvp_ref, kvc_ref, kvn_ref, lo, hi,
                         slice(kv_half + kh * HEAD_DIM, kv_half + (kh + 1) * HEAD_DIM))
        es, ms, sks = [], [], []
        for g in range(B_GROUP):
            sg = s[g * B_TQ:(g + 1) * B_TQ] + bias
            sk = sink_ref[kh * B_GROUP + g] * LOG2E
            m = jnp.maximum(jnp.max(sg, axis=1, keepdims=True), sk)
            es.append(jnp.exp2(sg - m).astype(jnp.bfloat16))
            ms.append(m)
            sks.append(sk)
        ov = jnp.dot(jnp.concatenate(es, axis=0), jnp.concatenate([v, ones], axis=1),
                     preferred_element_type=jnp.float32)
        for g in range(B_GROUP):
            og = ov[g * B_TQ:(g + 1) * B_TQ]
            den = og[:, HEAD_DIM:HEAD_DIM + 1] + jnp.exp2(sks[g] - ms[g])
            col = (kh * B_GROUP + g) * HEAD_DIM
            u_ref[rows, col:col + HEAD_DIM] = (
                og[:, :HEAD_DIM] * (1.0 / den) * z_ref[rows, heads[g]].astype(jnp.float32)
            ).astype(u_ref.dtype)

    per_round = 4
    pending = [scores(*u) for u in units[:per_round]]
    yield
    for r in range(0, len(units), per_round):
        for u, sc in zip(units[r:r + per_round], pending):
            finish(*u, sc)
        pending = [scores(*u) for u in units[r + per_round:r + 2 * per_round]]
        yield


def _attn_b(sink, pb, pz, batch, seq, step):
    nq = seq // step
    nh = seq // B_TQ
    ratio = step // B_TQ
    row = lambda slot: pl.BlockSpec((None, step, TN), lambda b, i: (slot, b * nq + i, 0))
    kv_prev = pl.BlockSpec((None, B_TQ, TN), lambda b, i: (2, b * nh + jnp.maximum(ratio * i - 1, 0), 0))
    kv_next = pl.BlockSpec((None, B_TQ, TN),
                           lambda b, i: (2, b * nh + jnp.minimum(ratio * (i + 1), nh - 1), 0))
    width = B_Q_HEADS * HEAD_DIM
    return dict(
        stages=functools.partial(_attn_b_stages, seq=seq),
        grid=(batch, nq),
        in_specs=[pl.BlockSpec(memory_space=pltpu.SMEM),
                  row(0), row(1), kv_prev, row(2), kv_next, row(1), row(2)],
        args=[sink, pb, pb, pb, pb, pb, pz, pz],
        out_specs=[pl.BlockSpec((step, width), lambda b, i: (b * nq + i, 0))],
        out_shape=[jax.ShapeDtypeStruct((batch * seq, width), jnp.bfloat16)],
        name="attn_b")


M_TQ = 512


def _attn_m_kernel(q_ref, k_ref, v_ref, z_ref, u_ref):
    for hh in range(M_HEADS):
        sl = slice(hh * M_HEAD_DIM, (hh + 1) * M_HEAD_DIM)
        s = lax.dot_general(q_ref[:, sl], k_ref[:, sl], (((1,), (1,)), ((), ())),
                            preferred_element_type=jnp.float32)
        m = jnp.max(s, axis=1, keepdims=True)
        e = jnp.exp2(s - m)
        den = jnp.sum(e, axis=1, keepdims=True)
        o = jnp.dot(e.astype(jnp.bfloat16), v_ref[:, sl], preferred_element_type=jnp.float32) * (1.0 / den)
        u_ref[:, sl] = (o * z_ref[:, sl].astype(jnp.float32)).astype(u_ref.dtype)


def _attn_m(pb, kv_mem, pz, batch, seq):
    nq = seq // M_TQ
    width = M_HEADS * M_HEAD_DIM
    return pl.pallas_call(
        _attn_m_kernel,
        grid=(batch, nq),
        in_specs=[pl.BlockSpec((None, M_TQ, TN), lambda b, i: (3, b * nq + i, 0)),
                  pl.BlockSpec((None, MEM_LEN, TN), lambda b, i: (0, b, 0)),
                  pl.BlockSpec((None, MEM_LEN, TN), lambda b, i: (1, b, 0)),
                  pl.BlockSpec((None, M_TQ, TN), lambda b, i: (3, b * nq + i, 0))],
        out_specs=pl.BlockSpec((M_TQ, width), lambda b, i: (b * nq + i, 0)),
        out_shape=jax.ShapeDtypeStruct((batch * seq, width), jnp.bfloat16),
        compiler_params=_params(("parallel", "parallel")),
        name="attn_m",
    )(pb, kv_mem, kv_mem, pz)


def _branch_kernel(ua_ref, ub_ref, um_ref, w_ref, ga_ref, gb_ref, gm_ref, u_ref):
    a_hi = A_WIDTH
    b_hi = A_WIDTH + B_Q_HEADS * HEAD_DIM
    dot = functools.partial(jnp.dot, preferred_element_type=jnp.float32)
    acc = ga_ref[...].astype(jnp.float32) * dot(ua_ref[...], w_ref[:a_hi, :])
    acc += gb_ref[...].astype(jnp.float32) * dot(ub_ref[...], w_ref[a_hi:b_hi, :])
    acc += gm_ref[...].astype(jnp.float32) * dot(um_ref[...], w_ref[b_hi:, :])
    u_ref[...] = acc.astype(u_ref.dtype)


def _branch(ua, ub, um, w_branch, gates, tm=512):
    m = ua.shape[0]
    k = w_branch.shape[0]
    n_tiles = D_MODEL // TN
    act = lambda width: pl.BlockSpec((tm, width), lambda j, i: (i, 0))
    gate = pl.BlockSpec((None, tm, TN), lambda j, i: (j, i, 0))
    return pl.pallas_call(
        _branch_kernel,
        grid=(n_tiles, m // tm),
        in_specs=[act(ua.shape[1]), act(ub.shape[1]), act(um.shape[1]),
                  pl.BlockSpec((k, TN), lambda j, i: (0, j)),
                  gate, gate, gate],
        out_specs=pl.BlockSpec((tm, TN), lambda j, i: (i, j)),
        out_shape=jax.ShapeDtypeStruct((m, D_MODEL), jnp.bfloat16),
        compiler_params=_params(("parallel", "parallel")),
        name="branch_proj",
    )(ua, ub, um, w_branch, *gates)


def _out_kernel(u_ref, w_ref, x_ref, g_ref, y_ref):
    y = x_ref[...] + jnp.dot(u_ref[...], w_ref[...], preferred_element_type=jnp.float32)
    inv = lax.rsqrt(jnp.mean(y * y, axis=1, keepdims=True) + NORM_EPS)
    y_ref[...] = (y * inv) * g_ref[...]


def _out_proj(u, w_out, x, g_final, tm=256):
    m, k = u.shape
    n = x.shape[1]
    return pl.pallas_call(
        _out_kernel,
        grid=(m // tm,),
        in_specs=[pl.BlockSpec((tm, k), lambda i: (i, 0)),
                  pl.BlockSpec((k, n), lambda i: (0, 0), pipeline_mode=pl.Buffered(1)),
                  pl.BlockSpec((tm, n), lambda i: (i, 0)),
                  pl.BlockSpec((1, n), lambda i: (0, 0))],
        out_specs=pl.BlockSpec((tm, n), lambda i: (i, 0)),
        out_shape=jax.ShapeDtypeStruct((m, n), jnp.float32),
        compiler_params=_params(("arbitrary",), vmem=OUT_VMEM),
        name="out_proj",
    )(u, w_out, x, g_final.reshape(1, n))


def _rope_tables(seq):
    inv_freq = ROPE_THETA ** (-jnp.arange(0, HEAD_DIM, 2, dtype=jnp.float32) / HEAD_DIM)
    ang = jnp.arange(seq, dtype=jnp.float32)[:, None] * inv_freq[None, :]
    cos, sin = jnp.cos(ang), jnp.sin(ang)
    cos, sin = jnp.concatenate([cos, cos], axis=1), jnp.concatenate([-sin, sin], axis=1)
    one, zero = jnp.ones_like(cos), jnp.zeros_like(sin)
    q_scale = HEAD_DIM ** -0.5 * LOG2E
    m_scale = M_HEAD_DIM ** -0.5 * LOG2E
    return (jnp.stack([cos * q_scale, cos, one, one * m_scale]),
            jnp.stack([sin * q_scale, sin, zero, zero]))


def _mix(h, x2, hm, pb, pz, ub, gates, w, tabs, g_final, batch, seq):
    kv_mem = _proj(hm, w["mem"], (0, 2), "none", batch=1, tm=batch * MEM_LEN, name="proj_mem")
    kv_mem = kv_mem.reshape(2, batch * MEM_LEN, TN)

    outs, lses = [], []
    for gi, (_, dil) in enumerate(DIL_CONFIGS):
        sd = seq // dil
        n = batch * dil
        a_sets = ((ROT_Q, ROT_K, ROT_ID),) * 2
        qkv = _proj(h, w["a"][gi], (0, 3), "rot", batch=batch, tm=1024, dil=dil, tabs=tabs, sets=a_sets,
                    name=f"proj_a{gi}")
        o, lse = _attn_a(qkv.reshape(3 * n, sd, A_WIDTH), n, sd, (0, n, 2 * n))
        outs.append(o.reshape(batch, dil, sd, A_WIDTH))
        lses.append(lse.reshape(batch, dil, sd, LSE_LANES))
    ua = _merge_a(outs, lses, pz, batch, seq)

    um = _attn_m(pb, kv_mem, pz, batch, seq)

    u = _branch(ua, ub, um, w["branch"], gates)
    return _out_proj(u, w["out"], x2, g_final).reshape(batch, seq, x2.shape[1])


def kernel(x_prompt, x_sample, mem_prompt, mem_sample, g_norm, w_in, attn_sink, g_mem, w_mem_kv, w_branch, w_out, g_final):
    bf = jnp.bfloat16
    w_in, w_mem_kv, w_branch, w_out = w_in[0], w_mem_kv[0], w_branch[0], w_out[0]
    trunks = []
    tabs = _rope_tables(max(x_prompt.shape[1], x_sample.shape[1]))
    for x, mem in ((x_prompt, mem_prompt), (x_sample, mem_sample)):
        batch, seq, d = x.shape
        x2 = x.reshape(batch * seq, d)
        trunks.append(dict(
            batch=batch, seq=seq, x2=x2, tabs=tabs,
            h=_rmsnorm(x2, g_norm[0], bf),
            hm=_rmsnorm(mem.reshape(batch * MEM_LEN, d), g_mem[0], bf)))

    w_b = w_in[:, B_TILES[0] * TN:B_TILES[1] * TN].astype(bf)
    b_sets = ((ROT_Q, ROT_Q, ROT_K, ROT_M), (ROT_Q, ROT_Q, ROT_ID, ROT_M))
    g_lo = G_TILES[0]
    n_dil = len(DIL_CONFIGS)
    out_tiles = -(-w_out.shape[1] // TN)

    def proj(t, wts, epi, cast, name, **kw):
        out, w_next = _proj(t["h"], wts, (0, 4), epi, batch=t["batch"], tm=1024, cast=cast, name=name, **kw)
        return out.reshape(4, t["batch"] * t["seq"], TN), w_next

    p, s_ = trunks
    pb_p, w_z = proj(p, w_b, "rot", (w_in, Z_TILES), "proj_b", tabs=p["tabs"], sets=b_sets)
    pb_s, w_ga = proj(s_, w_b, "rot", (w_in, (g_lo, g_lo + 4)), "proj_b", tabs=s_["tabs"], sets=b_sets)
    pz_p, w_gb = proj(p, w_z, "silu", (w_in, (g_lo + 4, g_lo + 8)), "proj_z")
    pz_s, w_gm = proj(s_, w_z, "silu", (w_in, (g_lo + 8, g_lo + 12)), "proj_z")
    ga_p, w_a0 = proj(p, w_ga, "sigmoid", (w_in, (0, A_TILES[1], n_dil)), "proj_ga")
    ga_s, w_a1 = proj(s_, w_ga, "sigmoid", (w_in, (1, A_TILES[1], n_dil)), "proj_ga")
    gb_p, w_a2 = proj(p, w_gb, "sigmoid", (w_in, (2, A_TILES[1], n_dil)), "proj_gb")
    gb_s, w_br = proj(s_, w_gb, "sigmoid", (w_branch, (0, D_MODEL // TN)), "proj_gb")

    def proj_with_attn_b(t, wts, cast, pb, pz):
        rows = t["batch"] * t["seq"]
        guest = _attn_b(attn_sink[0], pb, pz, t["batch"], t["seq"], step=rows // (rows // 1024 * 4))
        (out, w_next), (ub,) = _proj(t["h"], wts, (0, 4), "sigmoid_parts", batch=t["batch"], tm=1024,
                                     cast=cast, guest=guest, name="proj_gm")
        return out.reshape(4, rows, TN), w_next, ub

    gm_p, w_o, ub_p = proj_with_attn_b(p, w_gm, (w_out, (0, out_tiles)), pb_p, pz_p)
    gm_s, w_m, ub_s = proj_with_attn_b(s_, w_gm, (w_mem_kv, (0, 2)), pb_s, pz_s)

    w = dict(a=(w_a0, w_a1, w_a2), branch=w_br, out=w_o, mem=w_m)
    outs = []
    for t, pb, pz, gates, ub in ((p, pb_p, pz_p, (ga_p, gb_p, gm_p), ub_p),
                                 (s_, pb_s, pz_s, (ga_s, gb_s, gm_s), ub_s)):
        outs.append(_mix(t["h"], t["x2"], t["hm"], pb, pz, ub, gates, w, t["tabs"], g_final,
                         t["batch"], t["seq"]))
    return tuple(outs)
```

```python
import functools

import jax
import jax.numpy as jnp
from jax import lax
from jax.experimental import pallas as pl
from jax.experimental.pallas import tpu as pltpu

D_MODEL = 4096
HEAD_DIM = 128
ROPE_THETA = 10000.0
NORM_EPS = 1e-6
NEG_INF = -1e30
LOG2E = 1.4426950408889634
LN2 = 0.6931471805599453

DIL_CONFIGS = ((128, 1), (512, 4), (2048, 16))
A_HEADS = 8
A_WIDTH = A_HEADS * HEAD_DIM
A_RADIUS = 64

B_Q_HEADS = 16
B_KV_HEADS = 4
B_GROUP = B_Q_HEADS // B_KV_HEADS
B_RADIUS = 128

MEM_LEN = 256
M_HEADS = 4
M_HEAD_DIM = 256

TN = 1024
A_TILES = (0, 9)
B_TILES = (9, 13)
Z_TILES = (13, 17)
G_TILES = (17, 29)
ROT_Q, ROT_K, ROT_ID, ROT_M = range(4)

VMEM_LIMIT = 56 * 1024 * 1024
OUT_VMEM = 60 * 1024 * 1024


def _params(sem, vmem=VMEM_LIMIT):
    return pltpu.CompilerParams(dimension_semantics=sem, vmem_limit_bytes=vmem)


def _launch(host, guest=None):
    parts = [host] if guest is None else [host, guest]
    grid = host["grid"]
    if guest is not None:
        g0, g1 = guest["grid"]
        assert g0 * g1 == grid[0] * grid[1], (guest["grid"], grid)
        to_guest = lambda i, j: divmod(i * grid[1] + j, g1)
        remap = lambda spec: pl.BlockSpec(
            spec.block_shape, (lambda i, j, f=spec.index_map: f(*to_guest(i, j))),
            memory_space=spec.memory_space) if spec.index_map is not None else spec
        guest = dict(guest, in_specs=[remap(sp) for sp in guest["in_specs"]],
                     out_specs=[remap(sp) for sp in guest["out_specs"]])
        parts = [host, guest]
    n_in = [len(p["in_specs"]) for p in parts]
    n_out = [len(p["out_specs"]) for p in parts]
    n_scr = [len(p.get("scratch", [])) for p in parts]

    def body(*refs):
        refs = list(refs)
        take = lambda n: [refs.pop(0) for _ in range(n)]
        ins = [take(n) for n in n_in]
        outs = [take(n) for n in n_out]
        scr = [take(n) for n in n_scr]
        ids = (pl.program_id(0), pl.program_id(1)) if len(grid) == 2 else (pl.program_id(0),)
        gens = [host["stages"](ids, ins[0], outs[0], scr[0])]
        if guest is not None:
            gens.insert(0, guest["stages"](to_guest(*ids), ins[1], outs[1], scr[1]))
        while gens:
            gens = [g for g in gens if next(g, "done") != "done"]

    res = pl.pallas_call(
        body,
        grid=grid,
        in_specs=[sp for p in parts for sp in p["in_specs"]],
        out_specs=[sp for p in parts for sp in p["out_specs"]],
        out_shape=[sh for p in parts for sh in p["out_shape"]],
        scratch_shapes=[sc for p in parts for sc in p.get("scratch", [])],
        compiler_params=_params(("arbitrary",) * len(grid)),
        name=host["name"] if guest is None else host["name"] + "_" + guest["name"],
    )(*[a for p in parts for a in p["args"]])
    if guest is None:
        return res
    return res[:n_out[0]], res[n_out[0]:]


def _rmsnorm_kernel(x_ref, g_ref, o_ref):
    x = x_ref[...]
    ms = jnp.mean(x * x, axis=-1, keepdims=True)
    o_ref[...] = ((x * lax.rsqrt(ms + NORM_EPS)) * g_ref[...]).astype(o_ref.dtype)


def _rmsnorm(x, g, out_dtype, tm=256):
    m, d = x.shape
    return pl.pallas_call(
        _rmsnorm_kernel,
        grid=(m // tm,),
        in_specs=[pl.BlockSpec((tm, d), lambda i: (i, 0)),
                  pl.BlockSpec((1, d), lambda i: (0, 0))],
        out_specs=pl.BlockSpec((tm, d), lambda i: (i, 0)),
        out_shape=jax.ShapeDtypeStruct((m, d), out_dtype),
        compiler_params=_params(("parallel",)),
        name="rmsnorm",
    )(x, g.reshape(1, d))


def _sigmoid(x):
    return 0.5 * jnp.tanh(0.5 * x) + 0.5


def _proj_stages(ids, ins, outs, scratch, *, epi, dil, cast):
    del ids
    h_ref, w_ref = ins[:2]
    o_ref = outs[0]
    tm = h_ref.shape[0]
    if cast:
        outs[1][...] = ins[-1][...].astype(outs[1].dtype)
    if epi == "sigmoid_parts":
        nq = 4
        wq = TN // nq
        for q in range(nq):
            cols = slice(q * wq, (q + 1) * wq)
            acc = jnp.dot(h_ref[...], w_ref[:, cols], preferred_element_type=jnp.float32)
            o_ref[0, :, cols] = _sigmoid(acc).astype(o_ref.dtype)
            yield
        return
    yield
    acc = jnp.dot(h_ref[...], w_ref[...], preferred_element_type=jnp.float32)
    if epi == "none":
        o_ref[0] = acc.astype(o_ref.dtype)
    elif epi == "silu":
        o_ref[0] = (acc * _sigmoid(acc)).astype(o_ref.dtype)
    elif epi == "sigmoid":
        o_ref[0] = _sigmoid(acc).astype(o_ref.dtype)
    else:
        cos_lo, sin_lo, cos_hi, sin_hi = ins[2:6]
        heads = [slice(hh * HEAD_DIM, (hh + 1) * HEAD_DIM) for hh in range(TN // HEAD_DIM)]
        for hh, sl in enumerate(heads):
            cos_ref, sin_ref = (cos_lo, sin_lo) if hh < len(heads) // 2 else (cos_hi, sin_hi)
            x = acc[:, sl]
            x = x * cos_ref[...] + pltpu.roll(x, HEAD_DIM // 2, 1) * sin_ref[...]
            if dil > 1:
                o_ref[:, :, sl] = jnp.swapaxes(x.reshape(tm // dil, dil, HEAD_DIM), 0, 1).astype(o_ref.dtype)
            else:
                o_ref[0, :, sl] = x.astype(o_ref.dtype)


def _norm_proj_kernel(x_ref, g_ref, w_ref, cos_ref, sin_ref, h_ref, o_ref, *, sets):
    x = x_ref[...]
    h = ((x * lax.rsqrt(jnp.mean(x * x, axis=-1, keepdims=True) + NORM_EPS)) * g_ref[...]).astype(h_ref.dtype)
    h_ref[...] = h
    heads_per_tile = TN // HEAD_DIM
    for t in range(o_ref.shape[0]):
        acc = jnp.dot(h, w_ref[:, t * TN:(t + 1) * TN], preferred_element_type=jnp.float32)
        for hh in range(heads_per_tile):
            sl = slice(hh * HEAD_DIM, (hh + 1) * HEAD_DIM)
            tab = sets[hh * 2 // heads_per_tile][t]
            xh = acc[:, sl]
            o_ref[t, :, sl] = (xh * cos_ref[tab] + pltpu.roll(xh, HEAD_DIM // 2, 1) * sin_ref[tab]).astype(o_ref.dtype)


def _norm_proj(x, g, w, tabs, sets, *, batch, tm=256):
    m, d = x.shape
    seq = m // batch
    bps = seq // tm
    nt = w.shape[1] // TN
    n_sets = tabs[0].shape[0]
    tab_spec = pl.BlockSpec((n_sets, tm, HEAD_DIM), lambda i: (0, i % bps, 0))
    return pl.pallas_call(
        functools.partial(_norm_proj_kernel, sets=sets),
        grid=(m // tm,),
        in_specs=[pl.BlockSpec((tm, d), lambda i: (i, 0)),
                  pl.BlockSpec((1, d), lambda i: (0, 0)),
                  pl.BlockSpec(w.shape, lambda i: (0, 0), pipeline_mode=pl.Buffered(1)),
                  tab_spec, tab_spec],
        out_specs=[pl.BlockSpec((tm, d), lambda i: (i, 0)),
                   pl.BlockSpec((nt, tm, TN), lambda i: (0, i, 0))],
        out_shape=[jax.ShapeDtypeStruct((m, d), jnp.bfloat16),
                   jax.ShapeDtypeStruct((nt, m, TN), jnp.bfloat16)],
        compiler_params=_params(("arbitrary",), vmem=OUT_VMEM),
        name="norm_proj_b",
    )(x, g.reshape(1, d), w, *tabs)


def _select(j, values):
    if len(set(values)) == 1:
        return values[0]
    out = values[-1]
    for t in range(len(values) - 2, -1, -1):
        out = jnp.where(j == t, values[t], out)
    return out


CAST_ROWS = 512


def _proj(h, w, tiles, epi, *, batch, tm, dil=1, tabs=None, sets=None, cast=None, guest=None, name):
    m, k = h.shape
    seq = m // batch
    col_tiles = range(*tiles)
    lo, step, nt = col_tiles.start, col_tiles.step, len(col_tiles)
    assert seq % tm == 0 and tm % dil == 0 and (dil == 1 or epi == "rot")
    bps = seq // tm
    in_specs = [pl.BlockSpec((tm, k), lambda i, j: (i, 0)),
                pl.BlockSpec((k, TN), lambda i, j: (0, lo + step * j))]
    args = [h, w]
    if epi == "rot":
        for half in sets:
            assert len(half) == nt
            spec = pl.BlockSpec((None, tm, HEAD_DIM),
                                lambda i, j, half=half: (_select(j, half), i % bps, 0))
            in_specs += [spec, spec]
            args += list(tabs)
    out_specs = [pl.BlockSpec((None, None, dil, tm // dil, TN), lambda i, j: (j, i // bps, 0, i % bps, 0))]
    out_shape = [jax.ShapeDtypeStruct((nt, batch, dil, seq // dil, TN), jnp.bfloat16)]
    if cast is not None:
        src, src_tiles = cast
        src_tiles = range(*src_tiles)
        cr = min(CAST_ROWS, src.shape[0])
        cw = min(TN, src.shape[1])
        row_blocks = src.shape[0] // cr
        n_chunks = row_blocks * len(src_tiles)
        assert n_chunks <= (m // tm) * nt, "not enough grid steps to cast this weight group"
        chunk = lambda i, j: jnp.minimum(i * nt + j, n_chunks - 1)
        in_specs.append(pl.BlockSpec(
            (cr, cw), lambda i, j: (chunk(i, j) % row_blocks,
                                    src_tiles.start + src_tiles.step * (chunk(i, j) // row_blocks))))
        args.append(src)
        out_specs.append(pl.BlockSpec(
            (cr, cw), lambda i, j: (chunk(i, j) % row_blocks, chunk(i, j) // row_blocks)))
        out_shape.append(jax.ShapeDtypeStruct((src.shape[0], len(src_tiles) * cw), jnp.bfloat16))
    host = dict(
        stages=functools.partial(_proj_stages, epi=epi, dil=dil, cast=cast is not None),
        grid=(m // tm, nt), in_specs=in_specs, args=args, out_specs=out_specs, out_shape=out_shape,
        name=name)
    if guest is not None:
        outs, guest_outs = _launch(host, guest)
        return (outs if cast is not None else outs[0]), guest_outs
    outs = _launch(host)
    return outs if cast is not None else outs[0]


A_TQ = 128
A_STEP = 512
A_HALO = 64
LSE_LANES = 128


def _window_rows(prev_ref, cur_ref, next_ref, lo, hi, cols):
    n = cur_ref.shape[0]
    parts = []
    if lo < 0:
        parts.append(prev_ref[:, cols])
    parts.append(cur_ref[max(lo, 0):min(hi, n), cols])
    if hi > n:
        parts.append(next_ref[:, cols])
    return jnp.concatenate(parts, axis=0) if len(parts) > 1 else parts[0]


def _band_bias(tq, halo, radius, start, seq):
    nk = tq + 2 * halo
    r = lax.broadcasted_iota(jnp.int32, (tq, nk), 0)
    c = lax.broadcasted_iota(jnp.int32, (tq, nk), 1)
    kpos = start - halo + c
    valid = (jnp.abs(c - halo - r) <= radius) & (kpos >= 0) & (kpos < seq)
    return jnp.where(valid, 0.0, NEG_INF)


def _attn_a_kernel(q_ref, kp_ref, kc_ref, kn_ref, vp_ref, vc_ref, vn_ref, o_ref, lse_ref, *, seq):
    step = q_ref.shape[0]
    lane = lax.broadcasted_iota(jnp.int32, (A_TQ, LSE_LANES), 1)
    for b in range(step // A_TQ):
        rows = slice(b * A_TQ, (b + 1) * A_TQ)
        lo, hi = b * A_TQ - A_HALO, (b + 1) * A_TQ + A_HALO
        bias = _band_bias(A_TQ, A_HALO, A_RADIUS, pl.program_id(1) * step + b * A_TQ, seq)
        lse_all = jnp.zeros((A_TQ, LSE_LANES), jnp.float32)
        for hh in range(A_HEADS):
            sl = slice(hh * HEAD_DIM, (hh + 1) * HEAD_DIM)
            k = _window_rows(kp_ref, kc_ref, kn_ref, lo, hi, sl)
            v = _window_rows(vp_ref, vc_ref, vn_ref, lo, hi, sl)
            s = lax.dot_general(q_ref[rows, sl], k, (((1,), (1,)), ((), ())),
                                preferred_element_type=jnp.float32) + bias
            m = jnp.max(s, axis=1, keepdims=True)
            e = jnp.exp2(s - m)
            den = jnp.sum(e, axis=1, keepdims=True)
            o = jnp.dot(e.astype(jnp.bfloat16), v, preferred_element_type=jnp.float32)
            o_ref[rows, sl] = (o * (1.0 / den)).astype(o_ref.dtype)
            lse_all = jnp.where(lane == hh, m * LN2 + jnp.log(den), lse_all)
        lse_ref[rows, :] = lse_all


def _attn_a(qkv, n, seq, offs):
    qo, ko, vo = offs
    step = min(A_STEP, seq)
    nh = seq // A_HALO
    ratio = step // A_HALO
    cur = lambda off: pl.BlockSpec((None, step, A_WIDTH), lambda s, i: (off + s, i, 0))
    prev = lambda off: pl.BlockSpec(
        (None, A_HALO, A_WIDTH), lambda s, i: (off + s, jnp.maximum(ratio * i - 1, 0), 0))
    nxt = lambda off: pl.BlockSpec(
        (None, A_HALO, A_WIDTH), lambda s, i: (off + s, jnp.minimum(ratio * (i + 1), nh - 1), 0))
    return pl.pallas_call(
        functools.partial(_attn_a_kernel, seq=seq),
        grid=(n, seq // step),
        in_specs=[cur(qo), prev(ko), cur(ko), nxt(ko), prev(vo), cur(vo), nxt(vo)],
        out_specs=[pl.BlockSpec((None, step, A_WIDTH), lambda s, i: (s, i, 0)),
                   pl.BlockSpec((None, step, LSE_LANES), lambda s, i: (s, i, 0))],
        out_shape=[jax.ShapeDtypeStruct((n, seq, A_WIDTH), jnp.bfloat16),
                   jax.ShapeDtypeStruct((n, seq, LSE_LANES), jnp.float32)],
        compiler_params=_params(("parallel", "parallel")),
        name="attn_a",
    )(qkv, qkv, qkv, qkv, qkv, qkv, qkv)


def _merge_a_kernel(c0_ref, c1_ref, c2_ref, cl0_ref, cl1_ref, cl2_ref, z_ref, u_ref,
                    o1_ref, o2_ref, l1_ref, l2_ref):
    tm = u_ref.shape[0]
    o0_ref, l0_ref = c0_ref.at[0], cl0_ref.at[0]
    for c_ref, cl_ref, o_ref, l_ref in ((c1_ref, cl1_ref, o1_ref, l1_ref), (c2_ref, cl2_ref, o2_ref, l2_ref)):
        dil = c_ref.shape[0]
        for r in range(dil):
            rows = pl.ds(r, tm // dil, stride=dil)
            for hh in range(A_HEADS):
                o_ref[hh, rows, :] = c_ref[r, :, hh * HEAD_DIM:(hh + 1) * HEAD_DIM].astype(jnp.float32)
            l_ref[rows, :] = cl_ref[r]
    l0, l1, l2 = l0_ref[...], l1_ref[...], l2_ref[...]
    m = jnp.maximum(jnp.maximum(l0, l1), l2)
    e0, e1, e2 = jnp.exp(l0 - m), jnp.exp(l1 - m), jnp.exp(l2 - m)
    inv = 1.0 / (e0 + e1 + e2)
    a0, a1, a2 = e0 * inv, e1 * inv, e2 * inv
    for hh in range(A_HEADS):
        sl = slice(hh * HEAD_DIM, (hh + 1) * HEAD_DIM)
        col = slice(hh, hh + 1)
        o = a0[:, col] * o0_ref[:, sl] + a1[:, col] * o1_ref[hh] + a2[:, col] * o2_ref[hh]
        u_ref[:, sl] = (o * z_ref[:, sl].astype(jnp.float32)).astype(u_ref.dtype)


def _merge_a(outs, lses, pz, batch, seq, tm=512):
    bps = seq // tm
    cls = lambda dil, width: pl.BlockSpec((None, dil, tm // dil, width), lambda i: (i // bps, 0, i % bps, 0))
    dils = [d for _, d in DIL_CONFIGS]
    return pl.pallas_call(
        _merge_a_kernel,
        grid=(batch * bps,),
        in_specs=([cls(d, A_WIDTH) for d in dils] + [cls(d, LSE_LANES) for d in dils]
                  + [pl.BlockSpec((None, tm, TN), lambda i: (0, i, 0))]),
        out_specs=pl.BlockSpec((tm, A_WIDTH), lambda i: (i, 0)),
        out_shape=jax.ShapeDtypeStruct((batch * seq, A_WIDTH), jnp.bfloat16),
        scratch_shapes=[pltpu.VMEM((A_HEADS, tm, HEAD_DIM), jnp.float32)] * 2
                       + [pltpu.VMEM((tm, LSE_LANES), jnp.float32)] * 2,
        compiler_params=_params(("parallel",)),
        name="merge_a",
    )(*outs, *lses, pz)


B_TQ = 128


def _attn_b_stages(ids, ins, outs, scratch, *, seq):
    sink_ref, q0_ref, q1_ref, kvp_ref, kvc_ref, kvn_ref, z0_ref, z1_ref = ins
    u_ref, = outs
    step = u_ref.shape[0]
    ones = jnp.ones((3 * B_TQ, HEAD_DIM), jnp.bfloat16)
    kv_half = B_KV_HEADS * HEAD_DIM
    units = [(b, kh) for b in range(step // B_TQ) for kh in range(B_KV_HEADS)]

    def scores(b, kh):
        rows = slice(b * B_TQ, (b + 1) * B_TQ)
        lo, hi = (b - 1) * B_TQ, (b + 2) * B_TQ
        q_ref = q0_ref if kh < 2 else q1_ref
        base = (kh % 2) * B_GROUP * HEAD_DIM
        heads = [slice(base + g * HEAD_DIM, base + (g + 1) * HEAD_DIM) for g in range(B_GROUP)]
        k = _window_rows(kvp_ref, kvc_ref, kvn_ref, lo, hi, slice(kh * HEAD_DIM, (kh + 1) * HEAD_DIM))
        q = jnp.concatenate([q_ref[rows, sl] for sl in heads], axis=0)
        return lax.dot_general(q, k, (((1,), (1,)), ((), ())), preferred_element_type=jnp.float32)

    def finish(b, kh, s):
        rows = slice(b * B_TQ, (b + 1) * B_TQ)
        lo, hi = (b - 1) * B_TQ, (b + 2) * B_TQ
        z_ref = z0_ref if kh < 2 else z1_ref
        base = (kh % 2) * B_GROUP * HEAD_DIM
        heads = [slice(base + g * HEAD_DIM, base + (g + 1) * HEAD_DIM) for g in range(B_GROUP)]
        bias = _band_bias(B_TQ, B_TQ, B_RADIUS, ids[1] * step + b * B_TQ, seq)
        v = _window_rows(kvp_ref, kvc_ref, kvn_ref, lo, hi,
                         slice(kv_half + kh * HEAD_DIM, kv_half + (kh + 1) * HEAD_DIM))
        es, ms, sks = [], [], []
        for g in range(B_GROUP):
            sg = s[g * B_TQ:(g + 1) * B_TQ] + bias
            sk = sink_ref[kh * B_GROUP + g] * LOG2E
            m = jnp.maximum(jnp.max(sg, axis=1, keepdims=True), sk)
            es.append(jnp.exp2(sg - m).astype(jnp.bfloat16))
            ms.append(m)
            sks.append(sk)
        ov = jnp.dot(jnp.concatenate(es, axis=0), jnp.concatenate([v, ones], axis=1),
                     preferred_element_type=jnp.float32)
        for g in range(B_GROUP):
            og = ov[g * B_TQ:(g + 1) * B_TQ]
            den = og[:, HEAD_DIM:HEAD_DIM + 1] + jnp.exp2(sks[g] - ms[g])
            col = (kh * B_GROUP + g) * HEAD_DIM
            u_ref[rows, col:col + HEAD_DIM] = (
                og[:, :HEAD_DIM] * (1.0 / den) * z_ref[rows, heads[g]].astype(jnp.float32)
            ).astype(u_ref.dtype)

    per_round = 4
    pending = [scores(*u) for u in units[:per_round]]
    yield
    for r in range(0, len(units), per_round):
        for u, sc in zip(units[r:r + per_round], pending):
            finish(*u, sc)
        pending = [scores(*u) for u in units[r + per_round:r + 2 * per_round]]
        yield


def _attn_b(sink, pb, pz, batch, seq, step):
    nq = seq // step
    nh = seq // B_TQ
    ratio = step // B_TQ
    row = lambda slot: pl.BlockSpec((None, step, TN), lambda b, i: (slot, b * nq + i, 0))
    kv_prev = pl.BlockSpec((None, B_TQ, TN), lambda b, i: (2, b * nh + jnp.maximum(ratio * i - 1, 0), 0))
    kv_next = pl.BlockSpec((None, B_TQ, TN),
                           lambda b, i: (2, b * nh + jnp.minimum(ratio * (i + 1), nh - 1), 0))
    width = B_Q_HEADS * HEAD_DIM
    return dict(
        stages=functools.partial(_attn_b_stages, seq=seq),
        grid=(batch, nq),
        in_specs=[pl.BlockSpec(memory_space=pltpu.SMEM),
                  row(0), row(1), kv_prev, row(2), kv_next, row(1), row(2)],
        args=[sink, pb, pb, pb, pb, pb, pz, pz],
        out_specs=[pl.BlockSpec((step, width), lambda b, i: (b * nq + i, 0))],
        out_shape=[jax.ShapeDtypeStruct((batch * seq, width), jnp.bfloat16)],
        name="attn_b")


M_TQ = 512


def _attn_m_kernel(q_ref, k_ref, v_ref, z_ref, u_ref):
    for hh in range(M_HEADS):
        sl = slice(hh * M_HEAD_DIM, (hh + 1) * M_HEAD_DIM)
        s = lax.dot_general(q_ref[:, sl], k_ref[:, sl], (((1,), (1,)), ((), ())),
                            preferred_element_type=jnp.float32)
        m = jnp.max(s, axis=1, keepdims=True)
        e = jnp.exp2(s - m)
        den = jnp.sum(e, axis=1, keepdims=True)
        o = jnp.dot(e.astype(jnp.bfloat16), v_ref[:, sl], preferred_element_type=jnp.float32) * (1.0 / den)
        u_ref[:, sl] = (o * z_ref[:, sl].astype(jnp.float32)).astype(u_ref.dtype)


def _attn_m(pb, kv_mem, pz, batch, seq):
    nq = seq // M_TQ
    width = M_HEADS * M_HEAD_DIM
    return pl.pallas_call(
        _attn_m_kernel,
        grid=(batch, nq),
        in_specs=[pl.BlockSpec((None, M_TQ, TN), lambda b, i: (3, b * nq + i, 0)),
                  pl.BlockSpec((None, MEM_LEN, TN), lambda b, i: (0, b, 0)),
                  pl.BlockSpec((None, MEM_LEN, TN), lambda b, i: (1, b, 0)),
                  pl.BlockSpec((None, M_TQ, TN), lambda b, i: (3, b * nq + i, 0))],
        out_specs=pl.BlockSpec((M_TQ, width), lambda b, i: (b * nq + i, 0)),
        out_shape=jax.ShapeDtypeStruct((batch * seq, width), jnp.bfloat16),
        compiler_params=_params(("parallel", "parallel")),
        name="attn_m",
    )(pb, kv_mem, kv_mem, pz)


def _branch_kernel(ua_ref, ub_ref, um_ref, w_ref, ga_ref, gb_ref, gm_ref, u_ref):
    a_hi = A_WIDTH
    b_hi = A_WIDTH + B_Q_HEADS * HEAD_DIM
    dot = functools.partial(jnp.dot, preferred_element_type=jnp.float32)
    acc = ga_ref[...].astype(jnp.float32) * dot(ua_ref[...], w_ref[:a_hi, :])
    acc += gb_ref[...].astype(jnp.float32) * dot(ub_ref[...], w_ref[a_hi:b_hi, :])
    acc += gm_ref[...].astype(jnp.float32) * dot(um_ref[...], w_ref[b_hi:, :])
    u_ref[...] = acc.astype(u_ref.dtype)


def _branch(ua, ub, um, w_branch, gates, tm=512):
    m = ua.shape[0]
    k = w_branch.shape[0]
    n_tiles = D_MODEL // TN
    act = lambda width: pl.BlockSpec((tm, width), lambda j, i: (i, 0))
    gate = pl.BlockSpec((None, tm, TN), lambda j, i: (j, i, 0))
    return pl.pallas_call(
        _branch_kernel,
        grid=(n_tiles, m // tm),
        in_specs=[act(ua.shape[1]), act(ub.shape[1]), act(um.shape[1]),
                  pl.BlockSpec((k, TN), lambda j, i: (0, j)),
                  gate, gate, gate],
        out_specs=pl.BlockSpec((tm, TN), lambda j, i: (i, j)),
        out_shape=jax.ShapeDtypeStruct((m, D_MODEL), jnp.bfloat16),
        compiler_params=_params(("parallel", "parallel")),
        name="branch_proj",
    )(ua, ub, um, w_branch, *gates)


def _out_kernel(u_ref, w_ref, x_ref, g_ref, y_ref):
    y = x_ref[...] + jnp.dot(u_ref[...], w_ref[...], preferred_element_type=jnp.float32)
    inv = lax.rsqrt(jnp.mean(y * y, axis=1, keepdims=True) + NORM_EPS)
    y_ref[...] = (y * inv) * g_ref[...]


def _out_proj(u, w_out, x, g_final, tm=256):
    m, k = u.shape
    n = x.shape[1]
    return pl.pallas_call(
        _out_kernel,
        grid=(m // tm,),
        in_specs=[pl.BlockSpec((tm, k), lambda i: (i, 0)),
                  pl.BlockSpec((k, n), lambda i: (0, 0), pipeline_mode=pl.Buffered(1)),
                  pl.BlockSpec((tm, n), lambda i: (i, 0)),
                  pl.BlockSpec((1, n), lambda i: (0, 0))],
        out_specs=pl.BlockSpec((tm, n), lambda i: (i, 0)),
        out_shape=jax.ShapeDtypeStruct((m, n), jnp.float32),
        compiler_params=_params(("arbitrary",), vmem=OUT_VMEM),
        name="out_proj",
    )(u, w_out, x, g_final.reshape(1, n))


def _rope_tables(seq):
    inv_freq = ROPE_THETA ** (-jnp.arange(0, HEAD_DIM, 2, dtype=jnp.float32) / HEAD_DIM)
    ang = jnp.arange(seq, dtype=jnp.float32)[:, None] * inv_freq[None, :]
    cos, sin = jnp.cos(ang), jnp.sin(ang)
    cos, sin = jnp.concatenate([cos, cos], axis=1), jnp.concatenate([-sin, sin], axis=1)
    one, zero = jnp.ones_like(cos), jnp.zeros_like(sin)
    q_scale = HEAD_DIM ** -0.5 * LOG2E
    m_scale = M_HEAD_DIM ** -0.5 * LOG2E
    return (jnp.stack([cos * q_scale, cos, one, one * m_scale]),
            jnp.stack([sin * q_scale, sin, zero, zero]))


def _mix(h, x2, hm, pb, pz, ub, gates, w, tabs, g_final, batch, seq, cast_mem=None):
    outs, lses = [], []
    for gi, (_, dil) in enumerate(DIL_CONFIGS):
        sd = seq // dil
        n = batch * dil
        a_sets = ((ROT_Q, ROT_K, ROT_ID),) * 2
        qkv = _proj(h, w["a"][gi], (0, 3), "rot", batch=batch, tm=1024, dil=dil, tabs=tabs, sets=a_sets,
                    cast=cast_mem if gi == 0 else None, name=f"proj_a{gi}")
        if gi == 0 and cast_mem is not None:
            qkv, w_mem = qkv
            w = dict(w, mem=w_mem)
        o, lse = _attn_a(qkv.reshape(3 * n, sd, A_WIDTH), n, sd, (0, n, 2 * n))
        outs.append(o.reshape(batch, dil, sd, A_WIDTH))
        lses.append(lse.reshape(batch, dil, sd, LSE_LANES))
    ua = _merge_a(outs, lses, pz, batch, seq)

    kv_mem = _proj(hm, w["mem"], (0, 2), "none", batch=1, tm=batch * MEM_LEN, name="proj_mem")
    um = _attn_m(pb, kv_mem.reshape(2, batch * MEM_LEN, TN), pz, batch, seq)

    u = _branch(ua, ub, um, w["branch"], gates)
    return _out_proj(u, w["out"], x2, g_final).reshape(batch, seq, x2.shape[1]), w["mem"]


def kernel(x_prompt, x_sample, mem_prompt, mem_sample, g_norm, w_in, attn_sink, g_mem, w_mem_kv, w_branch, w_out, g_final):
    bf = jnp.bfloat16
    w_in, w_mem_kv, w_branch, w_out = w_in[0], w_mem_kv[0], w_branch[0], w_out[0]
    trunks = []
    tabs = _rope_tables(max(x_prompt.shape[1], x_sample.shape[1]))
    for x, mem in ((x_prompt, mem_prompt), (x_sample, mem_sample)):
        batch, seq, d = x.shape
        x2 = x.reshape(batch * seq, d)
        trunks.append(dict(
            batch=batch, seq=seq, x2=x2, tabs=tabs,
            hm=_rmsnorm(mem.reshape(batch * MEM_LEN, d), g_mem[0], bf)))

    w_bz = w_in[:, B_TILES[0] * TN:Z_TILES[1] * TN].astype(bf)
    w_b, w_z = w_bz[:, :4 * TN], w_bz[:, 4 * TN:]
    b_sets = ((ROT_Q, ROT_Q, ROT_K, ROT_M), (ROT_Q, ROT_Q, ROT_ID, ROT_M))
    g_lo = G_TILES[0]
    n_dil = len(DIL_CONFIGS)
    out_tiles = -(-w_out.shape[1] // TN)

    def proj(t, wts, epi, cast, name, **kw):
        out, w_next = _proj(t["h"], wts, (0, 4), epi, batch=t["batch"], tm=1024, cast=cast, name=name, **kw)
        return out.reshape(4, t["batch"] * t["seq"], TN), w_next

    p, s_ = trunks
    for t in trunks:
        t["h"], t["pb"] = _norm_proj(t["x2"], g_norm[0], w_b, tabs, b_sets, batch=t["batch"])
    pb_p, pb_s = p["pb"], s_["pb"]
    pz_p, w_ga = proj(p, w_z, "silu", (w_in, (g_lo, g_lo + 4)), "proj_z")
    pz_s, w_gb = proj(s_, w_z, "silu", (w_in, (g_lo + 4, g_lo + 8)), "proj_z")
    ga_p, w_gm = proj(p, w_ga, "sigmoid", (w_in, (g_lo + 8, g_lo + 12)), "proj_ga")
    ga_s, w_a0 = proj(s_, w_ga, "sigmoid", (w_in, (0, A_TILES[1], n_dil)), "proj_ga")
    gb_p, w_a1 = proj(p, w_gb, "sigmoid", (w_in, (1, A_TILES[1], n_dil)), "proj_gb")
    gb_s, w_a2 = proj(s_, w_gb, "sigmoid", (w_in, (2, A_TILES[1], n_dil)), "proj_gb")

    def proj_with_attn_b(t, wts, cast, pb, pz):
        rows = t["batch"] * t["seq"]
        guest = _attn_b(attn_sink[0], pb, pz, t["batch"], t["seq"], step=rows // (rows // 1024 * 4))
        (out, w_next), (ub,) = _proj(t["h"], wts, (0, 4), "sigmoid_parts", batch=t["batch"], tm=1024,
                                     cast=cast, guest=guest, name="proj_gm")
        return out.reshape(4, rows, TN), w_next, ub

    gm_p, w_o, ub_p = proj_with_attn_b(p, w_gm, (w_out, (0, out_tiles)), pb_p, pz_p)
    gm_s, w_br, ub_s = proj_with_attn_b(s_, w_gm, (w_branch, (0, D_MODEL // TN)), pb_s, pz_s)

    w = dict(a=(w_a0, w_a1, w_a2), branch=w_br, out=w_o)
    y_p, w_m = _mix(p["h"], p["x2"], p["hm"], pb_p, pz_p, ub_p, (ga_p, gb_p, gm_p), w, tabs, g_final,
                    p["batch"], p["seq"], cast_mem=(w_mem_kv, (0, 2)))
    y_s, _ = _mix(s_["h"], s_["x2"], s_["hm"], pb_s, pz_s, ub_s, (ga_s, gb_s, gm_s), dict(w, mem=w_m), tabs,
                  g_final, s_["batch"], s_["seq"])
    return y_p, y_s
```

```python
import functools

import jax
import jax.numpy as jnp
from jax import lax
from jax.experimental import pallas as pl
from jax.experimental.pallas import tpu as pltpu

D_MODEL = 4096
HEAD_DIM = 128
ROPE_THETA = 10000.0
NORM_EPS = 1e-6
NEG_INF = -1e30
LOG2E = 1.4426950408889634
LN2 = 0.6931471805599453

DIL_CONFIGS = ((128, 1), (512, 4), (2048, 16))
A_HEADS = 8
A_WIDTH = A_HEADS * HEAD_DIM
A_RADIUS = 64

B_Q_HEADS = 16
B_KV_HEADS = 4
B_GROUP = B_Q_HEADS // B_KV_HEADS
B_RADIUS = 128

MEM_LEN = 256
M_HEADS = 4
M_HEAD_DIM = 256

TN = 1024
A_TILES = (0, 9)
B_TILES = (9, 13)
Z_TILES = (13, 17)
G_TILES = (17, 29)
ROT_Q, ROT_K, ROT_ID, ROT_M = range(4)

VMEM_LIMIT = 56 * 1024 * 1024
OUT_VMEM = 60 * 1024 * 1024


def _params(sem, vmem=VMEM_LIMIT):
    return pltpu.CompilerParams(dimension_semantics=sem, vmem_limit_bytes=vmem)


def _launch(host, guest=None):
    parts = [host] if guest is None else [host, guest]
    grid = host["grid"]
    if guest is not None:
        g0, g1 = guest["grid"]
        assert g0 * g1 == grid[0] * grid[1], (guest["grid"], grid)
        to_guest = lambda i, j: divmod(i * grid[1] + j, g1)
        remap = lambda spec: pl.BlockSpec(
            spec.block_shape, (lambda i, j, f=spec.index_map: f(*to_guest(i, j))),
            memory_space=spec.memory_space) if spec.index_map is not None else spec
        guest = dict(guest, in_specs=[remap(sp) for sp in guest["in_specs"]],
                     out_specs=[remap(sp) for sp in guest["out_specs"]])
        parts = [host, guest]
    n_in = [len(p["in_specs"]) for p in parts]
    n_out = [len(p["out_specs"]) for p in parts]
    n_scr = [len(p.get("scratch", [])) for p in parts]

    def body(*refs):
        refs = list(refs)
        take = lambda n: [refs.pop(0) for _ in range(n)]
        ins = [take(n) for n in n_in]
        outs = [take(n) for n in n_out]
        scr = [take(n) for n in n_scr]
        ids = (pl.program_id(0), pl.program_id(1)) if len(grid) == 2 else (pl.program_id(0),)
        gens = [host["stages"](ids, ins[0], outs[0], scr[0])]
        if guest is not None:
            gens.insert(0, guest["stages"](to_guest(*ids), ins[1], outs[1], scr[1]))
        while gens:
            gens = [g for g in gens if next(g, "done") != "done"]

    res = pl.pallas_call(
        body,
        grid=grid,
        in_specs=[sp for p in parts for sp in p["in_specs"]],
        out_specs=[sp for p in parts for sp in p["out_specs"]],
        out_shape=[sh for p in parts for sh in p["out_shape"]],
        scratch_shapes=[sc for p in parts for sc in p.get("scratch", [])],
        compiler_params=_params(("arbitrary",) * len(grid)),
        name=host["name"] if guest is None else host["name"] + "_" + guest["name"],
    )(*[a for p in parts for a in p["args"]])
    if guest is None:
        return res
    return res[:n_out[0]], res[n_out[0]:]


def _rmsnorm_kernel(x_ref, g_ref, o_ref):
    x = x_ref[...]
    ms = jnp.mean(x * x, axis=-1, keepdims=True)
    o_ref[...] = ((x * lax.rsqrt(ms + NORM_EPS)) * g_ref[...]).astype(o_ref.dtype)


def _rmsnorm(x, g, out_dtype, tm=256):
    m, d = x.shape
    return pl.pallas_call(
        _rmsnorm_kernel,
        grid=(m // tm,),
        in_specs=[pl.BlockSpec((tm, d), lambda i: (i, 0)),
                  pl.BlockSpec((1, d), lambda i: (0, 0))],
        out_specs=pl.BlockSpec((tm, d), lambda i: (i, 0)),
        out_shape=jax.ShapeDtypeStruct((m, d), out_dtype),
        compiler_params=_params(("parallel",)),
        name="rmsnorm",
    )(x, g.reshape(1, d))


def _sigmoid(x):
    return 0.5 * jnp.tanh(0.5 * x) + 0.5


def _proj_stages(ids, ins, outs, scratch, *, epi, dil, cast):
    del ids
    h_ref, w_ref = ins[:2]
    o_ref = outs[0]
    tm = h_ref.shape[0]
    if cast:
        outs[1][...] = ins[-1][...].astype(outs[1].dtype)
    if epi == "sigmoid_parts":
        nq = 4
        wq = TN // nq
        for q in range(nq):
            cols = slice(q * wq, (q + 1) * wq)
            acc = jnp.dot(h_ref[...], w_ref[:, cols], preferred_element_type=jnp.float32)
            o_ref[0, :, cols] = _sigmoid(acc).astype(o_ref.dtype)
            yield
        return
    yield
    acc = jnp.dot(h_ref[...], w_ref[...], preferred_element_type=jnp.float32)
    if epi == "none":
        o_ref[0] = acc.astype(o_ref.dtype)
    elif epi == "silu":
        o_ref[0] = (acc * _sigmoid(acc)).astype(o_ref.dtype)
    elif epi == "sigmoid":
        o_ref[0] = _sigmoid(acc).astype(o_ref.dtype)
    else:
        cos_lo, sin_lo, cos_hi, sin_hi = ins[2:6]
        heads = [slice(hh * HEAD_DIM, (hh + 1) * HEAD_DIM) for hh in range(TN // HEAD_DIM)]
        for hh, sl in enumerate(heads):
            cos_ref, sin_ref = (cos_lo, sin_lo) if hh < len(heads) // 2 else (cos_hi, sin_hi)
            x = acc[:, sl]
            x = x * cos_ref[...] + pltpu.roll(x, HEAD_DIM // 2, 1) * sin_ref[...]
            if dil > 1:
                o_ref[:, :, sl] = jnp.swapaxes(x.reshape(tm // dil, dil, HEAD_DIM), 0, 1).astype(o_ref.dtype)
            else:
                o_ref[0, :, sl] = x.astype(o_ref.dtype)


def _norm_proj_kernel(x_ref, g_ref, w_ref, cos_ref, sin_ref, h_ref, o_ref, *, sets):
    x = x_ref[...]
    h = ((x * lax.rsqrt(jnp.mean(x * x, axis=-1, keepdims=True) + NORM_EPS)) * g_ref[...]).astype(h_ref.dtype)
    h_ref[...] = h
    heads_per_tile = TN // HEAD_DIM
    for t in range(o_ref.shape[0]):
        acc = jnp.dot(h, w_ref[:, t * TN:(t + 1) * TN], preferred_element_type=jnp.float32)
        for hh in range(heads_per_tile):
            sl = slice(hh * HEAD_DIM, (hh + 1) * HEAD_DIM)
            tab = sets[hh * 2 // heads_per_tile][t]
            xh = acc[:, sl]
            o_ref[t, :, sl] = (xh * cos_ref[tab] + pltpu.roll(xh, HEAD_DIM // 2, 1) * sin_ref[tab]).astype(o_ref.dtype)


def _norm_proj(x, g, w, tabs, sets, *, batch, tm=256):
    m, d = x.shape
    seq = m // batch
    bps = seq // tm
    nt = w.shape[1] // TN
    n_sets = tabs[0].shape[0]
    tab_spec = pl.BlockSpec((n_sets, tm, HEAD_DIM), lambda i: (0, i % bps, 0))
    return pl.pallas_call(
        functools.partial(_norm_proj_kernel, sets=sets),
        grid=(m // tm,),
        in_specs=[pl.BlockSpec((tm, d), lambda i: (i, 0)),
                  pl.BlockSpec((1, d), lambda i: (0, 0)),
                  pl.BlockSpec(w.shape, lambda i: (0, 0), pipeline_mode=pl.Buffered(1)),
                  tab_spec, tab_spec],
        out_specs=[pl.BlockSpec((tm, d), lambda i: (i, 0)),
                   pl.BlockSpec((nt, tm, TN), lambda i: (0, i, 0))],
        out_shape=[jax.ShapeDtypeStruct((m, d), jnp.bfloat16),
                   jax.ShapeDtypeStruct((nt, m, TN), jnp.bfloat16)],
        compiler_params=_params(("arbitrary",), vmem=OUT_VMEM),
        name="norm_proj_b",
    )(x, g.reshape(1, d), w, *tabs)


def _select(j, values):
    if len(set(values)) == 1:
        return values[0]
    out = values[-1]
    for t in range(len(values) - 2, -1, -1):
        out = jnp.where(j == t, values[t], out)
    return out


CAST_ROWS = 512


def _proj(h, w, tiles, epi, *, batch, tm, dil=1, tabs=None, sets=None, cast=None, guest=None, name):
    m, k = h.shape
    seq = m // batch
    col_tiles = range(*tiles)
    lo, step, nt = col_tiles.start, col_tiles.step, len(col_tiles)
    assert seq % tm == 0 and tm % dil == 0 and (dil == 1 or epi == "rot")
    bps = seq // tm
    in_specs = [pl.BlockSpec((tm, k), lambda i, j: (i, 0)),
                pl.BlockSpec((k, TN), lambda i, j: (0, lo + step * j))]
    args = [h, w]
    if epi == "rot":
        for half in sets:
            assert len(half) == nt
            spec = pl.BlockSpec((None, tm, HEAD_DIM),
                                lambda i, j, half=half: (_select(j, half), i % bps, 0))
            in_specs += [spec, spec]
            args += list(tabs)
    out_specs = [pl.BlockSpec((None, None, dil, tm // dil, TN), lambda i, j: (j, i // bps, 0, i % bps, 0))]
    out_shape = [jax.ShapeDtypeStruct((nt, batch, dil, seq // dil, TN), jnp.bfloat16)]
    if cast is not None:
        src, src_tiles = cast
        src_tiles = range(*src_tiles)
        cr = min(CAST_ROWS, src.shape[0])
        cw = min(TN, src.shape[1])
        row_blocks = src.shape[0] // cr
        n_chunks = row_blocks * len(src_tiles)
        assert n_chunks <= (m // tm) * nt, "not enough grid steps to cast this weight group"
        chunk = lambda i, j: jnp.minimum(i * nt + j, n_chunks - 1)
        in_specs.append(pl.BlockSpec(
            (cr, cw), lambda i, j: (chunk(i, j) % row_blocks,
                                    src_tiles.start + src_tiles.step * (chunk(i, j) // row_blocks))))
        args.append(src)
        out_specs.append(pl.BlockSpec(
            (cr, cw), lambda i, j: (chunk(i, j) % row_blocks, chunk(i, j) // row_blocks)))
        out_shape.append(jax.ShapeDtypeStruct((src.shape[0], len(src_tiles) * cw), jnp.bfloat16))
    host = dict(
        stages=functools.partial(_proj_stages, epi=epi, dil=dil, cast=cast is not None),
        grid=(m // tm, nt), in_specs=in_specs, args=args, out_specs=out_specs, out_shape=out_shape,
        name=name)
    if guest is not None:
        outs, guest_outs = _launch(host, guest)
        return (outs if cast is not None else outs[0]), guest_outs
    outs = _launch(host)
    return outs if cast is not None else outs[0]


A_TQ = 128
A_STEP = 512
A_HALO = 64
LSE_LANES = 128


def _window_rows(prev_ref, cur_ref, next_ref, lo, hi, cols):
    n = cur_ref.shape[0]
    parts = []
    if lo < 0:
        parts.append(prev_ref[:, cols])
    parts.append(cur_ref[max(lo, 0):min(hi, n), cols])
    if hi > n:
        parts.append(next_ref[:, cols])
    return jnp.concatenate(parts, axis=0) if len(parts) > 1 else parts[0]


def _band_bias(tq, halo, radius, start, seq):
    nk = tq + 2 * halo
    r = lax.broadcasted_iota(jnp.int32, (tq, nk), 0)
    c = lax.broadcasted_iota(jnp.int32, (tq, nk), 1)
    kpos = start - halo + c
    valid = (jnp.abs(c - halo - r) <= radius) & (kpos >= 0) & (kpos < seq)
    return jnp.where(valid, 0.0, NEG_INF)


def _attn_a_kernel(q_ref, kp_ref, kc_ref, kn_ref, vp_ref, vc_ref, vn_ref, o_ref, lse_ref, *, seq):
    step = q_ref.shape[0]
    lane = lax.broadcasted_iota(jnp.int32, (A_TQ, LSE_LANES), 1)
    for b in range(step // A_TQ):
        rows = slice(b * A_TQ, (b + 1) * A_TQ)
        lo, hi = b * A_TQ - A_HALO, (b + 1) * A_TQ + A_HALO
        bias = _band_bias(A_TQ, A_HALO, A_RADIUS, pl.program_id(1) * step + b * A_TQ, seq)
        lse_all = jnp.zeros((A_TQ, LSE_LANES), jnp.float32)
        for hh in range(A_HEADS):
            sl = slice(hh * HEAD_DIM, (hh + 1) * HEAD_DIM)
            k = _window_rows(kp_ref, kc_ref, kn_ref, lo, hi, sl)
            v = _window_rows(vp_ref, vc_ref, vn_ref, lo, hi, sl)
            s = lax.dot_general(q_ref[rows, sl], k, (((1,), (1,)), ((), ())),
                                preferred_element_type=jnp.float32) + bias
            m = jnp.max(s, axis=1, keepdims=True)
            e = jnp.exp2(s - m)
            den = jnp.sum(e, axis=1, keepdims=True)
            o = jnp.dot(e.astype(jnp.bfloat16), v, preferred_element_type=jnp.float32)
            o_ref[rows, sl] = (o * (1.0 / den)).astype(o_ref.dtype)
            lse_all = jnp.where(lane == hh, m * LN2 + jnp.log(den), lse_all)
        lse_ref[rows, :] = lse_all


def _attn_a(qkv, n, seq, offs):
    qo, ko, vo = offs
    step = min(A_STEP, seq)
    nh = seq // A_HALO
    ratio = step // A_HALO
    cur = lambda off: pl.BlockSpec((None, step, A_WIDTH), lambda s, i: (off + s, i, 0))
    prev = lambda off: pl.BlockSpec(
        (None, A_HALO, A_WIDTH), lambda s, i: (off + s, jnp.maximum(ratio * i - 1, 0), 0))
    nxt = lambda off: pl.BlockSpec(
        (None, A_HALO, A_WIDTH), lambda s, i: (off + s, jnp.minimum(ratio * (i + 1), nh - 1), 0))
    return pl.pallas_call(
        functools.partial(_attn_a_kernel, seq=seq),
        grid=(n, seq // step),
        in_specs=[cur(qo), prev(ko), cur(ko), nxt(ko), prev(vo), cur(vo), nxt(vo)],
        out_specs=[pl.BlockSpec((None, step, A_WIDTH), lambda s, i: (s, i, 0)),
                   pl.BlockSpec((None, step, LSE_LANES), lambda s, i: (s, i, 0))],
        out_shape=[jax.ShapeDtypeStruct((n, seq, A_WIDTH), jnp.bfloat16),
                   jax.ShapeDtypeStruct((n, seq, LSE_LANES), jnp.float32)],
        compiler_params=_params(("parallel", "parallel")),
        name="attn_a",
    )(qkv, qkv, qkv, qkv, qkv, qkv, qkv)


def _merge_a_kernel(c0_ref, c1_ref, c2_ref, cl0_ref, cl1_ref, cl2_ref, z_ref, u_ref,
                    o1_ref, o2_ref, l1_ref, l2_ref):
    tm = u_ref.shape[0]
    o0_ref, l0_ref = c0_ref.at[0], cl0_ref.at[0]
    for c_ref, cl_ref, o_ref, l_ref in ((c1_ref, cl1_ref, o1_ref, l1_ref), (c2_ref, cl2_ref, o2_ref, l2_ref)):
        dil = c_ref.shape[0]
        for r in range(dil):
            rows = pl.ds(r, tm // dil, stride=dil)
            for hh in range(A_HEADS):
                o_ref[hh, rows, :] = c_ref[r, :, hh * HEAD_DIM:(hh + 1) * HEAD_DIM].astype(jnp.float32)
            l_ref[rows, :] = cl_ref[r]
    l0, l1, l2 = l0_ref[...], l1_ref[...], l2_ref[...]
    m = jnp.maximum(jnp.maximum(l0, l1), l2)
    e0, e1, e2 = jnp.exp(l0 - m), jnp.exp(l1 - m), jnp.exp(l2 - m)
    inv = 1.0 / (e0 + e1 + e2)
    a0, a1, a2 = e0 * inv, e1 * inv, e2 * inv
    for hh in range(A_HEADS):
        sl = slice(hh * HEAD_DIM, (hh + 1) * HEAD_DIM)
        col = slice(hh, hh + 1)
        o = a0[:, col] * o0_ref[:, sl] + a1[:, col] * o1_ref[hh] + a2[:, col] * o2_ref[hh]
        u_ref[:, sl] = (o * z_ref[:, sl].astype(jnp.float32)).astype(u_ref.dtype)


def _merge_a(outs, lses, pz, batch, seq, tm=512):
    bps = seq // tm
    cls = lambda dil, width: pl.BlockSpec((None, dil, tm // dil, width), lambda i: (i // bps, 0, i % bps, 0))
    dils = [d for _, d in DIL_CONFIGS]
    return pl.pallas_call(
        _merge_a_kernel,
        grid=(batch * bps,),
        in_specs=([cls(d, A_WIDTH) for d in dils] + [cls(d, LSE_LANES) for d in dils]
                  + [pl.BlockSpec((None, tm, TN), lambda i: (0, i, 0))]),
        out_specs=pl.BlockSpec((tm, A_WIDTH), lambda i: (i, 0)),
        out_shape=jax.ShapeDtypeStruct((batch * seq, A_WIDTH), jnp.bfloat16),
        scratch_shapes=[pltpu.VMEM((A_HEADS, tm, HEAD_DIM), jnp.float32)] * 2
                       + [pltpu.VMEM((tm, LSE_LANES), jnp.float32)] * 2,
        compiler_params=_params(("parallel",)),
        name="merge_a",
    )(*outs, *lses, pz)


B_TQ = 128


def _attn_b_stages(ids, ins, outs, scratch, *, seq):
    sink_ref, q0_ref, q1_ref, kvp_ref, kvc_ref, kvn_ref, z0_ref, z1_ref = ins
    u_ref, = outs
    step = u_ref.shape[0]
    ones = jnp.ones((3 * B_TQ, HEAD_DIM), jnp.bfloat16)
    kv_half = B_KV_HEADS * HEAD_DIM
    units = [(b, kh) for b in range(step // B_TQ) for kh in range(B_KV_HEADS)]

    def scores(b, kh):
        rows = slice(b * B_TQ, (b + 1) * B_TQ)
        lo, hi = (b - 1) * B_TQ, (b + 2) * B_TQ
        q_ref = q0_ref if kh < 2 else q1_ref
        base = (kh % 2) * B_GROUP * HEAD_DIM
        heads = [slice(base + g * HEAD_DIM, base + (g + 1) * HEAD_DIM) for g in range(B_GROUP)]
        k = _window_rows(kvp_ref, kvc_ref, kvn_ref, lo, hi, slice(kh * HEAD_DIM, (kh + 1) * HEAD_DIM))
        q = jnp.concatenate([q_ref[rows, sl] for sl in heads], axis=0)
        return lax.dot_general(q, k, (((1,), (1,)), ((), ())), preferred_element_type=jnp.float32)

    def finish(b, kh, s):
        rows = slice(b * B_TQ, (b + 1) * B_TQ)
        lo, hi = (b - 1) * B_TQ, (b + 2) * B_TQ
        z_ref = z0_ref if kh < 2 else z1_ref
        base = (kh % 2) * B_GROUP * HEAD_DIM
        heads = [slice(base + g * HEAD_DIM, base + (g + 1) * HEAD_DIM) for g in range(B_GROUP)]
        bias = _band_bias(B_TQ, B_TQ, B_RADIUS, ids[1] * step + b * B_TQ, seq)
        v = _window_rows(kvp_ref, kvc_ref, kvn_ref, lo, hi,
                         slice(kv_half + kh * HEAD_DIM, kv_half + (kh + 1) * HEAD_DIM))
        es, ms, sks = [], [], []
        for g in range(B_GROUP):
            sg = s[g * B_TQ:(g + 1) * B_TQ] + bias
            sk = sink_ref[kh * B_GROUP + g] * LOG2E
            m = jnp.maximum(jnp.max(sg, axis=1, keepdims=True), sk)
            es.append(jnp.exp2(sg - m).astype(jnp.bfloat16))
            ms.append(m)
            sks.append(sk)
        ov = jnp.dot(jnp.concatenate(es, axis=0), jnp.concatenate([v, ones], axis=1),
                     preferred_element_type=jnp.float32)
        for g in range(B_GROUP):
            og = ov[g * B_TQ:(g + 1) * B_TQ]
            den = og[:, HEAD_DIM:HEAD_DIM + 1] + jnp.exp2(sks[g] - ms[g])
            col = (kh * B_GROUP + g) * HEAD_DIM
            u_ref[rows, col:col + HEAD_DIM] = (
                og[:, :HEAD_DIM] * (1.0 / den) * z_ref[rows, heads[g]].astype(jnp.float32)
            ).astype(u_ref.dtype)

    per_round = 4
    pending = [scores(*u) for u in units[:per_round]]
    yield
    for r in range(0, len(units), per_round):
        for u, sc in zip(units[r:r + per_round], pending):
            finish(*u, sc)
        pending = [scores(*u) for u in units[r + per_round:r + 2 * per_round]]
        yield


def _attn_b(sink, pb, pz, batch, seq, step):
    nq = seq // step
    nh = seq // B_TQ
    ratio = step // B_TQ
    row = lambda slot: pl.BlockSpec((None, step, TN), lambda b, i: (slot, b * nq + i, 0))
    kv_prev = pl.BlockSpec((None, B_TQ, TN), lambda b, i: (2, b * nh + jnp.maximum(ratio * i - 1, 0), 0))
    kv_next = pl.BlockSpec((None, B_TQ, TN),
                           lambda b, i: (2, b * nh + jnp.minimum(ratio * (i + 1), nh - 1), 0))
    width = B_Q_HEADS * HEAD_DIM
    return dict(
        stages=functools.partial(_attn_b_stages, seq=seq),
        grid=(batch, nq),
        in_specs=[pl.BlockSpec(memory_space=pltpu.SMEM),
                  row(0), row(1), kv_prev, row(2), kv_next, row(1), row(2)],
        args=[sink, pb, pb, pb, pb, pb, pz, pz],
        out_specs=[pl.BlockSpec((step, width), lambda b, i: (b * nq + i, 0))],
        out_shape=[jax.ShapeDtypeStruct((batch * seq, width), jnp.bfloat16)],
        name="attn_b")


M_TQ = 512


def _attn_m_kernel(q_ref, k_ref, v_ref, z_ref, u_ref):
    for hh in range(M_HEADS):
        sl = slice(hh * M_HEAD_DIM, (hh + 1) * M_HEAD_DIM)
        s = lax.dot_general(q_ref[:, sl], k_ref[:, sl], (((1,), (1,)), ((), ())),
                            preferred_element_type=jnp.float32)
        m = jnp.max(s, axis=1, keepdims=True)
        e = jnp.exp2(s - m)
        den = jnp.sum(e, axis=1, keepdims=True)
        o = jnp.dot(e.astype(jnp.bfloat16), v_ref[:, sl], preferred_element_type=jnp.float32) * (1.0 / den)
        u_ref[:, sl] = (o * z_ref[:, sl].astype(jnp.float32)).astype(u_ref.dtype)


def _attn_m(pb, kv_mem, pz, batch, seq):
    nq = seq // M_TQ
    width = M_HEADS * M_HEAD_DIM
    return pl.pallas_call(
        _attn_m_kernel,
        grid=(batch, nq),
        in_specs=[pl.BlockSpec((None, M_TQ, TN), lambda b, i: (3, b * nq + i, 0)),
                  pl.BlockSpec((None, MEM_LEN, TN), lambda b, i: (0, b, 0)),
                  pl.BlockSpec((None, MEM_LEN, TN), lambda b, i: (1, b, 0)),
                  pl.BlockSpec((None, M_TQ, TN), lambda b, i: (3, b * nq + i, 0))],
        out_specs=pl.BlockSpec((M_TQ, width), lambda b, i: (b * nq + i, 0)),
        out_shape=jax.ShapeDtypeStruct((batch * seq, width), jnp.bfloat16),
        compiler_params=_params(("parallel", "parallel")),
        name="attn_m",
    )(pb, kv_mem, kv_mem, pz)


def _branch_kernel(ua_ref, ub_ref, um_ref, w_ref, ga_ref, gb_ref, gm_ref, u_ref):
    a_hi = A_WIDTH
    b_hi = A_WIDTH + B_Q_HEADS * HEAD_DIM
    dot = functools.partial(jnp.dot, preferred_element_type=jnp.float32)
    acc = ga_ref[...].astype(jnp.float32) * dot(ua_ref[...], w_ref[:a_hi, :])
    acc += gb_ref[...].astype(jnp.float32) * dot(ub_ref[...], w_ref[a_hi:b_hi, :])
    acc += gm_ref[...].astype(jnp.float32) * dot(um_ref[...], w_ref[b_hi:, :])
    u_ref[...] = acc.astype(u_ref.dtype)


def _branch(ua, ub, um, w_branch, gates, tm=512):
    m = ua.shape[0]
    k = w_branch.shape[0]
    n_tiles = D_MODEL // TN
    act = lambda width: pl.BlockSpec((tm, width), lambda j, i: (i, 0))
    gate = pl.BlockSpec((None, tm, TN), lambda j, i: (j, i, 0))
    return pl.pallas_call(
        _branch_kernel,
        grid=(n_tiles, m // tm),
        in_specs=[act(ua.shape[1]), act(ub.shape[1]), act(um.shape[1]),
                  pl.BlockSpec((k, TN), lambda j, i: (0, j)),
                  gate, gate, gate],
        out_specs=pl.BlockSpec((tm, TN), lambda j, i: (i, j)),
        out_shape=jax.ShapeDtypeStruct((m, D_MODEL), jnp.bfloat16),
        compiler_params=_params(("parallel", "parallel")),
        name="branch_proj",
    )(ua, ub, um, w_branch, *gates)


def _out_kernel(u_ref, w_ref, x_ref, g_ref, y_ref):
    y = x_ref[...] + jnp.dot(u_ref[...], w_ref[...], preferred_element_type=jnp.float32)
    inv = lax.rsqrt(jnp.mean(y * y, axis=1, keepdims=True) + NORM_EPS)
    y_ref[...] = (y * inv) * g_ref[...]


def _out_proj(u, w_out, x, g_final, tm=256):
    m, k = u.shape
    n = x.shape[1]
    return pl.pallas_call(
        _out_kernel,
        grid=(m // tm,),
        in_specs=[pl.BlockSpec((tm, k), lambda i: (i, 0)),
                  pl.BlockSpec((k, n), lambda i: (0, 0), pipeline_mode=pl.Buffered(1)),
                  pl.BlockSpec((tm, n), lambda i: (i, 0)),
                  pl.BlockSpec((1, n), lambda i: (0, 0))],
        out_specs=pl.BlockSpec((tm, n), lambda i: (i, 0)),
        out_shape=jax.ShapeDtypeStruct((m, n), jnp.float32),
        compiler_params=_params(("arbitrary",), vmem=OUT_VMEM),
        name="out_proj",
    )(u, w_out, x, g_final.reshape(1, n))


def _rope_tables(seq):
    inv_freq = ROPE_THETA ** (-jnp.arange(0, HEAD_DIM, 2, dtype=jnp.float32) / HEAD_DIM)
    ang = jnp.arange(seq, dtype=jnp.float32)[:, None] * inv_freq[None, :]
    cos, sin = jnp.cos(ang), jnp.sin(ang)
    cos, sin = jnp.concatenate([cos, cos], axis=1), jnp.concatenate([-sin, sin], axis=1)
    one, zero = jnp.ones_like(cos), jnp.zeros_like(sin)
    q_scale = HEAD_DIM ** -0.5 * LOG2E
    m_scale = M_HEAD_DIM ** -0.5 * LOG2E
    return (jnp.stack([cos * q_scale, cos, one, one * m_scale]),
            jnp.stack([sin * q_scale, sin, zero, zero]))


def _mix(h, x2, hm, pb, pz, ub, gates, w, tabs, g_final, batch, seq, cast_mem=None):
    outs, lses = [], []
    for gi, (_, dil) in enumerate(DIL_CONFIGS):
        sd = seq // dil
        n = batch * dil
        a_sets = ((ROT_Q, ROT_K, ROT_ID),) * 2
        qkv = _proj(h, w["a"][gi], (0, 3), "rot", batch=batch, tm=1024, dil=dil, tabs=tabs, sets=a_sets,
                    cast=cast_mem if gi == 0 else None, name=f"proj_a{gi}")
        if gi == 0 and cast_mem is not None:
            qkv, w_mem = qkv
            w = dict(w, mem=w_mem)
        o, lse = _attn_a(qkv.reshape(3 * n, sd, A_WIDTH), n, sd, (0, n, 2 * n))
        outs.append(o.reshape(batch, dil, sd, A_WIDTH))
        lses.append(lse.reshape(batch, dil, sd, LSE_LANES))
    ua = _merge_a(outs, lses, pz, batch, seq)

    kv_mem = _proj(hm, w["mem"], (0, 2), "none", batch=1, tm=batch * MEM_LEN, name="proj_mem")
    um = _attn_m(pb, kv_mem.reshape(2, batch * MEM_LEN, TN), pz, batch, seq)

    u = _branch(ua, ub, um, w["branch"], gates)
    return _out_proj(u, w["out"], x2, g_final).reshape(batch, seq, x2.shape[1]), w["mem"]


def kernel(x_prompt, x_sample, mem_prompt, mem_sample, g_norm, w_in, attn_sink, g_mem, w_mem_kv, w_branch, w_out, g_final):
    bf = jnp.bfloat16
    w_in, w_mem_kv, w_branch, w_out = w_in[0], w_mem_kv[0], w_branch[0], w_out[0]
    trunks = []
    tabs = _rope_tables(max(x_prompt.shape[1], x_sample.shape[1]))
    for x, mem in ((x_prompt, mem_prompt), (x_sample, mem_sample)):
        batch, seq, d = x.shape
        x2 = x.reshape(batch * seq, d)
        trunks.append(dict(
            batch=batch, seq=seq, x2=x2, tabs=tabs,
            hm=_rmsnorm(mem.reshape(batch * MEM_LEN, d), g_mem[0], bf)))

    w_b = w_in[:, B_TILES[0] * TN:B_TILES[1] * TN].astype(bf)
    w_z = w_in[:, Z_TILES[0] * TN:Z_TILES[1] * TN].astype(bf)
    b_sets = ((ROT_Q, ROT_Q, ROT_K, ROT_M), (ROT_Q, ROT_Q, ROT_ID, ROT_M))
    g_lo = G_TILES[0]
    n_dil = len(DIL_CONFIGS)
    out_tiles = -(-w_out.shape[1] // TN)

    def proj(t, wts, epi, cast, name, **kw):
        out, w_next = _proj(t["h"], wts, (0, 4), epi, batch=t["batch"], tm=1024, cast=cast, name=name, **kw)
        return out.reshape(4, t["batch"] * t["seq"], TN), w_next

    p, s_ = trunks
    for t in trunks:
        t["h"], t["pb"] = _norm_proj(t["x2"], g_norm[0], w_b, tabs, b_sets, batch=t["batch"])
    pb_p, pb_s = p["pb"], s_["pb"]
    pz_p, w_ga = proj(p, w_z, "silu", (w_in, (g_lo, g_lo + 4)), "proj_z")
    pz_s, w_gb = proj(s_, w_z, "silu", (w_in, (g_lo + 4, g_lo + 8)), "proj_z")
    ga_p, w_gm = proj(p, w_ga, "sigmoid", (w_in, (g_lo + 8, g_lo + 12)), "proj_ga")
    ga_s, w_a0 = proj(s_, w_ga, "sigmoid", (w_in, (0, A_TILES[1], n_dil)), "proj_ga")
    gb_p, w_a1 = proj(p, w_gb, "sigmoid", (w_in, (1, A_TILES[1], n_dil)), "proj_gb")
    gb_s, w_a2 = proj(s_, w_gb, "sigmoid", (w_in, (2, A_TILES[1], n_dil)), "proj_gb")

    def proj_with_attn_b(t, wts, cast, pb, pz):
        rows = t["batch"] * t["seq"]
        guest = _attn_b(attn_sink[0], pb, pz, t["batch"], t["seq"], step=rows // (rows // 1024 * 4))
        (out, w_next), (ub,) = _proj(t["h"], wts, (0, 4), "sigmoid_parts", batch=t["batch"], tm=1024,
                                     cast=cast, guest=guest, name="proj_gm")
        return out.reshape(4, rows, TN), w_next, ub

    gm_p, w_o, ub_p = proj_with_attn_b(p, w_gm, (w_out, (0, out_tiles)), pb_p, pz_p)
    gm_s, w_br, ub_s = proj_with_attn_b(s_, w_gm, (w_branch, (0, D_MODEL // TN)), pb_s, pz_s)

    w = dict(a=(w_a0, w_a1, w_a2), branch=w_br, out=w_o)
    y_p, w_m = _mix(p["h"], p["x2"], p["hm"], pb_p, pz_p, ub_p, (ga_p, gb_p, gm_p), w, tabs, g_final,
                    p["batch"], p["seq"], cast_mem=(w_mem_kv, (0, 2)))
    y_s, _ = _mix(s_["h"], s_["x2"], s_["hm"], pb_s, pz_s, ub_s, (ga_s, gb_s, gm_s), dict(w, mem=w_m), tabs,
                  g_final, s_["batch"], s_["seq"])
    return y_p, y_s
```

```python
import functools

import jax
import jax.numpy as jnp
from jax import lax
from jax.experimental import pallas as pl
from jax.experimental.pallas import tpu as pltpu

D_MODEL = 4096
HEAD_DIM = 128
ROPE_THETA = 10000.0
NORM_EPS = 1e-6
NEG_INF = -1e30
LOG2E = 1.4426950408889634
LN2 = 0.6931471805599453

DIL_CONFIGS = ((128, 1), (512, 4), (2048, 16))
A_HEADS = 8
A_WIDTH = A_HEADS * HEAD_DIM
A_RADIUS = 64

B_Q_HEADS = 16
B_KV_HEADS = 4
B_GROUP = B_Q_HEADS // B_KV_HEADS
B_RADIUS = 128

MEM_LEN = 256
M_HEADS = 4
M_HEAD_DIM = 256

TN = 1024
A_TILES = (0, 9)
B_TILES = (9, 13)
Z_TILES = (13, 17)
G_TILES = (17, 29)
ROT_Q, ROT_K, ROT_ID, ROT_M = range(4)

VMEM_LIMIT = 56 * 1024 * 1024
OUT_VMEM = 60 * 1024 * 1024


def _params(sem, vmem=VMEM_LIMIT):
    return pltpu.CompilerParams(dimension_semantics=sem, vmem_limit_bytes=vmem)


def _launch(host, guest=None):
    parts = [host] if guest is None else [host, guest]
    grid = host["grid"]
    if guest is not None:
        g0, g1 = guest["grid"]
        assert g0 * g1 == grid[0] * grid[1], (guest["grid"], grid)
        to_guest = lambda i, j: divmod(i * grid[1] + j, g1)
        remap = lambda spec: pl.BlockSpec(
            spec.block_shape, (lambda i, j, f=spec.index_map: f(*to_guest(i, j))),
            memory_space=spec.memory_space) if spec.index_map is not None else spec
        guest = dict(guest, in_specs=[remap(sp) for sp in guest["in_specs"]],
                     out_specs=[remap(sp) for sp in guest["out_specs"]])
        parts = [host, guest]
    n_in = [len(p["in_specs"]) for p in parts]
    n_out = [len(p["out_specs"]) for p in parts]
    n_scr = [len(p.get("scratch", [])) for p in parts]

    def body(*refs):
        refs = list(refs)
        take = lambda n: [refs.pop(0) for _ in range(n)]
        ins = [take(n) for n in n_in]
        outs = [take(n) for n in n_out]
        scr = [take(n) for n in n_scr]
        ids = (pl.program_id(0), pl.program_id(1)) if len(grid) == 2 else (pl.program_id(0),)
        gens = [host["stages"](ids, ins[0], outs[0], scr[0])]
        if guest is not None:
            gens.insert(0, guest["stages"](to_guest(*ids), ins[1], outs[1], scr[1]))
        while gens:
            gens = [g for g in gens if next(g, "done") != "done"]

    res = pl.pallas_call(
        body,
        grid=grid,
        in_specs=[sp for p in parts for sp in p["in_specs"]],
        out_specs=[sp for p in parts for sp in p["out_specs"]],
        out_shape=[sh for p in parts for sh in p["out_shape"]],
        scratch_shapes=[sc for p in parts for sc in p.get("scratch", [])],
        compiler_params=_params(("arbitrary",) * len(grid)),
        name=host["name"] if guest is None else host["name"] + "_" + guest["name"],
    )(*[a for p in parts for a in p["args"]])
    if guest is None:
        return res
    return res[:n_out[0]], res[n_out[0]:]


def _rmsnorm_kernel(x_ref, g_ref, o_ref):
    x = x_ref[...]
    ms = jnp.mean(x * x, axis=-1, keepdims=True)
    o_ref[...] = ((x * lax.rsqrt(ms + NORM_EPS)) * g_ref[...]).astype(o_ref.dtype)


def _rmsnorm(x, g, out_dtype, tm=256):
    m, d = x.shape
    return pl.pallas_call(
        _rmsnorm_kernel,
        grid=(m // tm,),
        in_specs=[pl.BlockSpec((tm, d), lambda i: (i, 0)),
                  pl.BlockSpec((1, d), lambda i: (0, 0))],
        out_specs=pl.BlockSpec((tm, d), lambda i: (i, 0)),
        out_shape=jax.ShapeDtypeStruct((m, d), out_dtype),
        compiler_params=_params(("parallel",)),
        name="rmsnorm",
    )(x, g.reshape(1, d))


def _sigmoid(x):
    return 0.5 * jnp.tanh(0.5 * x) + 0.5


def _proj_stages(ids, ins, outs, scratch, *, epi, dil, cast):
    del ids
    h_ref, w_ref = ins[:2]
    o_ref = outs[0]
    tm = h_ref.shape[0]
    if cast:
        outs[1][...] = ins[-1][...].astype(outs[1].dtype)
    if epi == "sigmoid_parts":
        nq = 4
        wq = TN // nq
        for q in range(nq):
            cols = slice(q * wq, (q + 1) * wq)
            acc = jnp.dot(h_ref[...], w_ref[:, cols], preferred_element_type=jnp.float32)
            o_ref[0, :, cols] = _sigmoid(acc).astype(o_ref.dtype)
            yield
        return
    yield
    acc = jnp.dot(h_ref[...], w_ref[...], preferred_element_type=jnp.float32)
    if epi == "none":
        o_ref[0] = acc.astype(o_ref.dtype)
    elif epi == "silu":
        o_ref[0] = (acc * _sigmoid(acc)).astype(o_ref.dtype)
    elif epi == "sigmoid":
        o_ref[0] = _sigmoid(acc).astype(o_ref.dtype)
    else:
        cos_lo, sin_lo, cos_hi, sin_hi = ins[2:6]
        heads = [slice(hh * HEAD_DIM, (hh + 1) * HEAD_DIM) for hh in range(TN // HEAD_DIM)]
        for hh, sl in enumerate(heads):
            cos_ref, sin_ref = (cos_lo, sin_lo) if hh < len(heads) // 2 else (cos_hi, sin_hi)
            x = acc[:, sl]
            x = x * cos_ref[...] + pltpu.roll(x, HEAD_DIM // 2, 1) * sin_ref[...]
            if dil > 1:
                o_ref[:, :, sl] = jnp.swapaxes(x.reshape(tm // dil, dil, HEAD_DIM), 0, 1).astype(o_ref.dtype)
            else:
                o_ref[0, :, sl] = x.astype(o_ref.dtype)


def _norm_proj_kernel(x_ref, g_ref, w_ref, cos_ref, sin_ref, h_ref, o_ref, *, sets):
    x = x_ref[...]
    h = ((x * lax.rsqrt(jnp.mean(x * x, axis=-1, keepdims=True) + NORM_EPS)) * g_ref[...]).astype(h_ref.dtype)
    h_ref[...] = h
    heads_per_tile = TN // HEAD_DIM
    for t in range(o_ref.shape[0]):
        acc = jnp.dot(h, w_ref[:, t * TN:(t + 1) * TN], preferred_element_type=jnp.float32)
        for hh in range(heads_per_tile):
            sl = slice(hh * HEAD_DIM, (hh + 1) * HEAD_DIM)
            tab = sets[hh * 2 // heads_per_tile][t]
            xh = acc[:, sl]
            o_ref[t, :, sl] = (xh * cos_ref[tab] + pltpu.roll(xh, HEAD_DIM // 2, 1) * sin_ref[tab]).astype(o_ref.dtype)


def _norm_proj(x, g, w, tabs, sets, *, batch, tm=256):
    m, d = x.shape
    seq = m // batch
    bps = seq // tm
    nt = w.shape[1] // TN
    n_sets = tabs[0].shape[0]
    tab_spec = pl.BlockSpec((n_sets, tm, HEAD_DIM), lambda i: (0, i % bps, 0))
    return pl.pallas_call(
        functools.partial(_norm_proj_kernel, sets=sets),
        grid=(m // tm,),
        in_specs=[pl.BlockSpec((tm, d), lambda i: (i, 0)),
                  pl.BlockSpec((1, d), lambda i: (0, 0)),
                  pl.BlockSpec(w.shape, lambda i: (0, 0), pipeline_mode=pl.Buffered(1)),
                  tab_spec, tab_spec],
        out_specs=[pl.BlockSpec((tm, d), lambda i: (i, 0)),
                   pl.BlockSpec((nt, tm, TN), lambda i: (0, i, 0))],
        out_shape=[jax.ShapeDtypeStruct((m, d), jnp.bfloat16),
                   jax.ShapeDtypeStruct((nt, m, TN), jnp.bfloat16)],
        compiler_params=_params(("arbitrary",), vmem=OUT_VMEM),
        name="norm_proj_b",
    )(x, g.reshape(1, d), w, *tabs)


def _cast_kernel(src_ref, dst_ref):
    dst_ref[...] = src_ref[...].astype(dst_ref.dtype)


def _cast_tiles(src, tiles):
    lo, hi = tiles
    rows = src.shape[0]
    cr = min(CAST_ROWS, rows)
    return pl.pallas_call(
        _cast_kernel,
        grid=(hi - lo, rows // cr),
        in_specs=[pl.BlockSpec((cr, TN), lambda j, i: (i, lo + j))],
        out_specs=pl.BlockSpec((cr, TN), lambda j, i: (i, j)),
        out_shape=jax.ShapeDtypeStruct((rows, (hi - lo) * TN), jnp.bfloat16),
        compiler_params=_params(("parallel", "parallel")),
        name="cast_tiles",
    )(src)


def _select(j, values):
    if len(set(values)) == 1:
        return values[0]
    out = values[-1]
    for t in range(len(values) - 2, -1, -1):
        out = jnp.where(j == t, values[t], out)
    return out


CAST_ROWS = 512


def _proj(h, w, tiles, epi, *, batch, tm, dil=1, tabs=None, sets=None, cast=None, guest=None, name):
    m, k = h.shape
    seq = m // batch
    col_tiles = range(*tiles)
    lo, step, nt = col_tiles.start, col_tiles.step, len(col_tiles)
    assert seq % tm == 0 and tm % dil == 0 and (dil == 1 or epi == "rot")
    bps = seq // tm
    in_specs = [pl.BlockSpec((tm, k), lambda i, j: (i, 0)),
                pl.BlockSpec((k, TN), lambda i, j: (0, lo + step * j))]
    args = [h, w]
    if epi == "rot":
        for half in sets:
            assert len(half) == nt
            spec = pl.BlockSpec((None, tm, HEAD_DIM),
                                lambda i, j, half=half: (_select(j, half), i % bps, 0))
            in_specs += [spec, spec]
            args += list(tabs)
    out_specs = [pl.BlockSpec((None, None, dil, tm // dil, TN), lambda i, j: (j, i // bps, 0, i % bps, 0))]
    out_shape = [jax.ShapeDtypeStruct((nt, batch, dil, seq // dil, TN), jnp.bfloat16)]
    if cast is not None:
        src, src_tiles = cast
        src_tiles = range(*src_tiles)
        cr = min(CAST_ROWS, src.shape[0])
        cw = min(TN, src.shape[1])
        row_blocks = src.shape[0] // cr
        n_chunks = row_blocks * len(src_tiles)
        assert n_chunks <= (m // tm) * nt, "not enough grid steps to cast this weight group"
        chunk = lambda i, j: jnp.minimum(i * nt + j, n_chunks - 1)
        in_specs.append(pl.BlockSpec(
            (cr, cw), lambda i, j: (chunk(i, j) % row_blocks,
                                    src_tiles.start + src_tiles.step * (chunk(i, j) // row_blocks))))
        args.append(src)
        out_specs.append(pl.BlockSpec(
            (cr, cw), lambda i, j: (chunk(i, j) % row_blocks, chunk(i, j) // row_blocks)))
        out_shape.append(jax.ShapeDtypeStruct((src.shape[0], len(src_tiles) * cw), jnp.bfloat16))
    host = dict(
        stages=functools.partial(_proj_stages, epi=epi, dil=dil, cast=cast is not None),
        grid=(m // tm, nt), in_specs=in_specs, args=args, out_specs=out_specs, out_shape=out_shape,
        name=name)
    if guest is not None:
        outs, guest_outs = _launch(host, guest)
        return (outs if cast is not None else outs[0]), guest_outs
    outs = _launch(host)
    return outs if cast is not None else outs[0]


A_TQ = 128
A_STEP = 512
A_HALO = 64
LSE_LANES = 128


def _window_rows(prev_ref, cur_ref, next_ref, lo, hi, cols):
    n = cur_ref.shape[0]
    parts = []
    if lo < 0:
        parts.append(prev_ref[:, cols])
    parts.append(cur_ref[max(lo, 0):min(hi, n), cols])
    if hi > n:
        parts.append(next_ref[:, cols])
    return jnp.concatenate(parts, axis=0) if len(parts) > 1 else parts[0]


def _band_bias(tq, halo, radius, start, seq):
    nk = tq + 2 * halo
    r = lax.broadcasted_iota(jnp.int32, (tq, nk), 0)
    c = lax.broadcasted_iota(jnp.int32, (tq, nk), 1)
    kpos = start - halo + c
    valid = (jnp.abs(c - halo - r) <= radius) & (kpos >= 0) & (kpos < seq)
    return jnp.where(valid, 0.0, NEG_INF)


def _attn_a_kernel(q_ref, kp_ref, kc_ref, kn_ref, vp_ref, vc_ref, vn_ref, o_ref, lse_ref, *, seq):
    step = q_ref.shape[0]
    lane = lax.broadcasted_iota(jnp.int32, (A_TQ, LSE_LANES), 1)
    for b in range(step // A_TQ):
        rows = slice(b * A_TQ, (b + 1) * A_TQ)
        lo, hi = b * A_TQ - A_HALO, (b + 1) * A_TQ + A_HALO
        bias = _band_bias(A_TQ, A_HALO, A_RADIUS, pl.program_id(1) * step + b * A_TQ, seq)
        lse_all = jnp.zeros((A_TQ, LSE_LANES), jnp.float32)
        for hh in range(A_HEADS):
            sl = slice(hh * HEAD_DIM, (hh + 1) * HEAD_DIM)
            k = _window_rows(kp_ref, kc_ref, kn_ref, lo, hi, sl)
            v = _window_rows(vp_ref, vc_ref, vn_ref, lo, hi, sl)
            s = lax.dot_general(q_ref[rows, sl], k, (((1,), (1,)), ((), ())),
                                preferred_element_type=jnp.float32) + bias
            m = jnp.max(s, axis=1, keepdims=True)
            e = jnp.exp2(s - m)
            den = jnp.sum(e, axis=1, keepdims=True)
            o = jnp.dot(e.astype(jnp.bfloat16), v, preferred_element_type=jnp.float32)
            o_ref[rows, sl] = (o * (1.0 / den)).astype(o_ref.dtype)
            lse_all = jnp.where(lane == hh, m * LN2 + jnp.log(den), lse_all)
        lse_ref[rows, :] = lse_all


def _attn_a(qkv, n, seq, offs):
    qo, ko, vo = offs
    step = min(A_STEP, seq)
    nh = seq // A_HALO
    ratio = step // A_HALO
    cur = lambda off: pl.BlockSpec((None, step, A_WIDTH), lambda s, i: (off + s, i, 0))
    prev = lambda off: pl.BlockSpec(
        (None, A_HALO, A_WIDTH), lambda s, i: (off + s, jnp.maximum(ratio * i - 1, 0), 0))
    nxt = lambda off: pl.BlockSpec(
        (None, A_HALO, A_WIDTH), lambda s, i: (off + s, jnp.minimum(ratio * (i + 1), nh - 1), 0))
    return pl.pallas_call(
        functools.partial(_attn_a_kernel, seq=seq),
        grid=(n, seq // step),
        in_specs=[cur(qo), prev(ko), cur(ko), nxt(ko), prev(vo), cur(vo), nxt(vo)],
        out_specs=[pl.BlockSpec((None, step, A_WIDTH), lambda s, i: (s, i, 0)),
                   pl.BlockSpec((None, step, LSE_LANES), lambda s, i: (s, i, 0))],
        out_shape=[jax.ShapeDtypeStruct((n, seq, A_WIDTH), jnp.bfloat16),
                   jax.ShapeDtypeStruct((n, seq, LSE_LANES), jnp.float32)],
        compiler_params=_params(("parallel", "parallel")),
        name="attn_a",
    )(qkv, qkv, qkv, qkv, qkv, qkv, qkv)


def _merge_a_kernel(c0_ref, c1_ref, c2_ref, cl0_ref, cl1_ref, cl2_ref, z_ref, u_ref,
                    o1_ref, o2_ref, l1_ref, l2_ref):
    tm = u_ref.shape[0]
    o0_ref, l0_ref = c0_ref.at[0], cl0_ref.at[0]
    for c_ref, cl_ref, o_ref, l_ref in ((c1_ref, cl1_ref, o1_ref, l1_ref), (c2_ref, cl2_ref, o2_ref, l2_ref)):
        dil = c_ref.shape[0]
        for r in range(dil):
            rows = pl.ds(r, tm // dil, stride=dil)
            for hh in range(A_HEADS):
                o_ref[hh, rows, :] = c_ref[r, :, hh * HEAD_DIM:(hh + 1) * HEAD_DIM].astype(jnp.float32)
            l_ref[rows, :] = cl_ref[r]
    l0, l1, l2 = l0_ref[...], l1_ref[...], l2_ref[...]
    m = jnp.maximum(jnp.maximum(l0, l1), l2)
    e0, e1, e2 = jnp.exp(l0 - m), jnp.exp(l1 - m), jnp.exp(l2 - m)
    inv = 1.0 / (e0 + e1 + e2)
    a0, a1, a2 = e0 * inv, e1 * inv, e2 * inv
    for hh in range(A_HEADS):
        sl = slice(hh * HEAD_DIM, (hh + 1) * HEAD_DIM)
        col = slice(hh, hh + 1)
        o = a0[:, col] * o0_ref[:, sl] + a1[:, col] * o1_ref[hh] + a2[:, col] * o2_ref[hh]
        u_ref[:, sl] = (o * z_ref[:, sl].astype(jnp.float32)).astype(u_ref.dtype)


def _merge_a(outs, lses, pz, batch, seq, tm=512):
    bps = seq // tm
    cls = lambda dil, width: pl.BlockSpec((None, dil, tm // dil, width), lambda i: (i // bps, 0, i % bps, 0))
    dils = [d for _, d in DIL_CONFIGS]
    return pl.pallas_call(
        _merge_a_kernel,
        grid=(batch * bps,),
        in_specs=([cls(d, A_WIDTH) for d in dils] + [cls(d, LSE_LANES) for d in dils]
                  + [pl.BlockSpec((None, tm, TN), lambda i: (0, i, 0))]),
        out_specs=pl.BlockSpec((tm, A_WIDTH), lambda i: (i, 0)),
        out_shape=jax.ShapeDtypeStruct((batch * seq, A_WIDTH), jnp.bfloat16),
        scratch_shapes=[pltpu.VMEM((A_HEADS, tm, HEAD_DIM), jnp.float32)] * 2
                       + [pltpu.VMEM((tm, LSE_LANES), jnp.float32)] * 2,
        compiler_params=_params(("parallel",)),
        name="merge_a",
    )(*outs, *lses, pz)


B_TQ = 128


def _attn_b_stages(ids, ins, outs, scratch, *, seq):
    sink_ref, q0_ref, q1_ref, kvp_ref, kvc_ref, kvn_ref, z0_ref, z1_ref = ins
    u_ref, = outs
    step = u_ref.shape[0]
    ones = jnp.ones((3 * B_TQ, HEAD_DIM), jnp.bfloat16)
    kv_half = B_KV_HEADS * HEAD_DIM
    units = [(b, kh) for b in range(step // B_TQ) for kh in range(B_KV_HEADS)]

    def scores(b, kh):
        rows = slice(b * B_TQ, (b + 1) * B_TQ)
        lo, hi = (b - 1) * B_TQ, (b + 2) * B_TQ
        q_ref = q0_ref if kh < 2 else q1_ref
        base = (kh % 2) * B_GROUP * HEAD_DIM
        heads = [slice(base + g * HEAD_DIM, base + (g + 1) * HEAD_DIM) for g in range(B_GROUP)]
        k = _window_rows(kvp_ref, kvc_ref, kvn_ref, lo, hi, slice(kh * HEAD_DIM, (kh + 1) * HEAD_DIM))
        q = jnp.concatenate([q_ref[rows, sl] for sl in heads], axis=0)
        return lax.dot_general(q, k, (((1,), (1,)), ((), ())), preferred_element_type=jnp.float32)

    def finish(b, kh, s):
        rows = slice(b * B_TQ, (b + 1) * B_TQ)
        lo, hi = (b - 1) * B_TQ, (b + 2) * B_TQ
        z_ref = z0_ref if kh < 2 else z1_ref
        base = (kh % 2) * B_GROUP * HEAD_DIM
        heads = [slice(base + g * HEAD_DIM, base + (g + 1) * HEAD_DIM) for g in range(B_GROUP)]
        bias = _band_bias(B_TQ, B_TQ, B_RADIUS, ids[1] * step + b * B_TQ, seq)
        v = _window_rows(kvp_ref, kvc_ref, kvn_ref, lo, hi,
                         slice(kv_half + kh * HEAD_DIM, kv_half + (kh + 1) * HEAD_DIM))
        es, ms, sks = [], [], []
        for g in range(B_GROUP):
            sg = s[g * B_TQ:(g + 1) * B_TQ] + bias
            sk = sink_ref[kh * B_GROUP + g] * LOG2E
            m = jnp.maximum(jnp.max(sg, axis=1, keepdims=True), sk)
            es.append(jnp.exp2(sg - m).astype(jnp.bfloat16))
            ms.append(m)
            sks.append(sk)
        ov = jnp.dot(jnp.concatenate(es, axis=0), jnp.concatenate([v, ones], axis=1),
                     preferred_element_type=jnp.float32)
        for g in range(B_GROUP):
            og = ov[g * B_TQ:(g + 1) * B_TQ]
            den = og[:, HEAD_DIM:HEAD_DIM + 1] + jnp.exp2(sks[g] - ms[g])
            col = (kh * B_GROUP + g) * HEAD_DIM
            u_ref[rows, col:col + HEAD_DIM] = (
                og[:, :HEAD_DIM] * (1.0 / den) * z_ref[rows, heads[g]].astype(jnp.float32)
            ).astype(u_ref.dtype)

    per_round = 4
    pending = [scores(*u) for u in units[:per_round]]
    yield
    for r in range(0, len(units), per_round):
        for u, sc in zip(units[r:r + per_round], pending):
            finish(*u, sc)
        pending = [scores(*u) for u in units[r + per_round:r + 2 * per_round]]
        yield


def _attn_b(sink, pb, pz, batch, seq, step):
    nq = seq // step
    nh = seq // B_TQ
    ratio = step // B_TQ
    row = lambda slot: pl.BlockSpec((None, step, TN), lambda b, i: (slot, b * nq + i, 0))
    kv_prev = pl.BlockSpec((None, B_TQ, TN), lambda b, i: (2, b * nh + jnp.maximum(ratio * i - 1, 0), 0))
    kv_next = pl.BlockSpec((None, B_TQ, TN),
                           lambda b, i: (2, b * nh + jnp.minimum(ratio * (i + 1), nh - 1), 0))
    width = B_Q_HEADS * HEAD_DIM
    return dict(
        stages=functools.partial(_attn_b_stages, seq=seq),
        grid=(batch, nq),
        in_specs=[pl.BlockSpec(memory_space=pltpu.SMEM),
                  row(0), row(1), kv_prev, row(2), kv_next, row(1), row(2)],
        args=[sink, pb, pb, pb, pb, pb, pz, pz],
        out_specs=[pl.BlockSpec((step, width), lambda b, i: (b * nq + i, 0))],
        out_shape=[jax.ShapeDtypeStruct((batch * seq, width), jnp.bfloat16)],
        name="attn_b")


M_TQ = 512


def _attn_m_kernel(q_ref, k_ref, v_ref, z_ref, u_ref):
    for hh in range(M_HEADS):
        sl = slice(hh * M_HEAD_DIM, (hh + 1) * M_HEAD_DIM)
        s = lax.dot_general(q_ref[:, sl], k_ref[:, sl], (((1,), (1,)), ((), ())),
                            preferred_element_type=jnp.float32)
        m = jnp.max(s, axis=1, keepdims=True)
        e = jnp.exp2(s - m)
        den = jnp.sum(e, axis=1, keepdims=True)
        o = jnp.dot(e.astype(jnp.bfloat16), v_ref[:, sl], preferred_element_type=jnp.float32) * (1.0 / den)
        u_ref[:, sl] = (o * z_ref[:, sl].astype(jnp.float32)).astype(u_ref.dtype)


def _attn_m(pb, kv_mem, pz, batch, seq):
    nq = seq // M_TQ
    width = M_HEADS * M_HEAD_DIM
    return pl.pallas_call(
        _attn_m_kernel,
        grid=(batch, nq),
        in_specs=[pl.BlockSpec((None, M_TQ, TN), lambda b, i: (3, b * nq + i, 0)),
                  pl.BlockSpec((None, MEM_LEN, TN), lambda b, i: (0, b, 0)),
                  pl.BlockSpec((None, MEM_LEN, TN), lambda b, i: (1, b, 0)),
                  pl.BlockSpec((None, M_TQ, TN), lambda b, i: (3, b * nq + i, 0))],
        out_specs=pl.BlockSpec((M_TQ, width), lambda b, i: (b * nq + i, 0)),
        out_shape=jax.ShapeDtypeStruct((batch * seq, width), jnp.bfloat16),
        compiler_params=_params(("parallel", "parallel")),
        name="attn_m",
    )(pb, kv_mem, kv_mem, pz)


def _branch_kernel(ua_ref, ub_ref, um_ref, w_ref, ga_ref, gb_ref, gm_ref, u_ref):
    a_hi = A_WIDTH
    b_hi = A_WIDTH + B_Q_HEADS * HEAD_DIM
    dot = functools.partial(jnp.dot, preferred_element_type=jnp.float32)
    acc = ga_ref[...].astype(jnp.float32) * dot(ua_ref[...], w_ref[:a_hi, :])
    acc += gb_ref[...].astype(jnp.float32) * dot(ub_ref[...], w_ref[a_hi:b_hi, :])
    acc += gm_ref[...].astype(jnp.float32) * dot(um_ref[...], w_ref[b_hi:, :])
    u_ref[...] = acc.astype(u_ref.dtype)


def _branch(ua, ub, um, w_branch, gates, tm=512):
    m = ua.shape[0]
    k = w_branch.shape[0]
    n_tiles = D_MODEL // TN
    act = lambda width: pl.BlockSpec((tm, width), lambda j, i: (i, 0))
    gate = pl.BlockSpec((None, tm, TN), lambda j, i: (j, i, 0))
    return pl.pallas_call(
        _branch_kernel,
        grid=(n_tiles, m // tm),
        in_specs=[act(ua.shape[1]), act(ub.shape[1]), act(um.shape[1]),
                  pl.BlockSpec((k, TN), lambda j, i: (0, j)),
                  gate, gate, gate],
        out_specs=pl.BlockSpec((tm, TN), lambda j, i: (i, j)),
        out_shape=jax.ShapeDtypeStruct((m, D_MODEL), jnp.bfloat16),
        compiler_params=_params(("parallel", "parallel")),
        name="branch_proj",
    )(ua, ub, um, w_branch, *gates)


def _out_kernel(u_ref, w_ref, x_ref, g_ref, y_ref):
    y = x_ref[...] + jnp.dot(u_ref[...], w_ref[...], preferred_element_type=jnp.float32)
    inv = lax.rsqrt(jnp.mean(y * y, axis=1, keepdims=True) + NORM_EPS)
    y_ref[...] = (y * inv) * g_ref[...]


def _out_proj(u, w_out, x, g_final, tm=256):
    m, k = u.shape
    n = x.shape[1]
    return pl.pallas_call(
        _out_kernel,
        grid=(m // tm,),
        in_specs=[pl.BlockSpec((tm, k), lambda i: (i, 0)),
                  pl.BlockSpec((k, n), lambda i: (0, 0), pipeline_mode=pl.Buffered(1)),
                  pl.BlockSpec((tm, n), lambda i: (i, 0)),
                  pl.BlockSpec((1, n), lambda i: (0, 0))],
        out_specs=pl.BlockSpec((tm, n), lambda i: (i, 0)),
        out_shape=jax.ShapeDtypeStruct((m, n), jnp.float32),
        compiler_params=_params(("arbitrary",), vmem=OUT_VMEM),
        name="out_proj",
    )(u, w_out, x, g_final.reshape(1, n))


def _rope_tables(seq):
    inv_freq = ROPE_THETA ** (-jnp.arange(0, HEAD_DIM, 2, dtype=jnp.float32) / HEAD_DIM)
    ang = jnp.arange(seq, dtype=jnp.float32)[:, None] * inv_freq[None, :]
    cos, sin = jnp.cos(ang), jnp.sin(ang)
    cos, sin = jnp.concatenate([cos, cos], axis=1), jnp.concatenate([-sin, sin], axis=1)
    one, zero = jnp.ones_like(cos), jnp.zeros_like(sin)
    q_scale = HEAD_DIM ** -0.5 * LOG2E
    m_scale = M_HEAD_DIM ** -0.5 * LOG2E
    return (jnp.stack([cos * q_scale, cos, one, one * m_scale]),
            jnp.stack([sin * q_scale, sin, zero, zero]))


def _mix(h, x2, hm, pb, pz, ub, gates, w, tabs, g_final, batch, seq, cast_mem=None):
    outs, lses = [], []
    for gi, (_, dil) in enumerate(DIL_CONFIGS):
        sd = seq // dil
        n = batch * dil
        a_sets = ((ROT_Q, ROT_K, ROT_ID),) * 2
        qkv = _proj(h, w["a"][gi], (0, 3), "rot", batch=batch, tm=1024, dil=dil, tabs=tabs, sets=a_sets,
                    cast=cast_mem if gi == 0 else None, name=f"proj_a{gi}")
        if gi == 0 and cast_mem is not None:
            qkv, w_mem = qkv
            w = dict(w, mem=w_mem)
        o, lse = _attn_a(qkv.reshape(3 * n, sd, A_WIDTH), n, sd, (0, n, 2 * n))
        outs.append(o.reshape(batch, dil, sd, A_WIDTH))
        lses.append(lse.reshape(batch, dil, sd, LSE_LANES))
    ua = _merge_a(outs, lses, pz, batch, seq)

    kv_mem = _proj(hm, w["mem"], (0, 2), "none", batch=1, tm=batch * MEM_LEN, name="proj_mem")
    um = _attn_m(pb, kv_mem.reshape(2, batch * MEM_LEN, TN), pz, batch, seq)

    u = _branch(ua, ub, um, w["branch"], gates)
    return _out_proj(u, w["out"], x2, g_final).reshape(batch, seq, x2.shape[1]), w["mem"]


def kernel(x_prompt, x_sample, mem_prompt, mem_sample, g_norm, w_in, attn_sink, g_mem, w_mem_kv, w_branch, w_out, g_final):
    bf = jnp.bfloat16
    w_in, w_mem_kv, w_branch, w_out = w_in[0], w_mem_kv[0], w_branch[0], w_out[0]
    trunks = []
    tabs = _rope_tables(max(x_prompt.shape[1], x_sample.shape[1]))
    for x, mem in ((x_prompt, mem_prompt), (x_sample, mem_sample)):
        batch, seq, d = x.shape
        x2 = x.reshape(batch * seq, d)
        trunks.append(dict(
            batch=batch, seq=seq, x2=x2, tabs=tabs,
            hm=_rmsnorm(mem.reshape(batch * MEM_LEN, d), g_mem[0], bf)))

    w_b = _cast_tiles(w_in, B_TILES)
    w_z = _cast_tiles(w_in, Z_TILES)
    b_sets = ((ROT_Q, ROT_Q, ROT_K, ROT_M), (ROT_Q, ROT_Q, ROT_ID, ROT_M))
    g_lo = G_TILES[0]
    n_dil = len(DIL_CONFIGS)
    out_tiles = -(-w_out.shape[1] // TN)

    def proj(t, wts, epi, cast, name, **kw):
        out, w_next = _proj(t["h"], wts, (0, 4), epi, batch=t["batch"], tm=1024, cast=cast, name=name, **kw)
        return out.reshape(4, t["batch"] * t["seq"], TN), w_next

    p, s_ = trunks
    for t in trunks:
        t["h"], t["pb"] = _norm_proj(t["x2"], g_norm[0], w_b, tabs, b_sets, batch=t["batch"])
    pb_p, pb_s = p["pb"], s_["pb"]
    pz_p, w_ga = proj(p, w_z, "silu", (w_in, (g_lo, g_lo + 4)), "proj_z")
    pz_s, w_gb = proj(s_, w_z, "silu", (w_in, (g_lo + 4, g_lo + 8)), "proj_z")
    ga_p, w_gm = proj(p, w_ga, "sigmoid", (w_in, (g_lo + 8, g_lo + 12)), "proj_ga")
    ga_s, w_a0 = proj(s_, w_ga, "sigmoid", (w_in, (0, A_TILES[1], n_dil)), "proj_ga")
    gb_p, w_a1 = proj(p, w_gb, "sigmoid", (w_in, (1, A_TILES[1], n_dil)), "proj_gb")
    gb_s, w_a2 = proj(s_, w_gb, "sigmoid", (w_in, (2, A_TILES[1], n_dil)), "proj_gb")

    def proj_with_attn_b(t, wts, cast, pb, pz):
        rows = t["batch"] * t["seq"]
        guest = _attn_b(attn_sink[0], pb, pz, t["batch"], t["seq"], step=rows // (rows // 1024 * 4))
        (out, w_next), (ub,) = _proj(t["h"], wts, (0, 4), "sigmoid_parts", batch=t["batch"], tm=1024,
                                     cast=cast, guest=guest, name="proj_gm")
        return out.reshape(4, rows, TN), w_next, ub

    gm_p, w_o, ub_p = proj_with_attn_b(p, w_gm, (w_out, (0, out_tiles)), pb_p, pz_p)
    gm_s, w_br, ub_s = proj_with_attn_b(s_, w_gm, (w_branch, (0, D_MODEL // TN)), pb_s, pz_s)

    w = dict(a=(w_a0, w_a1, w_a2), branch=w_br, out=w_o)
    y_p, w_m = _mix(p["h"], p["x2"], p["hm"], pb_p, pz_p, ub_p, (ga_p, gb_p, gm_p), w, tabs, g_final,
                    p["batch"], p["seq"], cast_mem=(w_mem_kv, (0, 2)))
    y_s, _ = _mix(s_["h"], s_["x2"], s_["hm"], pb_s, pz_s, ub_s, (ga_s, gb_s, gm_s), dict(w, mem=w_m), tabs,
                  g_final, s_["batch"], s_["seq"])
    return y_p, y_s
```

```python
import functools

import jax
import jax.numpy as jnp
from jax import lax
from jax.experimental import pallas as pl
from jax.experimental.pallas import tpu as pltpu

D_MODEL = 4096
HEAD_DIM = 128
ROPE_THETA = 10000.0
NORM_EPS = 1e-6
NEG_INF = -1e30
LOG2E = 1.4426950408889634
LN2 = 0.6931471805599453

DIL_CONFIGS = ((128, 1), (512, 4), (2048, 16))
A_HEADS = 8
A_WIDTH = A_HEADS * HEAD_DIM
A_RADIUS = 64

B_Q_HEADS = 16
B_KV_HEADS = 4
B_GROUP = B_Q_HEADS // B_KV_HEADS
B_RADIUS = 128

MEM_LEN = 256
M_HEADS = 4
M_HEAD_DIM = 256

TN = 1024
A_TILES = (0, 9)
B_TILES = (9, 13)
Z_TILES = (13, 17)
G_TILES = (17, 29)
ROT_Q, ROT_K, ROT_ID, ROT_M = range(4)

VMEM_LIMIT = 56 * 1024 * 1024
OUT_VMEM = 60 * 1024 * 1024


def _params(sem, vmem=VMEM_LIMIT):
    return pltpu.CompilerParams(dimension_semantics=sem, vmem_limit_bytes=vmem)


def _launch(host, guest=None):
    parts = [host] if guest is None else [host, guest]
    grid = host["grid"]
    if guest is not None:
        g0, g1 = guest["grid"]
        assert g0 * g1 == grid[0] * grid[1], (guest["grid"], grid)
        to_guest = lambda i, j: divmod(i * grid[1] + j, g1)
        remap = lambda spec: pl.BlockSpec(
            spec.block_shape, (lambda i, j, f=spec.index_map: f(*to_guest(i, j))),
            memory_space=spec.memory_space) if spec.index_map is not None else spec
        guest = dict(guest, in_specs=[remap(sp) for sp in guest["in_specs"]],
                     out_specs=[remap(sp) for sp in guest["out_specs"]])
        parts = [host, guest]
    n_in = [len(p["in_specs"]) for p in parts]
    n_out = [len(p["out_specs"]) for p in parts]
    n_scr = [len(p.get("scratch", [])) for p in parts]

    def body(*refs):
        refs = list(refs)
        take = lambda n: [refs.pop(0) for _ in range(n)]
        ins = [take(n) for n in n_in]
        outs = [take(n) for n in n_out]
        scr = [take(n) for n in n_scr]
        ids = (pl.program_id(0), pl.program_id(1)) if len(grid) == 2 else (pl.program_id(0),)
        gens = [host["stages"](ids, ins[0], outs[0], scr[0])]
        if guest is not None:
            gens.insert(0, guest["stages"](to_guest(*ids), ins[1], outs[1], scr[1]))
        while gens:
            gens = [g for g in gens if next(g, "done") != "done"]

    res = pl.pallas_call(
        body,
        grid=grid,
        in_specs=[sp for p in parts for sp in p["in_specs"]],
        out_specs=[sp for p in parts for sp in p["out_specs"]],
        out_shape=[sh for p in parts for sh in p["out_shape"]],
        scratch_shapes=[sc for p in parts for sc in p.get("scratch", [])],
        compiler_params=_params(("arbitrary",) * len(grid)),
        name=host["name"] if guest is None else host["name"] + "_" + guest["name"],
    )(*[a for p in parts for a in p["args"]])
    if guest is None:
        return res
    return res[:n_out[0]], res[n_out[0]:]


def _rmsnorm_kernel(x_ref, g_ref, o_ref):
    x = x_ref[...]
    ms = jnp.mean(x * x, axis=-1, keepdims=True)
    o_ref[...] = ((x * lax.rsqrt(ms + NORM_EPS)) * g_ref[...]).astype(o_ref.dtype)


def _rmsnorm(x, g, out_dtype, tm=256):
    m, d = x.shape
    return pl.pallas_call(
        _rmsnorm_kernel,
        grid=(m // tm,),
        in_specs=[pl.BlockSpec((tm, d), lambda i: (i, 0)),
                  pl.BlockSpec((1, d), lambda i: (0, 0))],
        out_specs=pl.BlockSpec((tm, d), lambda i: (i, 0)),
        out_shape=jax.ShapeDtypeStruct((m, d), out_dtype),
        compiler_params=_params(("parallel",)),
        name="rmsnorm",
    )(x, g.reshape(1, d))


def _sigmoid(x):
    return 0.5 * jnp.tanh(0.5 * x) + 0.5


def _proj_stages(ids, ins, outs, scratch, *, epi, dil, cast):
    del ids
    h_ref, w_ref = ins[:2]
    o_ref = outs[0]
    tm = h_ref.shape[0]
    if cast:
        outs[1][...] = ins[-1][...].astype(outs[1].dtype)
    if epi == "sigmoid_parts":
        nq = 4
        wq = TN // nq
        for q in range(nq):
            cols = slice(q * wq, (q + 1) * wq)
            acc = jnp.dot(h_ref[...], w_ref[:, cols], preferred_element_type=jnp.float32)
            o_ref[0, :, cols] = _sigmoid(acc).astype(o_ref.dtype)
            yield
        return
    yield
    acc = jnp.dot(h_ref[...], w_ref[...], preferred_element_type=jnp.float32)
    if epi == "none":
        o_ref[0] = acc.astype(o_ref.dtype)
    elif epi == "silu":
        o_ref[0] = (acc * _sigmoid(acc)).astype(o_ref.dtype)
    elif epi == "sigmoid":
        o_ref[0] = _sigmoid(acc).astype(o_ref.dtype)
    else:
        cos_lo, sin_lo, cos_hi, sin_hi = ins[2:6]
        heads = [slice(hh * HEAD_DIM, (hh + 1) * HEAD_DIM) for hh in range(TN // HEAD_DIM)]
        for hh, sl in enumerate(heads):
            cos_ref, sin_ref = (cos_lo, sin_lo) if hh < len(heads) // 2 else (cos_hi, sin_hi)
            x = acc[:, sl]
            x = x * cos_ref[...] + pltpu.roll(x, HEAD_DIM // 2, 1) * sin_ref[...]
            if dil > 1:
                o_ref[:, :, sl] = jnp.swapaxes(x.reshape(tm // dil, dil, HEAD_DIM), 0, 1).astype(o_ref.dtype)
            else:
                o_ref[0, :, sl] = x.astype(o_ref.dtype)


def _norm_proj_kernel(x_ref, g_ref, w_ref, cos_ref, sin_ref, h_ref, o_ref, *, sets):
    x = x_ref[...]
    h = ((x * lax.rsqrt(jnp.mean(x * x, axis=-1, keepdims=True) + NORM_EPS)) * g_ref[...]).astype(h_ref.dtype)
    h_ref[...] = h
    heads_per_tile = TN // HEAD_DIM
    for t in range(o_ref.shape[0]):
        acc = jnp.dot(h, w_ref[:, t * TN:(t + 1) * TN], preferred_element_type=jnp.float32)
        for hh in range(heads_per_tile):
            sl = slice(hh * HEAD_DIM, (hh + 1) * HEAD_DIM)
            tab = sets[hh * 2 // heads_per_tile][t]
            xh = acc[:, sl]
            o_ref[t, :, sl] = (xh * cos_ref[tab] + pltpu.roll(xh, HEAD_DIM // 2, 1) * sin_ref[tab]).astype(o_ref.dtype)


def _norm_proj(x, g, w, tabs, sets, *, batch, tm=256):
    m, d = x.shape
    seq = m // batch
    bps = seq // tm
    nt = w.shape[1] // TN
    n_sets = tabs[0].shape[0]
    tab_spec = pl.BlockSpec((n_sets, tm, HEAD_DIM), lambda i: (0, i % bps, 0))
    return pl.pallas_call(
        functools.partial(_norm_proj_kernel, sets=sets),
        grid=(m // tm,),
        in_specs=[pl.BlockSpec((tm, d), lambda i: (i, 0)),
                  pl.BlockSpec((1, d), lambda i: (0, 0)),
                  pl.BlockSpec(w.shape, lambda i: (0, 0), pipeline_mode=pl.Buffered(1)),
                  tab_spec, tab_spec],
        out_specs=[pl.BlockSpec((tm, d), lambda i: (i, 0)),
                   pl.BlockSpec((nt, tm, TN), lambda i: (0, i, 0))],
        out_shape=[jax.ShapeDtypeStruct((m, d), jnp.bfloat16),
                   jax.ShapeDtypeStruct((nt, m, TN), jnp.bfloat16)],
        compiler_params=_params(("arbitrary",), vmem=OUT_VMEM),
        name="norm_proj_b",
    )(x, g.reshape(1, d), w, *tabs)


def _cast_kernel(src_ref, dst_ref):
    dst_ref[...] = src_ref[...].astype(dst_ref.dtype)


def _cast_tiles(src, tiles):
    lo, hi = tiles
    rows = src.shape[0]
    cr = min(CAST_ROWS, rows)
    return pl.pallas_call(
        _cast_kernel,
        grid=(hi - lo, rows // cr),
        in_specs=[pl.BlockSpec((cr, TN), lambda j, i: (i, lo + j))],
        out_specs=pl.BlockSpec((cr, TN), lambda j, i: (i, j)),
        out_shape=jax.ShapeDtypeStruct((rows, (hi - lo) * TN), jnp.bfloat16),
        compiler_params=_params(("parallel", "parallel")),
        name="cast_tiles",
    )(src)


def _select(j, values):
    if len(set(values)) == 1:
        return values[0]
    out = values[-1]
    for t in range(len(values) - 2, -1, -1):
        out = jnp.where(j == t, values[t], out)
    return out


CAST_ROWS = 512


def _proj(h, w, tiles, epi, *, batch, tm, dil=1, tabs=None, sets=None, cast=None, guest=None, name):
    m, k = h.shape
    seq = m // batch
    col_tiles = range(*tiles)
    lo, step, nt = col_tiles.start, col_tiles.step, len(col_tiles)
    assert seq % tm == 0 and tm % dil == 0 and (dil == 1 or epi == "rot")
    bps = seq // tm
    in_specs = [pl.BlockSpec((tm, k), lambda i, j: (i, 0)),
                pl.BlockSpec((k, TN), lambda i, j: (0, lo + step * j))]
    args = [h, w]
    if epi == "rot":
        for half in sets:
            assert len(half) == nt
            spec = pl.BlockSpec((None, tm, HEAD_DIM),
                                lambda i, j, half=half: (_select(j, half), i % bps, 0))
            in_specs += [spec, spec]
            args += list(tabs)
    out_specs = [pl.BlockSpec((None, None, dil, tm // dil, TN), lambda i, j: (j, i // bps, 0, i % bps, 0))]
    out_shape = [jax.ShapeDtypeStruct((nt, batch, dil, seq // dil, TN), jnp.bfloat16)]
    if cast is not None:
        src, src_tiles = cast
        src_tiles = range(*src_tiles)
        cr = min(CAST_ROWS, src.shape[0])
        cw = min(TN, src.shape[1])
        row_blocks = src.shape[0] // cr
        n_chunks = row_blocks * len(src_tiles)
        assert n_chunks <= (m // tm) * nt, "not enough grid steps to cast this weight group"
        chunk = lambda i, j: jnp.minimum(i * nt + j, n_chunks - 1)
        in_specs.append(pl.BlockSpec(
            (cr, cw), lambda i, j: (chunk(i, j) % row_blocks,
                                    src_tiles.start + src_tiles.step * (chunk(i, j) // row_blocks))))
        args.append(src)
        out_specs.append(pl.BlockSpec(
            (cr, cw), lambda i, j: (chunk(i, j) % row_blocks, chunk(i, j) // row_blocks)))
        out_shape.append(jax.ShapeDtypeStruct((src.shape[0], len(src_tiles) * cw), jnp.bfloat16))
    host = dict(
        stages=functools.partial(_proj_stages, epi=epi, dil=dil, cast=cast is not None),
        grid=(m // tm, nt), in_specs=in_specs, args=args, out_specs=out_specs, out_shape=out_shape,
        name=name)
    if guest is not None:
        outs, guest_outs = _launch(host, guest)
        return (outs if cast is not None else outs[0]), guest_outs
    outs = _launch(host)
    return outs if cast is not None else outs[0]


A_TQ = 128
A_STEP = 512
A_HALO = 64
LSE_LANES = 128


def _window_rows(prev_ref, cur_ref, next_ref, lo, hi, cols):
    n = cur_ref.shape[0]
    parts = []
    if lo < 0:
        parts.append(prev_ref[:, cols])
    parts.append(cur_ref[max(lo, 0):min(hi, n), cols])
    if hi > n:
        parts.append(next_ref[:, cols])
    return jnp.concatenate(parts, axis=0) if len(parts) > 1 else parts[0]


def _band_bias(tq, halo, radius, start, seq):
    nk = tq + 2 * halo
    r = lax.broadcasted_iota(jnp.int32, (tq, nk), 0)
    c = lax.broadcasted_iota(jnp.int32, (tq, nk), 1)
    kpos = start - halo + c
    valid = (jnp.abs(c - halo - r) <= radius) & (kpos >= 0) & (kpos < seq)
    return jnp.where(valid, 0.0, NEG_INF)


def _attn_a_kernel(q_ref, kp_ref, kc_ref, kn_ref, vp_ref, vc_ref, vn_ref, o_ref, lse_ref, *, seq):
    step = q_ref.shape[0]
    lane = lax.broadcasted_iota(jnp.int32, (A_TQ, LSE_LANES), 1)
    for b in range(step // A_TQ):
        rows = slice(b * A_TQ, (b + 1) * A_TQ)
        lo, hi = b * A_TQ - A_HALO, (b + 1) * A_TQ + A_HALO
        bias = _band_bias(A_TQ, A_HALO, A_RADIUS, pl.program_id(1) * step + b * A_TQ, seq)
        lse_all = jnp.zeros((A_TQ, LSE_LANES), jnp.float32)
        for hh in range(A_HEADS):
            sl = slice(hh * HEAD_DIM, (hh + 1) * HEAD_DIM)
            k = _window_rows(kp_ref, kc_ref, kn_ref, lo, hi, sl)
            v = _window_rows(vp_ref, vc_ref, vn_ref, lo, hi, sl)
            s = lax.dot_general(q_ref[rows, sl], k, (((1,), (1,)), ((), ())),
                                preferred_element_type=jnp.float32) + bias
            m = jnp.max(s, axis=1, keepdims=True)
            e = jnp.exp2(s - m)
            den = jnp.sum(e, axis=1, keepdims=True)
            o = jnp.dot(e.astype(jnp.bfloat16), v, preferred_element_type=jnp.float32)
            o_ref[rows, sl] = (o * (1.0 / den)).astype(o_ref.dtype)
            lse_all = jnp.where(lane == hh, m * LN2 + jnp.log(den), lse_all)
        lse_ref[rows, :] = lse_all


def _attn_a(qkv, n, seq, offs):
    qo, ko, vo = offs
    step = min(A_STEP, seq)
    nh = seq // A_HALO
    ratio = step // A_HALO
    cur = lambda off: pl.BlockSpec((None, step, A_WIDTH), lambda s, i: (off + s, i, 0))
    prev = lambda off: pl.BlockSpec(
        (None, A_HALO, A_WIDTH), lambda s, i: (off + s, jnp.maximum(ratio * i - 1, 0), 0))
    nxt = lambda off: pl.BlockSpec(
        (None, A_HALO, A_WIDTH), lambda s, i: (off + s, jnp.minimum(ratio * (i + 1), nh - 1), 0))
    return pl.pallas_call(
        functools.partial(_attn_a_kernel, seq=seq),
        grid=(n, seq // step),
        in_specs=[cur(qo), prev(ko), cur(ko), nxt(ko), prev(vo), cur(vo), nxt(vo)],
        out_specs=[pl.BlockSpec((None, step, A_WIDTH), lambda s, i: (s, i, 0)),
                   pl.BlockSpec((None, step, LSE_LANES), lambda s, i: (s, i, 0))],
        out_shape=[jax.ShapeDtypeStruct((n, seq, A_WIDTH), jnp.bfloat16),
                   jax.ShapeDtypeStruct((n, seq, LSE_LANES), jnp.float32)],
        compiler_params=_params(("parallel", "parallel")),
        name="attn_a",
    )(qkv, qkv, qkv, qkv, qkv, qkv, qkv)


def _merge_a_stages(ids, ins, outs, scratch):
    del ids
    c0_ref, c1_ref, c2_ref, cl0_ref, cl1_ref, cl2_ref, z_ref = ins
    u_ref, = outs
    o1_ref, o2_ref, l1_ref, l2_ref = scratch
    tm = u_ref.shape[0]
    o0_ref, l0_ref = c0_ref.at[0], cl0_ref.at[0]
    for c_ref, cl_ref, o_ref, l_ref in ((c1_ref, cl1_ref, o1_ref, l1_ref), (c2_ref, cl2_ref, o2_ref, l2_ref)):
        dil = c_ref.shape[0]
        for r in range(dil):
            rows = pl.ds(r, tm // dil, stride=dil)
            for hh in range(A_HEADS):
                o_ref[hh, rows, :] = c_ref[r, :, hh * HEAD_DIM:(hh + 1) * HEAD_DIM].astype(jnp.float32)
            l_ref[rows, :] = cl_ref[r]
        yield
    l0, l1, l2 = l0_ref[...], l1_ref[...], l2_ref[...]
    m = jnp.maximum(jnp.maximum(l0, l1), l2)
    e0, e1, e2 = jnp.exp(l0 - m), jnp.exp(l1 - m), jnp.exp(l2 - m)
    inv = 1.0 / (e0 + e1 + e2)
    a0, a1, a2 = e0 * inv, e1 * inv, e2 * inv
    for hh in range(A_HEADS):
        sl = slice(hh * HEAD_DIM, (hh + 1) * HEAD_DIM)
        col = slice(hh, hh + 1)
        o = a0[:, col] * o0_ref[:, sl] + a1[:, col] * o1_ref[hh] + a2[:, col] * o2_ref[hh]
        u_ref[:, sl] = (o * z_ref[:, sl].astype(jnp.float32)).astype(u_ref.dtype)
        if hh == A_HEADS // 2 - 1:
            yield


def _merge_a(outs, lses, pz, batch, seq, tm=512):
    bps = seq // tm
    cls = lambda dil, width: pl.BlockSpec((None, dil, tm // dil, width),
                                          lambda i, _: (i // bps, 0, i % bps, 0))
    dils = [d for _, d in DIL_CONFIGS]
    return dict(
        stages=_merge_a_stages,
        grid=(batch * bps, 1),
        in_specs=([cls(d, A_WIDTH) for d in dils] + [cls(d, LSE_LANES) for d in dils]
                  + [pl.BlockSpec((None, tm, TN), lambda i, _: (0, i, 0))]),
        args=[*outs, *lses, pz],
        out_specs=[pl.BlockSpec((tm, A_WIDTH), lambda i, _: (i, 0))],
        out_shape=[jax.ShapeDtypeStruct((batch * seq, A_WIDTH), jnp.bfloat16)],
        scratch=[pltpu.VMEM((A_HEADS, tm, HEAD_DIM), jnp.float32)] * 2
                + [pltpu.VMEM((tm, LSE_LANES), jnp.float32)] * 2,
        name="merge_a")


B_TQ = 128


def _attn_b_stages(ids, ins, outs, scratch, *, seq):
    sink_ref, q0_ref, q1_ref, kvp_ref, kvc_ref, kvn_ref, z0_ref, z1_ref = ins
    u_ref, = outs
    step = u_ref.shape[0]
    ones = jnp.ones((3 * B_TQ, HEAD_DIM), jnp.bfloat16)
    kv_half = B_KV_HEADS * HEAD_DIM
    units = [(b, kh) for b in range(step // B_TQ) for kh in range(B_KV_HEADS)]

    def scores(b, kh):
        rows = slice(b * B_TQ, (b + 1) * B_TQ)
        lo, hi = (b - 1) * B_TQ, (b + 2) * B_TQ
        q_ref = q0_ref if kh < 2 else q1_ref
        base = (kh % 2) * B_GROUP * HEAD_DIM
        heads = [slice(base + g * HEAD_DIM, base + (g + 1) * HEAD_DIM) for g in range(B_GROUP)]
        k = _window_rows(kvp_ref, kvc_ref, kvn_ref, lo, hi, slice(kh * HEAD_DIM, (kh + 1) * HEAD_DIM))
        q = jnp.concatenate([q_ref[rows, sl] for sl in heads], axis=0)
        return lax.dot_general(q, k, (((1,), (1,)), ((), ())), preferred_element_type=jnp.float32)

    def finish(b, kh, s):
        rows = slice(b * B_TQ, (b + 1) * B_TQ)
        lo, hi = (b - 1) * B_TQ, (b + 2) * B_TQ
        z_ref = z0_ref if kh < 2 else z1_ref
        base = (kh % 2) * B_GROUP * HEAD_DIM
        heads = [slice(base + g * HEAD_DIM, base + (g + 1) * HEAD_DIM) for g in range(B_GROUP)]
        bias = _band_bias(B_TQ, B_TQ, B_RADIUS, ids[1] * step + b * B_TQ, seq)
        v = _window_rows(kvp_ref, kvc_ref, kvn_ref, lo, hi,
                         slice(kv_half + kh * HEAD_DIM, kv_half + (kh + 1) * HEAD_DIM))
        es, ms, sks = [], [], []
        for g in range(B_GROUP):
            sg = s[g * B_TQ:(g + 1) * B_TQ] + bias
            sk = sink_ref[kh * B_GROUP + g] * LOG2E
            m = jnp.maximum(jnp.max(sg, axis=1, keepdims=True), sk)
            es.append(jnp.exp2(sg - m).astype(jnp.bfloat16))
            ms.append(m)
            sks.append(sk)
        ov = jnp.dot(jnp.concatenate(es, axis=0), jnp.concatenate([v, ones], axis=1),
                     preferred_element_type=jnp.float32)
        for g in range(B_GROUP):
            og = ov[g * B_TQ:(g + 1) * B_TQ]
            den = og[:, HEAD_DIM:HEAD_DIM + 1] + jnp.exp2(sks[g] - ms[g])
            col = (kh * B_GROUP + g) * HEAD_DIM
            u_ref[rows, col:col + HEAD_DIM] = (
                og[:, :HEAD_DIM] * (1.0 / den) * z_ref[rows, heads[g]].astype(jnp.float32)
            ).astype(u_ref.dtype)

    per_round = 4
    pending = [scores(*u) for u in units[:per_round]]
    yield
    for r in range(0, len(units), per_round):
        for u, sc in zip(units[r:r + per_round], pending):
            finish(*u, sc)
        pending = [scores(*u) for u in units[r + per_round:r + 2 * per_round]]
        yield


def _attn_b(sink, pb, pz, batch, seq, step):
    nq = seq // step
    nh = seq // B_TQ
    ratio = step // B_TQ
    row = lambda slot: pl.BlockSpec((None, step, TN), lambda b, i: (slot, b * nq + i, 0))
    kv_prev = pl.BlockSpec((None, B_TQ, TN), lambda b, i: (2, b * nh + jnp.maximum(ratio * i - 1, 0), 0))
    kv_next = pl.BlockSpec((None, B_TQ, TN),
                           lambda b, i: (2, b * nh + jnp.minimum(ratio * (i + 1), nh - 1), 0))
    width = B_Q_HEADS * HEAD_DIM
    return dict(
        stages=functools.partial(_attn_b_stages, seq=seq),
        grid=(batch, nq),
        in_specs=[pl.BlockSpec(memory_space=pltpu.SMEM),
                  row(0), row(1), kv_prev, row(2), kv_next, row(1), row(2)],
        args=[sink, pb, pb, pb, pb, pb, pz, pz],
        out_specs=[pl.BlockSpec((step, width), lambda b, i: (b * nq + i, 0))],
        out_shape=[jax.ShapeDtypeStruct((batch * seq, width), jnp.bfloat16)],
        name="attn_b")


M_TQ = 512


def _attn_m_kernel(q_ref, k_ref, v_ref, z_ref, u_ref):
    for hh in range(M_HEADS):
        sl = slice(hh * M_HEAD_DIM, (hh + 1) * M_HEAD_DIM)
        s = lax.dot_general(q_ref[:, sl], k_ref[:, sl], (((1,), (1,)), ((), ())),
                            preferred_element_type=jnp.float32)
        m = jnp.max(s, axis=1, keepdims=True)
        e = jnp.exp2(s - m)
        den = jnp.sum(e, axis=1, keepdims=True)
        o = jnp.dot(e.astype(jnp.bfloat16), v_ref[:, sl], preferred_element_type=jnp.float32) * (1.0 / den)
        u_ref[:, sl] = (o * z_ref[:, sl].astype(jnp.float32)).astype(u_ref.dtype)


def _attn_m(pb, kv_mem, pz, batch, seq):
    nq = seq // M_TQ
    width = M_HEADS * M_HEAD_DIM
    return pl.pallas_call(
        _attn_m_kernel,
        grid=(batch, nq),
        in_specs=[pl.BlockSpec((None, M_TQ, TN), lambda b, i: (3, b * nq + i, 0)),
                  pl.BlockSpec((None, MEM_LEN, TN), lambda b, i: (0, b, 0)),
                  pl.BlockSpec((None, MEM_LEN, TN), lambda b, i: (1, b, 0)),
                  pl.BlockSpec((None, M_TQ, TN), lambda b, i: (3, b * nq + i, 0))],
        out_specs=pl.BlockSpec((M_TQ, width), lambda b, i: (b * nq + i, 0)),
        out_shape=jax.ShapeDtypeStruct((batch * seq, width), jnp.bfloat16),
        compiler_params=_params(("parallel", "parallel")),
        name="attn_m",
    )(pb, kv_mem, kv_mem, pz)


def _branch_stages(ids, ins, outs, scratch):
    del ids, scratch
    ua_ref, ub_ref, um_ref, w_ref, ga_ref, gb_ref, gm_ref = ins
    u_ref, = outs
    a_hi = A_WIDTH
    b_hi = A_WIDTH + B_Q_HEADS * HEAD_DIM
    dot = functools.partial(jnp.dot, preferred_element_type=jnp.float32)
    acc = ga_ref[...].astype(jnp.float32) * dot(ua_ref[...], w_ref[:a_hi, :])
    yield
    acc += gb_ref[...].astype(jnp.float32) * dot(ub_ref[...], w_ref[a_hi:b_hi, :])
    yield
    acc += gm_ref[...].astype(jnp.float32) * dot(um_ref[...], w_ref[b_hi:, :])
    u_ref[...] = acc.astype(u_ref.dtype)


def _branch(ua, ub, um, w_branch, gates, tm=512):
    m = ua.shape[0]
    k = w_branch.shape[0]
    n_tiles = D_MODEL // TN
    act = lambda width: pl.BlockSpec((tm, width), lambda j, i: (i, 0))
    gate = pl.BlockSpec((None, tm, TN), lambda j, i: (j, i, 0))
    return dict(
        stages=_branch_stages,
        grid=(n_tiles, m // tm),
        in_specs=[act(ua.shape[1]), act(ub.shape[1]), act(um.shape[1]),
                  pl.BlockSpec((k, TN), lambda j, i: (0, j)),
                  gate, gate, gate],
        args=[ua, ub, um, w_branch, *gates],
        out_specs=[pl.BlockSpec((tm, TN), lambda j, i: (i, j))],
        out_shape=[jax.ShapeDtypeStruct((m, D_MODEL), jnp.bfloat16)],
        name="branch_proj")


def _out_kernel(u_ref, w_ref, x_ref, g_ref, y_ref):
    y = x_ref[...] + jnp.dot(u_ref[...], w_ref[...], preferred_element_type=jnp.float32)
    inv = lax.rsqrt(jnp.mean(y * y, axis=1, keepdims=True) + NORM_EPS)
    y_ref[...] = (y * inv) * g_ref[...]


def _out_proj(u, w_out, x, g_final, tm=256):
    m, k = u.shape
    n = x.shape[1]
    return pl.pallas_call(
        _out_kernel,
        grid=(m // tm,),
        in_specs=[pl.BlockSpec((tm, k), lambda i: (i, 0)),
                  pl.BlockSpec((k, n), lambda i: (0, 0), pipeline_mode=pl.Buffered(1)),
                  pl.BlockSpec((tm, n), lambda i: (i, 0)),
                  pl.BlockSpec((1, n), lambda i: (0, 0))],
        out_specs=pl.BlockSpec((tm, n), lambda i: (i, 0)),
        out_shape=jax.ShapeDtypeStruct((m, n), jnp.float32),
        compiler_params=_params(("arbitrary",), vmem=OUT_VMEM),
        name="out_proj",
    )(u, w_out, x, g_final.reshape(1, n))


def _rope_tables(seq):
    inv_freq = ROPE_THETA ** (-jnp.arange(0, HEAD_DIM, 2, dtype=jnp.float32) / HEAD_DIM)
    ang = jnp.arange(seq, dtype=jnp.float32)[:, None] * inv_freq[None, :]
    cos, sin = jnp.cos(ang), jnp.sin(ang)
    cos, sin = jnp.concatenate([cos, cos], axis=1), jnp.concatenate([-sin, sin], axis=1)
    one, zero = jnp.ones_like(cos), jnp.zeros_like(sin)
    q_scale = HEAD_DIM ** -0.5 * LOG2E
    m_scale = M_HEAD_DIM ** -0.5 * LOG2E
    return (jnp.stack([cos * q_scale, cos, one, one * m_scale]),
            jnp.stack([sin * q_scale, sin, zero, zero]))


def _attend(h, hm, pb, pz, w, tabs, batch, seq, cast_mem=None):
    outs, lses = [], []
    for gi, (_, dil) in enumerate(DIL_CONFIGS):
        sd = seq // dil
        n = batch * dil
        a_sets = ((ROT_Q, ROT_K, ROT_ID),) * 2
        qkv = _proj(h, w["a"][gi], (0, 3), "rot", batch=batch, tm=1024, dil=dil, tabs=tabs, sets=a_sets,
                    cast=cast_mem if gi == 0 else None, name=f"proj_a{gi}")
        if gi == 0 and cast_mem is not None:
            qkv, w_mem = qkv
            w = dict(w, mem=w_mem)
        o, lse = _attn_a(qkv.reshape(3 * n, sd, A_WIDTH), n, sd, (0, n, 2 * n))
        outs.append(o.reshape(batch, dil, sd, A_WIDTH))
        lses.append(lse.reshape(batch, dil, sd, LSE_LANES))
    kv_mem = _proj(hm, w["mem"], (0, 2), "none", batch=1, tm=batch * MEM_LEN, name="proj_mem")
    um = _attn_m(pb, kv_mem.reshape(2, batch * MEM_LEN, TN), pz, batch, seq)
    return outs, lses, um, w["mem"]


def kernel(x_prompt, x_sample, mem_prompt, mem_sample, g_norm, w_in, attn_sink, g_mem, w_mem_kv, w_branch, w_out, g_final):
    bf = jnp.bfloat16
    w_in, w_mem_kv, w_branch, w_out = w_in[0], w_mem_kv[0], w_branch[0], w_out[0]
    trunks = []
    tabs = _rope_tables(max(x_prompt.shape[1], x_sample.shape[1]))
    for x, mem in ((x_prompt, mem_prompt), (x_sample, mem_sample)):
        batch, seq, d = x.shape
        x2 = x.reshape(batch * seq, d)
        trunks.append(dict(
            batch=batch, seq=seq, x2=x2, tabs=tabs,
            hm=_rmsnorm(mem.reshape(batch * MEM_LEN, d), g_mem[0], bf)))

    w_b = _cast_tiles(w_in, B_TILES)
    w_z = _cast_tiles(w_in, Z_TILES)
    b_sets = ((ROT_Q, ROT_Q, ROT_K, ROT_M), (ROT_Q, ROT_Q, ROT_ID, ROT_M))
    g_lo = G_TILES[0]
    n_dil = len(DIL_CONFIGS)
    out_tiles = -(-w_out.shape[1] // TN)

    def proj(t, wts, epi, cast, name, **kw):
        out, w_next = _proj(t["h"], wts, (0, 4), epi, batch=t["batch"], tm=1024, cast=cast, name=name, **kw)
        return out.reshape(4, t["batch"] * t["seq"], TN), w_next

    p, s_ = trunks
    for t in trunks:
        t["h"], t["pb"] = _norm_proj(t["x2"], g_norm[0], w_b, tabs, b_sets, batch=t["batch"])
    pb_p, pb_s = p["pb"], s_["pb"]
    pz_p, w_ga = proj(p, w_z, "silu", (w_in, (g_lo, g_lo + 4)), "proj_z")
    pz_s, w_gb = proj(s_, w_z, "silu", (w_in, (g_lo + 4, g_lo + 8)), "proj_z")
    ga_p, w_gm = proj(p, w_ga, "sigmoid", (w_in, (g_lo + 8, g_lo + 12)), "proj_ga")
    ga_s, w_a0 = proj(s_, w_ga, "sigmoid", (w_in, (0, A_TILES[1], n_dil)), "proj_ga")
    gb_p, w_a1 = proj(p, w_gb, "sigmoid", (w_in, (1, A_TILES[1], n_dil)), "proj_gb")
    gb_s, w_a2 = proj(s_, w_gb, "sigmoid", (w_in, (2, A_TILES[1], n_dil)), "proj_gb")

    def proj_with_attn_b(t, wts, cast, pb, pz):
        rows = t["batch"] * t["seq"]
        guest = _attn_b(attn_sink[0], pb, pz, t["batch"], t["seq"], step=rows // (rows // 1024 * 4))
        (out, w_next), (ub,) = _proj(t["h"], wts, (0, 4), "sigmoid_parts", batch=t["batch"], tm=1024,
                                     cast=cast, guest=guest, name="proj_gm")
        return out.reshape(4, rows, TN), w_next, ub

    gm_p, w_o, ub_p = proj_with_attn_b(p, w_gm, (w_out, (0, out_tiles)), pb_p, pz_p)
    gm_s, w_br, ub_s = proj_with_attn_b(s_, w_gm, (w_branch, (0, D_MODEL // TN)), pb_s, pz_s)

    w = dict(a=(w_a0, w_a1, w_a2))
    ao_p, al_p, um_p, w_m = _attend(p["h"], p["hm"], pb_p, pz_p, w, tabs, p["batch"], p["seq"],
                                    cast_mem=(w_mem_kv, (0, 2)))
    ao_s, al_s, um_s, _ = _attend(s_["h"], s_["hm"], pb_s, pz_s, dict(w, mem=w_m), tabs, s_["batch"], s_["seq"])
    ua_p, = _launch(_merge_a(ao_p, al_p, pz_p, p["batch"], p["seq"]))
    branch_p = _branch(ua_p, ub_p, um_p, w_br, (ga_p, gb_p, gm_p))
    steps = branch_p["grid"][0] * branch_p["grid"][1]
    (u_p,), (ua_s,) = _launch(branch_p, _merge_a(ao_s, al_s, pz_s, s_["batch"], s_["seq"],
                                                 tm=s_["batch"] * s_["seq"] // steps))
    u_s, = _launch(_branch(ua_s, ub_s, um_s, w_br, (ga_s, gb_s, gm_s)))
    finish = lambda t, u: _out_proj(u, w_o, t["x2"], g_final).reshape(t["batch"], t["seq"], t["x2"].shape[1])
    return finish(p, u_p), finish(s_, u_s)
```

```python
import functools

import jax
import jax.numpy as jnp
from jax import lax
from jax.experimental import pallas as pl
from jax.experimental.pallas import tpu as pltpu

D_MODEL = 4096
HEAD_DIM = 128
ROPE_THETA = 10000.0
NORM_EPS = 1e-6
NEG_INF = -1e30
LOG2E = 1.4426950408889634
LN2 = 0.6931471805599453

DIL_CONFIGS = ((128, 1), (512, 4), (2048, 16))
A_HEADS = 8
A_WIDTH = A_HEADS * HEAD_DIM
A_RADIUS = 64

B_Q_HEADS = 16
B_KV_HEADS = 4
B_GROUP = B_Q_HEADS // B_KV_HEADS
B_RADIUS = 128

MEM_LEN = 256
M_HEADS = 4
M_HEAD_DIM = 256

TN = 1024
A_TILES = (0, 9)
B_TILES = (9, 13)
Z_TILES = (13, 17)
G_TILES = (17, 29)
ROT_Q, ROT_K, ROT_ID, ROT_M = range(4)

VMEM_LIMIT = 56 * 1024 * 1024
OUT_VMEM = 60 * 1024 * 1024


def _params(sem, vmem=VMEM_LIMIT):
    return pltpu.CompilerParams(dimension_semantics=sem, vmem_limit_bytes=vmem)


def _launch(host, guest=None):
    parts = [host] if guest is None else [host, guest]
    grid = host["grid"]
    if guest is not None:
        g0, g1 = guest["grid"]
        assert g0 * g1 == grid[0] * grid[1], (guest["grid"], grid)
        to_guest = lambda i, j: divmod(i * grid[1] + j, g1)
        remap = lambda spec: pl.BlockSpec(
            spec.block_shape, (lambda i, j, f=spec.index_map: f(*to_guest(i, j))),
            memory_space=spec.memory_space) if spec.index_map is not None else spec
        guest = dict(guest, in_specs=[remap(sp) for sp in guest["in_specs"]],
                     out_specs=[remap(sp) for sp in guest["out_specs"]])
        parts = [host, guest]
    n_in = [len(p["in_specs"]) for p in parts]
    n_out = [len(p["out_specs"]) for p in parts]
    n_scr = [len(p.get("scratch", [])) for p in parts]

    def body(*refs):
        refs = list(refs)
        take = lambda n: [refs.pop(0) for _ in range(n)]
        ins = [take(n) for n in n_in]
        outs = [take(n) for n in n_out]
        scr = [take(n) for n in n_scr]
        ids = (pl.program_id(0), pl.program_id(1)) if len(grid) == 2 else (pl.program_id(0),)
        gens = [host["stages"](ids, ins[0], outs[0], scr[0])]
        if guest is not None:
            gens.insert(0, guest["stages"](to_guest(*ids), ins[1], outs[1], scr[1]))
        while gens:
            gens = [g for g in gens if next(g, "done") != "done"]

    res = pl.pallas_call(
        body,
        grid=grid,
        in_specs=[sp for p in parts for sp in p["in_specs"]],
        out_specs=[sp for p in parts for sp in p["out_specs"]],
        out_shape=[sh for p in parts for sh in p["out_shape"]],
        scratch_shapes=[sc for p in parts for sc in p.get("scratch", [])],
        compiler_params=_params(("arbitrary",) * len(grid)),
        name=host["name"] if guest is None else host["name"] + "_" + guest["name"],
    )(*[a for p in parts for a in p["args"]])
    if guest is None:
        return res
    return res[:n_out[0]], res[n_out[0]:]


def _rmsnorm_kernel(x_ref, g_ref, o_ref):
    x = x_ref[...]
    ms = jnp.mean(x * x, axis=-1, keepdims=True)
    o_ref[...] = ((x * lax.rsqrt(ms + NORM_EPS)) * g_ref[...]).astype(o_ref.dtype)


def _rmsnorm(x, g, out_dtype, tm=256):
    m, d = x.shape
    return pl.pallas_call(
        _rmsnorm_kernel,
        grid=(m // tm,),
        in_specs=[pl.BlockSpec((tm, d), lambda i: (i, 0)),
                  pl.BlockSpec((1, d), lambda i: (0, 0))],
        out_specs=pl.BlockSpec((tm, d), lambda i: (i, 0)),
        out_shape=jax.ShapeDtypeStruct((m, d), out_dtype),
        compiler_params=_params(("parallel",)),
        name="rmsnorm",
    )(x, g.reshape(1, d))


def _sigmoid(x):
    return 0.5 * jnp.tanh(0.5 * x) + 0.5


def _proj_stages(ids, ins, outs, scratch, *, epi, dil, cast):
    del ids
    h_ref, w_ref = ins[:2]
    o_ref = outs[0]
    tm = h_ref.shape[0]
    if cast:
        outs[1][...] = ins[-1][...].astype(outs[1].dtype)
    if epi == "sigmoid_parts":
        nq = 4
        wq = TN // nq
        for q in range(nq):
            cols = slice(q * wq, (q + 1) * wq)
            acc = jnp.dot(h_ref[...], w_ref[:, cols], preferred_element_type=jnp.float32)
            o_ref[0, :, cols] = _sigmoid(acc).astype(o_ref.dtype)
            yield
        return
    yield
    acc = jnp.dot(h_ref[...], w_ref[...], preferred_element_type=jnp.float32)
    if epi == "none":
        o_ref[0] = acc.astype(o_ref.dtype)
    elif epi == "silu":
        o_ref[0] = (acc * _sigmoid(acc)).astype(o_ref.dtype)
    elif epi == "sigmoid":
        o_ref[0] = _sigmoid(acc).astype(o_ref.dtype)
    else:
        cos_lo, sin_lo, cos_hi, sin_hi = ins[2:6]
        heads = [slice(hh * HEAD_DIM, (hh + 1) * HEAD_DIM) for hh in range(TN // HEAD_DIM)]
        for hh, sl in enumerate(heads):
            cos_ref, sin_ref = (cos_lo, sin_lo) if hh < len(heads) // 2 else (cos_hi, sin_hi)
            x = acc[:, sl]
            x = x * cos_ref[...] + pltpu.roll(x, HEAD_DIM // 2, 1) * sin_ref[...]
            if dil > 1:
                o_ref[:, :, sl] = jnp.swapaxes(x.reshape(tm // dil, dil, HEAD_DIM), 0, 1).astype(o_ref.dtype)
            else:
                o_ref[0, :, sl] = x.astype(o_ref.dtype)


def _norm_proj_kernel(x_ref, g_ref, w_ref, cos_ref, sin_ref, *rest, sets):
    if len(rest) == 4:
        src_ref, h_ref, o_ref, dst_ref = rest
        dst_ref[...] = src_ref[...].astype(dst_ref.dtype)
    else:
        h_ref, o_ref = rest
    x = x_ref[...]
    h = ((x * lax.rsqrt(jnp.mean(x * x, axis=-1, keepdims=True) + NORM_EPS)) * g_ref[...]).astype(h_ref.dtype)
    h_ref[...] = h
    heads_per_tile = TN // HEAD_DIM
    for t in range(o_ref.shape[0]):
        acc = jnp.dot(h, w_ref[:, t * TN:(t + 1) * TN], preferred_element_type=jnp.float32)
        for hh in range(heads_per_tile):
            sl = slice(hh * HEAD_DIM, (hh + 1) * HEAD_DIM)
            tab = sets[hh * 2 // heads_per_tile][t]
            xh = acc[:, sl]
            o_ref[t, :, sl] = (xh * cos_ref[tab] + pltpu.roll(xh, HEAD_DIM // 2, 1) * sin_ref[tab]).astype(o_ref.dtype)


def _norm_proj(x, g, w, tabs, sets, *, batch, tm=256, cast=None):
    m, d = x.shape
    seq = m // batch
    bps = seq // tm
    nt = w.shape[1] // TN
    n_sets = tabs[0].shape[0]
    tab_spec = pl.BlockSpec((n_sets, tm, HEAD_DIM), lambda i: (0, i % bps, 0))
    in_specs = [pl.BlockSpec((tm, d), lambda i: (i, 0)),
                pl.BlockSpec((1, d), lambda i: (0, 0)),
                pl.BlockSpec(w.shape, lambda i: (0, 0), pipeline_mode=pl.Buffered(1)),
                tab_spec, tab_spec]
    out_specs = [pl.BlockSpec((tm, d), lambda i: (i, 0)),
                 pl.BlockSpec((nt, tm, TN), lambda i: (0, i, 0))]
    out_shape = [jax.ShapeDtypeStruct((m, d), jnp.bfloat16),
                 jax.ShapeDtypeStruct((nt, m, TN), jnp.bfloat16)]
    args = [x, g.reshape(1, d), w, *tabs]
    if cast is not None:
        src, (lo, hi) = cast
        steps = m // tm
        row_blocks = steps // (hi - lo)
        cr = src.shape[0] // row_blocks
        assert row_blocks * (hi - lo) == steps and cr * row_blocks == src.shape[0] and cr % 16 == 0
        in_specs.append(pl.BlockSpec((cr, TN), lambda i: (i % row_blocks, lo + i // row_blocks)))
        out_specs.append(pl.BlockSpec((cr, TN), lambda i: (i % row_blocks, i // row_blocks)))
        out_shape.append(jax.ShapeDtypeStruct((src.shape[0], (hi - lo) * TN), jnp.bfloat16))
        args.append(src)
    return pl.pallas_call(
        functools.partial(_norm_proj_kernel, sets=sets),
        grid=(m // tm,),
        in_specs=in_specs,
        out_specs=out_specs,
        out_shape=out_shape,
        compiler_params=_params(("arbitrary",), vmem=OUT_VMEM),
        name="norm_proj_b",
    )(*args)


def _cast_kernel(src_ref, dst_ref):
    dst_ref[...] = src_ref[...].astype(dst_ref.dtype)


def _cast_tiles(src, tiles):
    lo, hi = tiles
    rows = src.shape[0]
    cr = min(CAST_ROWS, rows)
    return pl.pallas_call(
        _cast_kernel,
        grid=(hi - lo, rows // cr),
        in_specs=[pl.BlockSpec((cr, TN), lambda j, i: (i, lo + j))],
        out_specs=pl.BlockSpec((cr, TN), lambda j, i: (i, j)),
        out_shape=jax.ShapeDtypeStruct((rows, (hi - lo) * TN), jnp.bfloat16),
        compiler_params=_params(("parallel", "parallel")),
        name="cast_tiles",
    )(src)


def _select(j, values):
    if len(set(values)) == 1:
        return values[0]
    out = values[-1]
    for t in range(len(values) - 2, -1, -1):
        out = jnp.where(j == t, values[t], out)
    return out


CAST_ROWS = 512


def _proj(h, w, tiles, epi, *, batch, tm, dil=1, tabs=None, sets=None, cast=None, guest=None, name):
    m, k = h.shape
    seq = m // batch
    col_tiles = range(*tiles)
    lo, step, nt = col_tiles.start, col_tiles.step, len(col_tiles)
    assert seq % tm == 0 and tm % dil == 0 and (dil == 1 or epi == "rot")
    bps = seq // tm
    in_specs = [pl.BlockSpec((tm, k), lambda i, j: (i, 0)),
                pl.BlockSpec((k, TN), lambda i, j: (0, lo + step * j))]
    args = [h, w]
    if epi == "rot":
        for half in sets:
            assert len(half) == nt
            spec = pl.BlockSpec((None, tm, HEAD_DIM),
                                lambda i, j, half=half: (_select(j, half), i % bps, 0))
            in_specs += [spec, spec]
            args += list(tabs)
    out_specs = [pl.BlockSpec((None, None, dil, tm // dil, TN), lambda i, j: (j, i // bps, 0, i % bps, 0))]
    out_shape = [jax.ShapeDtypeStruct((nt, batch, dil, seq // dil, TN), jnp.bfloat16)]
    if cast is not None:
        src, src_tiles = cast
        src_tiles = range(*src_tiles)
        cr = min(CAST_ROWS, src.shape[0])
        cw = min(TN, src.shape[1])
        row_blocks = src.shape[0] // cr
        n_chunks = row_blocks * len(src_tiles)
        assert n_chunks <= (m // tm) * nt, "not enough grid steps to cast this weight group"
        chunk = lambda i, j: jnp.minimum(i * nt + j, n_chunks - 1)
        in_specs.append(pl.BlockSpec(
            (cr, cw), lambda i, j: (chunk(i, j) % row_blocks,
                                    src_tiles.start + src_tiles.step * (chunk(i, j) // row_blocks))))
        args.append(src)
        out_specs.append(pl.BlockSpec(
            (cr, cw), lambda i, j: (chunk(i, j) % row_blocks, chunk(i, j) // row_blocks)))
        out_shape.append(jax.ShapeDtypeStruct((src.shape[0], len(src_tiles) * cw), jnp.bfloat16))
    host = dict(
        stages=functools.partial(_proj_stages, epi=epi, dil=dil, cast=cast is not None),
        grid=(m // tm, nt), in_specs=in_specs, args=args, out_specs=out_specs, out_shape=out_shape,
        name=name)
    if guest is not None:
        outs, guest_outs = _launch(host, guest)
        return (outs if cast is not None else outs[0]), guest_outs
    outs = _launch(host)
    return outs if cast is not None else outs[0]


A_TQ = 128
A_STEP = 512
A_HALO = 64
LSE_LANES = 128


def _window_rows(prev_ref, cur_ref, next_ref, lo, hi, cols):
    n = cur_ref.shape[0]
    parts = []
    if lo < 0:
        parts.append(prev_ref[:, cols])
    parts.append(cur_ref[max(lo, 0):min(hi, n), cols])
    if hi > n:
        parts.append(next_ref[:, cols])
    return jnp.concatenate(parts, axis=0) if len(parts) > 1 else parts[0]


def _band_bias(tq, halo, radius, start, seq):
    nk = tq + 2 * halo
    r = lax.broadcasted_iota(jnp.int32, (tq, nk), 0)
    c = lax.broadcasted_iota(jnp.int32, (tq, nk), 1)
    kpos = start - halo + c
    valid = (jnp.abs(c - halo - r) <= radius) & (kpos >= 0) & (kpos < seq)
    return jnp.where(valid, 0.0, NEG_INF)


def _attn_a_kernel(q_ref, kp_ref, kc_ref, kn_ref, vp_ref, vc_ref, vn_ref, o_ref, lse_ref, *, seq):
    step = q_ref.shape[0]
    lane = lax.broadcasted_iota(jnp.int32, (A_TQ, LSE_LANES), 1)
    for b in range(step // A_TQ):
        rows = slice(b * A_TQ, (b + 1) * A_TQ)
        lo, hi = b * A_TQ - A_HALO, (b + 1) * A_TQ + A_HALO
        bias = _band_bias(A_TQ, A_HALO, A_RADIUS, pl.program_id(1) * step + b * A_TQ, seq)
        lse_all = jnp.zeros((A_TQ, LSE_LANES), jnp.float32)
        for hh in range(A_HEADS):
            sl = slice(hh * HEAD_DIM, (hh + 1) * HEAD_DIM)
            k = _window_rows(kp_ref, kc_ref, kn_ref, lo, hi, sl)
            v = _window_rows(vp_ref, vc_ref, vn_ref, lo, hi, sl)
            s = lax.dot_general(q_ref[rows, sl], k, (((1,), (1,)), ((), ())),
                                preferred_element_type=jnp.float32) + bias
            m = jnp.max(s, axis=1, keepdims=True)
            e = jnp.exp2(s - m)
            den = jnp.sum(e, axis=1, keepdims=True)
            o = jnp.dot(e.astype(jnp.bfloat16), v, preferred_element_type=jnp.float32)
            o_ref[rows, sl] = (o * (1.0 / den)).astype(o_ref.dtype)
            lse_all = jnp.where(lane == hh, m * LN2 + jnp.log(den), lse_all)
        lse_ref[rows, :] = lse_all


def _attn_a(qkv, n, seq, offs):
    qo, ko, vo = offs
    step = min(A_STEP, seq)
    nh = seq // A_HALO
    ratio = step // A_HALO
    cur = lambda off: pl.BlockSpec((None, step, A_WIDTH), lambda s, i: (off + s, i, 0))
    prev = lambda off: pl.BlockSpec(
        (None, A_HALO, A_WIDTH), lambda s, i: (off + s, jnp.maximum(ratio * i - 1, 0), 0))
    nxt = lambda off: pl.BlockSpec(
        (None, A_HALO, A_WIDTH), lambda s, i: (off + s, jnp.minimum(ratio * (i + 1), nh - 1), 0))
    return pl.pallas_call(
        functools.partial(_attn_a_kernel, seq=seq),
        grid=(n, seq // step),
        in_specs=[cur(qo), prev(ko), cur(ko), nxt(ko), prev(vo), cur(vo), nxt(vo)],
        out_specs=[pl.BlockSpec((None, step, A_WIDTH), lambda s, i: (s, i, 0)),
                   pl.BlockSpec((None, step, LSE_LANES), lambda s, i: (s, i, 0))],
        out_shape=[jax.ShapeDtypeStruct((n, seq, A_WIDTH), jnp.bfloat16),
                   jax.ShapeDtypeStruct((n, seq, LSE_LANES), jnp.float32)],
        compiler_params=_params(("parallel", "parallel")),
        name="attn_a",
    )(qkv, qkv, qkv, qkv, qkv, qkv, qkv)


def _merge_a_stages(ids, ins, outs, scratch):
    del ids
    c0_ref, c1_ref, c2_ref, cl0_ref, cl1_ref, cl2_ref, z_ref = ins
    u_ref, = outs
    o1_ref, o2_ref, l1_ref, l2_ref = scratch
    tm = u_ref.shape[0]
    o0_ref, l0_ref = c0_ref.at[0], cl0_ref.at[0]
    for c_ref, cl_ref, o_ref, l_ref in ((c1_ref, cl1_ref, o1_ref, l1_ref), (c2_ref, cl2_ref, o2_ref, l2_ref)):
        dil = c_ref.shape[0]
        for r in range(dil):
            rows = pl.ds(r, tm // dil, stride=dil)
            for hh in range(A_HEADS):
                o_ref[hh, rows, :] = c_ref[r, :, hh * HEAD_DIM:(hh + 1) * HEAD_DIM].astype(jnp.float32)
            l_ref[rows, :] = cl_ref[r]
        yield
    l0, l1, l2 = l0_ref[...], l1_ref[...], l2_ref[...]
    m = jnp.maximum(jnp.maximum(l0, l1), l2)
    e0, e1, e2 = jnp.exp(l0 - m), jnp.exp(l1 - m), jnp.exp(l2 - m)
    inv = 1.0 / (e0 + e1 + e2)
    a0, a1, a2 = e0 * inv, e1 * inv, e2 * inv
    for hh in range(A_HEADS):
        sl = slice(hh * HEAD_DIM, (hh + 1) * HEAD_DIM)
        col = slice(hh, hh + 1)
        o = a0[:, col] * o0_ref[:, sl] + a1[:, col] * o1_ref[hh] + a2[:, col] * o2_ref[hh]
        u_ref[:, sl] = (o * z_ref[:, sl].astype(jnp.float32)).astype(u_ref.dtype)
        if hh == A_HEADS // 2 - 1:
            yield


def _merge_a(outs, lses, pz, batch, seq, tm=512):
    bps = seq // tm
    cls = lambda dil, width: pl.BlockSpec((None, dil, tm // dil, width),
                                          lambda i, _: (i // bps, 0, i % bps, 0))
    dils = [d for _, d in DIL_CONFIGS]
    return dict(
        stages=_merge_a_stages,
        grid=(batch * bps, 1),
        in_specs=([cls(d, A_WIDTH) for d in dils] + [cls(d, LSE_LANES) for d in dils]
                  + [pl.BlockSpec((None, tm, TN), lambda i, _: (0, i, 0))]),
        args=[*outs, *lses, pz],
        out_specs=[pl.BlockSpec((tm, A_WIDTH), lambda i, _: (i, 0))],
        out_shape=[jax.ShapeDtypeStruct((batch * seq, A_WIDTH), jnp.bfloat16)],
        scratch=[pltpu.VMEM((A_HEADS, tm, HEAD_DIM), jnp.float32)] * 2
                + [pltpu.VMEM((tm, LSE_LANES), jnp.float32)] * 2,
        name="merge_a")


B_TQ = 128


def _attn_b_stages(ids, ins, outs, scratch, *, seq):
    sink_ref, q0_ref, q1_ref, kvp_ref, kvc_ref, kvn_ref, z0_ref, z1_ref = ins
    u_ref, = outs
    step = u_ref.shape[0]
    ones = jnp.ones((3 * B_TQ, HEAD_DIM), jnp.bfloat16)
    kv_half = B_KV_HEADS * HEAD_DIM
    units = [(b, kh) for b in range(step // B_TQ) for kh in range(B_KV_HEADS)]

    def scores(b, kh):
        rows = slice(b * B_TQ, (b + 1) * B_TQ)
        lo, hi = (b - 1) * B_TQ, (b + 2) * B_TQ
        q_ref = q0_ref if kh < 2 else q1_ref
        base = (kh % 2) * B_GROUP * HEAD_DIM
        heads = [slice(base + g * HEAD_DIM, base + (g + 1) * HEAD_DIM) for g in range(B_GROUP)]
        k = _window_rows(kvp_ref, kvc_ref, kvn_ref, lo, hi, slice(kh * HEAD_DIM, (kh + 1) * HEAD_DIM))
        q = jnp.concatenate([q_ref[rows, sl] for sl in heads], axis=0)
        return lax.dot_general(q, k, (((1,), (1,)), ((), ())), preferred_element_type=jnp.float32)

    def finish(b, kh, s):
        rows = slice(b * B_TQ, (b + 1) * B_TQ)
        lo, hi = (b - 1) * B_TQ, (b + 2) * B_TQ
        z_ref = z0_ref if kh < 2 else z1_ref
        base = (kh % 2) * B_GROUP * HEAD_DIM
        heads = [slice(base + g * HEAD_DIM, base + (g + 1) * HEAD_DIM) for g in range(B_GROUP)]
        bias = _band_bias(B_TQ, B_TQ, B_RADIUS, ids[1] * step + b * B_TQ, seq)
        v = _window_rows(kvp_ref, kvc_ref, kvn_ref, lo, hi,
                         slice(kv_half + kh * HEAD_DIM, kv_half + (kh + 1) * HEAD_DIM))
        es, ms, sks = [], [], []
        for g in range(B_GROUP):
            sg = s[g * B_TQ:(g + 1) * B_TQ] + bias
            sk = sink_ref[kh * B_GROUP + g] * LOG2E
            m = jnp.maximum(jnp.max(sg, axis=1, keepdims=True), sk)
            es.append(jnp.exp2(sg - m).astype(jnp.bfloat16))
            ms.append(m)
            sks.append(sk)
        ov = jnp.dot(jnp.concatenate(es, axis=0), jnp.concatenate([v, ones], axis=1),
                     preferred_element_type=jnp.float32)
        for g in range(B_GROUP):
            og = ov[g * B_TQ:(g + 1) * B_TQ]
            den = og[:, HEAD_DIM:HEAD_DIM + 1] + jnp.exp2(sks[g] - ms[g])
            col = (kh * B_GROUP + g) * HEAD_DIM
            u_ref[rows, col:col + HEAD_DIM] = (
                og[:, :HEAD_DIM] * (1.0 / den) * z_ref[rows, heads[g]].astype(jnp.float32)
            ).astype(u_ref.dtype)

    per_round = 4
    pending = [scores(*u) for u in units[:per_round]]
    yield
    for r in range(0, len(units), per_round):
        for u, sc in zip(units[r:r + per_round], pending):
            finish(*u, sc)
        pending = [scores(*u) for u in units[r + per_round:r + 2 * per_round]]
        yield


def _attn_b(sink, pb, pz, batch, seq, step):
    nq = seq // step
    nh = seq // B_TQ
    ratio = step // B_TQ
    row = lambda slot: pl.BlockSpec((None, step, TN), lambda b, i: (slot, b * nq + i, 0))
    kv_prev = pl.BlockSpec((None, B_TQ, TN), lambda b, i: (2, b * nh + jnp.maximum(ratio * i - 1, 0), 0))
    kv_next = pl.BlockSpec((None, B_TQ, TN),
                           lambda b, i: (2, b * nh + jnp.minimum(ratio * (i + 1), nh - 1), 0))
    width = B_Q_HEADS * HEAD_DIM
    return dict(
        stages=functools.partial(_attn_b_stages, seq=seq),
        grid=(batch, nq),
        in_specs=[pl.BlockSpec(memory_space=pltpu.SMEM),
                  row(0), row(1), kv_prev, row(2), kv_next, row(1), row(2)],
        args=[sink, pb, pb, pb, pb, pb, pz, pz],
        out_specs=[pl.BlockSpec((step, width), lambda b, i: (b * nq + i, 0))],
        out_shape=[jax.ShapeDtypeStruct((batch * seq, width), jnp.bfloat16)],
        name="attn_b")


M_TQ = 512


def _attn_m_kernel(q_ref, k_ref, v_ref, z_ref, u_ref):
    for hh in range(M_HEADS):
        sl = slice(hh * M_HEAD_DIM, (hh + 1) * M_HEAD_DIM)
        s = lax.dot_general(q_ref[:, sl], k_ref[:, sl], (((1,), (1,)), ((), ())),
                            preferred_element_type=jnp.float32)
        m = jnp.max(s, axis=1, keepdims=True)
        e = jnp.exp2(s - m)
        den = jnp.sum(e, axis=1, keepdims=True)
        o = jnp.dot(e.astype(jnp.bfloat16), v_ref[:, sl], preferred_element_type=jnp.float32) * (1.0 / den)
        u_ref[:, sl] = (o * z_ref[:, sl].astype(jnp.float32)).astype(u_ref.dtype)


def _attn_m(pb, kv_mem, pz, batch, seq):
    nq = seq // M_TQ
    width = M_HEADS * M_HEAD_DIM
    return pl.pallas_call(
        _attn_m_kernel,
        grid=(batch, nq),
        in_specs=[pl.BlockSpec((None, M_TQ, TN), lambda b, i: (3, b * nq + i, 0)),
                  pl.BlockSpec((None, MEM_LEN, TN), lambda b, i: (0, b, 0)),
                  pl.BlockSpec((None, MEM_LEN, TN), lambda b, i: (1, b, 0)),
                  pl.BlockSpec((None, M_TQ, TN), lambda b, i: (3, b * nq + i, 0))],
        out_specs=pl.BlockSpec((M_TQ, width), lambda b, i: (b * nq + i, 0)),
        out_shape=jax.ShapeDtypeStruct((batch * seq, width), jnp.bfloat16),
        compiler_params=_params(("parallel", "parallel")),
        name="attn_m",
    )(pb, kv_mem, kv_mem, pz)


def _branch_stages(ids, ins, outs, scratch):
    del ids, scratch
    ua_ref, ub_ref, um_ref, w_ref, ga_ref, gb_ref, gm_ref = ins
    u_ref, = outs
    a_hi = A_WIDTH
    b_hi = A_WIDTH + B_Q_HEADS * HEAD_DIM
    dot = functools.partial(jnp.dot, preferred_element_type=jnp.float32)
    acc = ga_ref[...].astype(jnp.float32) * dot(ua_ref[...], w_ref[:a_hi, :])
    yield
    acc += gb_ref[...].astype(jnp.float32) * dot(ub_ref[...], w_ref[a_hi:b_hi, :])
    yield
    acc += gm_ref[...].astype(jnp.float32) * dot(um_ref[...], w_ref[b_hi:, :])
    u_ref[...] = acc.astype(u_ref.dtype)


def _branch(ua, ub, um, w_branch, gates, tm=512):
    m = ua.shape[0]
    k = w_branch.shape[0]
    n_tiles = D_MODEL // TN
    act = lambda width: pl.BlockSpec((tm, width), lambda j, i: (i, 0))
    gate = pl.BlockSpec((None, tm, TN), lambda j, i: (j, i, 0))
    return dict(
        stages=_branch_stages,
        grid=(n_tiles, m // tm),
        in_specs=[act(ua.shape[1]), act(ub.shape[1]), act(um.shape[1]),
                  pl.BlockSpec((k, TN), lambda j, i: (0, j)),
                  gate, gate, gate],
        args=[ua, ub, um, w_branch, *gates],
        out_specs=[pl.BlockSpec((tm, TN), lambda j, i: (i, j))],
        out_shape=[jax.ShapeDtypeStruct((m, D_MODEL), jnp.bfloat16)],
        name="branch_proj")


def _out_kernel(u_ref, w_ref, x_ref, g_ref, y_ref):
    y = x_ref[...] + jnp.dot(u_ref[...], w_ref[...], preferred_element_type=jnp.float32)
    inv = lax.rsqrt(jnp.mean(y * y, axis=1, keepdims=True) + NORM_EPS)
    y_ref[...] = (y * inv) * g_ref[...]


def _out_proj(u, w_out, x, g_final, tm=256):
    m, k = u.shape
    n = x.shape[1]
    return pl.pallas_call(
        _out_kernel,
        grid=(m // tm,),
        in_specs=[pl.BlockSpec((tm, k), lambda i: (i, 0)),
                  pl.BlockSpec((k, n), lambda i: (0, 0), pipeline_mode=pl.Buffered(1)),
                  pl.BlockSpec((tm, n), lambda i: (i, 0)),
                  pl.BlockSpec((1, n), lambda i: (0, 0))],
        out_specs=pl.BlockSpec((tm, n), lambda i: (i, 0)),
        out_shape=jax.ShapeDtypeStruct((m, n), jnp.float32),
        compiler_params=_params(("arbitrary",), vmem=OUT_VMEM),
        name="out_proj",
    )(u, w_out, x, g_final.reshape(1, n))


def _rope_tables(seq):
    inv_freq = ROPE_THETA ** (-jnp.arange(0, HEAD_DIM, 2, dtype=jnp.float32) / HEAD_DIM)
    ang = jnp.arange(seq, dtype=jnp.float32)[:, None] * inv_freq[None, :]
    cos, sin = jnp.cos(ang), jnp.sin(ang)
    cos, sin = jnp.concatenate([cos, cos], axis=1), jnp.concatenate([-sin, sin], axis=1)
    one, zero = jnp.ones_like(cos), jnp.zeros_like(sin)
    q_scale = HEAD_DIM ** -0.5 * LOG2E
    m_scale = M_HEAD_DIM ** -0.5 * LOG2E
    return (jnp.stack([cos * q_scale, cos, one, one * m_scale]),
            jnp.stack([sin * q_scale, sin, zero, zero]))


def _attend(h, hm, pb, pz, w, tabs, batch, seq, cast_mem=None):
    outs, lses = [], []
    for gi, (_, dil) in enumerate(DIL_CONFIGS):
        sd = seq // dil
        n = batch * dil
        a_sets = ((ROT_Q, ROT_K, ROT_ID),) * 2
        qkv = _proj(h, w["a"][gi], (0, 3), "rot", batch=batch, tm=1024, dil=dil, tabs=tabs, sets=a_sets,
                    cast=cast_mem if gi == 0 else None, name=f"proj_a{gi}")
        if gi == 0 and cast_mem is not None:
            qkv, w_mem = qkv
            w = dict(w, mem=w_mem)
        o, lse = _attn_a(qkv.reshape(3 * n, sd, A_WIDTH), n, sd, (0, n, 2 * n))
        outs.append(o.reshape(batch, dil, sd, A_WIDTH))
        lses.append(lse.reshape(batch, dil, sd, LSE_LANES))
    kv_mem = _proj(hm, w["mem"], (0, 2), "none", batch=1, tm=batch * MEM_LEN, name="proj_mem")
    um = _attn_m(pb, kv_mem.reshape(2, batch * MEM_LEN, TN), pz, batch, seq)
    return outs, lses, um, w["mem"]


def kernel(x_prompt, x_sample, mem_prompt, mem_sample, g_norm, w_in, attn_sink, g_mem, w_mem_kv, w_branch, w_out, g_final):
    bf = jnp.bfloat16
    w_in, w_mem_kv, w_branch, w_out = w_in[0], w_mem_kv[0], w_branch[0], w_out[0]
    trunks = []
    tabs = _rope_tables(max(x_prompt.shape[1], x_sample.shape[1]))
    for x, mem in ((x_prompt, mem_prompt), (x_sample, mem_sample)):
        batch, seq, d = x.shape
        x2 = x.reshape(batch * seq, d)
        trunks.append(dict(
            batch=batch, seq=seq, x2=x2, tabs=tabs,
            hm=_rmsnorm(mem.reshape(batch * MEM_LEN, d), g_mem[0], bf)))

    w_b = _cast_tiles(w_in, B_TILES)
    b_sets = ((ROT_Q, ROT_Q, ROT_K, ROT_M), (ROT_Q, ROT_Q, ROT_ID, ROT_M))
    g_lo = G_TILES[0]
    n_dil = len(DIL_CONFIGS)
    out_tiles = -(-w_out.shape[1] // TN)

    def proj(t, wts, epi, cast, name, **kw):
        out, w_next = _proj(t["h"], wts, (0, 4), epi, batch=t["batch"], tm=1024, cast=cast, name=name, **kw)
        return out.reshape(4, t["batch"] * t["seq"], TN), w_next

    p, s_ = trunks
    s_["h"], pb_s, w_z = _norm_proj(s_["x2"], g_norm[0], w_b, tabs, b_sets, batch=s_["batch"],
                                   cast=(w_in, Z_TILES))
    p["h"], pb_p = _norm_proj(p["x2"], g_norm[0], w_b, tabs, b_sets, batch=p["batch"])
    pz_p, w_ga = proj(p, w_z, "silu", (w_in, (g_lo, g_lo + 4)), "proj_z")
    pz_s, w_gb = proj(s_, w_z, "silu", (w_in, (g_lo + 4, g_lo + 8)), "proj_z")
    ga_p, w_gm = proj(p, w_ga, "sigmoid", (w_in, (g_lo + 8, g_lo + 12)), "proj_ga")
    ga_s, w_a0 = proj(s_, w_ga, "sigmoid", (w_in, (0, A_TILES[1], n_dil)), "proj_ga")
    gb_p, w_a1 = proj(p, w_gb, "sigmoid", (w_in, (1, A_TILES[1], n_dil)), "proj_gb")
    gb_s, w_a2 = proj(s_, w_gb, "sigmoid", (w_in, (2, A_TILES[1], n_dil)), "proj_gb")

    def proj_with_attn_b(t, wts, cast, pb, pz):
        rows = t["batch"] * t["seq"]
        guest = _attn_b(attn_sink[0], pb, pz, t["batch"], t["seq"], step=rows // (rows // 1024 * 4))
        (out, w_next), (ub,) = _proj(t["h"], wts, (0, 4), "sigmoid_parts", batch=t["batch"], tm=1024,
                                     cast=cast, guest=guest, name="proj_gm")
        return out.reshape(4, rows, TN), w_next, ub

    gm_p, w_o, ub_p = proj_with_attn_b(p, w_gm, (w_out, (0, out_tiles)), pb_p, pz_p)
    gm_s, w_br, ub_s = proj_with_attn_b(s_, w_gm, (w_branch, (0, D_MODEL // TN)), pb_s, pz_s)

    w = dict(a=(w_a0, w_a1, w_a2))
    ao_p, al_p, um_p, w_m = _attend(p["h"], p["hm"], pb_p, pz_p, w, tabs, p["batch"], p["seq"],
                                    cast_mem=(w_mem_kv, (0, 2)))
    ao_s, al_s, um_s, _ = _attend(s_["h"], s_["hm"], pb_s, pz_s, dict(w, mem=w_m), tabs, s_["batch"], s_["seq"])
    ua_p, = _launch(_merge_a(ao_p, al_p, pz_p, p["batch"], p["seq"]))
    branch_p = _branch(ua_p, ub_p, um_p, w_br, (ga_p, gb_p, gm_p))
    steps = branch_p["grid"][0] * branch_p["grid"][1]
    (u_p,), (ua_s,) = _launch(branch_p, _merge_a(ao_s, al_s, pz_s, s_["batch"], s_["seq"],
                                                 tm=s_["batch"] * s_["seq"] // steps))
    u_s, = _launch(_branch(ua_s, ub_s, um_s, w_br, (ga_s, gb_s, gm_s)))
    finish = lambda t, u: _out_proj(u, w_o, t["x2"], g_final).reshape(t["batch"], t["seq"], t["x2"].shape[1])
    return finish(p, u_p), finish(s_, u_s)
```

```python
import functools

import jax
import jax.numpy as jnp
from jax import lax
from jax.experimental import pallas as pl
from jax.experimental.pallas import tpu as pltpu

D_MODEL = 4096
HEAD_DIM = 128
ROPE_THETA = 10000.0
NORM_EPS = 1e-6
NEG_INF = -1e30
LOG2E = 1.4426950408889634
LN2 = 0.6931471805599453

DIL_CONFIGS = ((128, 1), (512, 4), (2048, 16))
A_HEADS = 8
A_WIDTH = A_HEADS * HEAD_DIM
A_RADIUS = 64

B_Q_HEADS = 16
B_KV_HEADS = 4
B_GROUP = B_Q_HEADS // B_KV_HEADS
B_RADIUS = 128

MEM_LEN = 256
M_HEADS = 4
M_HEAD_DIM = 256

TN = 1024
A_TILES = (0, 9)
B_TILES = (9, 13)
Z_TILES = (13, 17)
G_TILES = (17, 29)
ROT_Q, ROT_K, ROT_ID, ROT_M = range(4)

VMEM_LIMIT = 56 * 1024 * 1024
RESIDENT_VMEM_LIMIT = 60 * 1024 * 1024


def _params(sem, vmem=VMEM_LIMIT):
    return pltpu.CompilerParams(dimension_semantics=sem, vmem_limit_bytes=vmem)


def _launch(host, guest=None):
    parts = [host] if guest is None else [host, guest]
    grid = host["grid"]
    if guest is not None:
        g0, g1 = guest["grid"]
        assert g0 * g1 == grid[0] * grid[1], (guest["grid"], grid)
        to_guest = lambda i, j: divmod(i * grid[1] + j, g1)
        remap = lambda spec: pl.BlockSpec(
            spec.block_shape, (lambda i, j, f=spec.index_map: f(*to_guest(i, j))),
            memory_space=spec.memory_space) if spec.index_map is not None else spec
        guest = dict(guest, in_specs=[remap(sp) for sp in guest["in_specs"]],
                     out_specs=[remap(sp) for sp in guest["out_specs"]])
        parts = [host, guest]
    n_in = [len(p["in_specs"]) for p in parts]
    n_out = [len(p["out_specs"]) for p in parts]
    n_scr = [len(p.get("scratch", [])) for p in parts]

    def body(*refs):
        refs = list(refs)
        take = lambda n: [refs.pop(0) for _ in range(n)]
        ins = [take(n) for n in n_in]
        outs = [take(n) for n in n_out]
        scr = [take(n) for n in n_scr]
        ids = (pl.program_id(0), pl.program_id(1)) if len(grid) == 2 else (pl.program_id(0),)
        gens = [host["stages"](ids, ins[0], outs[0], scr[0])]
        if guest is not None:
            gens.insert(0, guest["stages"](to_guest(*ids), ins[1], outs[1], scr[1]))
        while gens:
            gens = [g for g in gens if next(g, "done") != "done"]

    res = pl.pallas_call(
        body,
        grid=grid,
        in_specs=[sp for p in parts for sp in p["in_specs"]],
        out_specs=[sp for p in parts for sp in p["out_specs"]],
        out_shape=[sh for p in parts for sh in p["out_shape"]],
        scratch_shapes=[sc for p in parts for sc in p.get("scratch", [])],
        compiler_params=_params(("arbitrary",) * len(grid)),
        name=host["name"] if guest is None else host["name"] + "_" + guest["name"],
    )(*[a for p in parts for a in p["args"]])
    if guest is None:
        return res
    return res[:n_out[0]], res[n_out[0]:]


def _rmsnorm_kernel(x_ref, g_ref, o_ref):
    x = x_ref[...]
    ms = jnp.mean(x * x, axis=-1, keepdims=True)
    o_ref[...] = ((x * lax.rsqrt(ms + NORM_EPS)) * g_ref[...]).astype(o_ref.dtype)


def _rmsnorm(x, g, out_dtype, tm=256):
    m, d = x.shape
    return pl.pallas_call(
        _rmsnorm_kernel,
        grid=(m // tm,),
        in_specs=[pl.BlockSpec((tm, d), lambda i: (i, 0)),
                  pl.BlockSpec((1, d), lambda i: (0, 0))],
        out_specs=pl.BlockSpec((tm, d), lambda i: (i, 0)),
        out_shape=jax.ShapeDtypeStruct((m, d), out_dtype),
        compiler_params=_params(("parallel",)),
        name="rmsnorm",
    )(x, g.reshape(1, d))


def _sigmoid(x):
    return 0.5 * jnp.tanh(0.5 * x) + 0.5


def _proj_stages(ids, ins, outs, scratch, *, epi, dil, cast):
    del ids
    h_ref, w_ref = ins[:2]
    o_ref = outs[0]
    tm = h_ref.shape[0]
    if cast:
        outs[1][...] = ins[-1][...].astype(outs[1].dtype)
    if epi == "sigmoid_parts":
        nq = 4
        wq = TN // nq
        for q in range(nq):
            cols = slice(q * wq, (q + 1) * wq)
            acc = jnp.dot(h_ref[...], w_ref[:, cols], preferred_element_type=jnp.float32)
            o_ref[0, :, cols] = _sigmoid(acc).astype(o_ref.dtype)
            yield
        return
    yield
    acc = jnp.dot(h_ref[...], w_ref[...], preferred_element_type=jnp.float32)
    if epi == "none":
        o_ref[0] = acc.astype(o_ref.dtype)
    elif epi == "silu":
        o_ref[0] = (acc * _sigmoid(acc)).astype(o_ref.dtype)
    elif epi == "sigmoid":
        o_ref[0] = _sigmoid(acc).astype(o_ref.dtype)
    else:
        cos_lo, sin_lo, cos_hi, sin_hi = ins[2:6]
        heads = [slice(hh * HEAD_DIM, (hh + 1) * HEAD_DIM) for hh in range(TN // HEAD_DIM)]
        for hh, sl in enumerate(heads):
            cos_ref, sin_ref = (cos_lo, sin_lo) if hh < len(heads) // 2 else (cos_hi, sin_hi)
            x = acc[:, sl]
            x = x * cos_ref[...] + pltpu.roll(x, HEAD_DIM // 2, 1) * sin_ref[...]
            if dil > 1:
                o_ref[:, :, sl] = jnp.swapaxes(x.reshape(tm // dil, dil, HEAD_DIM), 0, 1).astype(o_ref.dtype)
            else:
                o_ref[0, :, sl] = x.astype(o_ref.dtype)


def _norm_proj_kernel(x_ref, g_ref, w_ref, cos_ref, sin_ref, *rest, sets):
    if len(rest) == 4:
        src_ref, h_ref, o_ref, dst_ref = rest
        dst_ref[...] = src_ref[...].astype(dst_ref.dtype)
    else:
        h_ref, o_ref = rest
    x = x_ref[...]
    h = ((x * lax.rsqrt(jnp.mean(x * x, axis=-1, keepdims=True) + NORM_EPS)) * g_ref[...]).astype(h_ref.dtype)
    h_ref[...] = h
    heads_per_tile = TN // HEAD_DIM
    for t in range(o_ref.shape[0]):
        acc = jnp.dot(h, w_ref[:, t * TN:(t + 1) * TN], preferred_element_type=jnp.float32)
        for hh in range(heads_per_tile):
            sl = slice(hh * HEAD_DIM, (hh + 1) * HEAD_DIM)
            tab = sets[hh * 2 // heads_per_tile][t]
            xh = acc[:, sl]
            o_ref[t, :, sl] = (xh * cos_ref[tab] + pltpu.roll(xh, HEAD_DIM // 2, 1) * sin_ref[tab]).astype(o_ref.dtype)


def _norm_proj(x, g, w, tabs, sets, *, batch, tm=256, cast=None):
    m, d = x.shape
    seq = m // batch
    bps = seq // tm
    nt = w.shape[1] // TN
    n_sets = tabs[0].shape[0]
    tab_spec = pl.BlockSpec((n_sets, tm, HEAD_DIM), lambda i: (0, i % bps, 0))
    in_specs = [pl.BlockSpec((tm, d), lambda i: (i, 0)),
                pl.BlockSpec((1, d), lambda i: (0, 0)),
                pl.BlockSpec(w.shape, lambda i: (0, 0), pipeline_mode=pl.Buffered(1)),
                tab_spec, tab_spec]
    out_specs = [pl.BlockSpec((tm, d), lambda i: (i, 0)),
                 pl.BlockSpec((nt, tm, TN), lambda i: (0, i, 0))]
    out_shape = [jax.ShapeDtypeStruct((m, d), jnp.bfloat16),
                 jax.ShapeDtypeStruct((nt, m, TN), jnp.bfloat16)]
    args = [x, g.reshape(1, d), w, *tabs]
    if cast is not None:
        src, (lo, hi) = cast
        steps = m // tm
        row_blocks = steps // (hi - lo)
        cr = src.shape[0] // row_blocks
        assert row_blocks * (hi - lo) == steps and cr * row_blocks == src.shape[0] and cr % 16 == 0
        in_specs.append(pl.BlockSpec((cr, TN), lambda i: (i % row_blocks, lo + i // row_blocks)))
        out_specs.append(pl.BlockSpec((cr, TN), lambda i: (i % row_blocks, i // row_blocks)))
        out_shape.append(jax.ShapeDtypeStruct((src.shape[0], (hi - lo) * TN), jnp.bfloat16))
        args.append(src)
    return pl.pallas_call(
        functools.partial(_norm_proj_kernel, sets=sets),
        grid=(m // tm,),
        in_specs=in_specs,
        out_specs=out_specs,
        out_shape=out_shape,
        compiler_params=_params(("arbitrary",), vmem=RESIDENT_VMEM_LIMIT),
        name="norm_proj_b",
    )(*args)


def _cast_kernel(src_ref, dst_ref):
    dst_ref[...] = src_ref[...].astype(dst_ref.dtype)


def _cast_tiles(src, tiles):
    lo, hi = tiles
    rows = src.shape[0]
    cr = min(CAST_ROWS, rows)
    return pl.pallas_call(
        _cast_kernel,
        grid=(hi - lo, rows // cr),
        in_specs=[pl.BlockSpec((cr, TN), lambda j, i: (i, lo + j))],
        out_specs=pl.BlockSpec((cr, TN), lambda j, i: (i, j)),
        out_shape=jax.ShapeDtypeStruct((rows, (hi - lo) * TN), jnp.bfloat16),
        compiler_params=_params(("parallel", "parallel")),
        name="cast_tiles",
    )(src)


def _select(j, values):
    if len(set(values)) == 1:
        return values[0]
    out = values[-1]
    for t in range(len(values) - 2, -1, -1):
        out = jnp.where(j == t, values[t], out)
    return out


CAST_ROWS = 512


def _proj(h, w, tiles, epi, *, batch, tm, dil=1, tabs=None, sets=None, cast=None, guest=None, name):
    m, k = h.shape
    seq = m // batch
    col_tiles = range(*tiles)
    lo, step, nt = col_tiles.start, col_tiles.step, len(col_tiles)
    assert seq % tm == 0 and tm % dil == 0 and (dil == 1 or epi == "rot")
    bps = seq // tm
    in_specs = [pl.BlockSpec((tm, k), lambda i, j: (i, 0)),
                pl.BlockSpec((k, TN), lambda i, j: (0, lo + step * j))]
    args = [h, w]
    if epi == "rot":
        for half in sets:
            assert len(half) == nt
            spec = pl.BlockSpec((None, tm, HEAD_DIM),
                                lambda i, j, half=half: (_select(j, half), i % bps, 0))
            in_specs += [spec, spec]
            args += list(tabs)
    out_specs = [pl.BlockSpec((None, None, dil, tm // dil, TN), lambda i, j: (j, i // bps, 0, i % bps, 0))]
    out_shape = [jax.ShapeDtypeStruct((nt, batch, dil, seq // dil, TN), jnp.bfloat16)]
    if cast is not None:
        src, src_tiles = cast
        src_tiles = range(*src_tiles)
        cr = min(CAST_ROWS, src.shape[0])
        cw = min(TN, src.shape[1])
        row_blocks = src.shape[0] // cr
        n_chunks = row_blocks * len(src_tiles)
        assert n_chunks <= (m // tm) * nt, "not enough grid steps to cast this weight group"
        chunk = lambda i, j: jnp.minimum(i * nt + j, n_chunks - 1)
        in_specs.append(pl.BlockSpec(
            (cr, cw), lambda i, j: (chunk(i, j) % row_blocks,
                                    src_tiles.start + src_tiles.step * (chunk(i, j) // row_blocks))))
        args.append(src)
        out_specs.append(pl.BlockSpec(
            (cr, cw), lambda i, j: (chunk(i, j) % row_blocks, chunk(i, j) // row_blocks)))
        out_shape.append(jax.ShapeDtypeStruct((src.shape[0], len(src_tiles) * cw), jnp.bfloat16))
    host = dict(
        stages=functools.partial(_proj_stages, epi=epi, dil=dil, cast=cast is not None),
        grid=(m // tm, nt), in_specs=in_specs, args=args, out_specs=out_specs, out_shape=out_shape,
        name=name)
    if guest is not None:
        outs, guest_outs = _launch(host, guest)
        return (outs if cast is not None else outs[0]), guest_outs
    outs = _launch(host)
    return outs if cast is not None else outs[0]


A_TQ = 128
A_STEP = 512
A_HALO = 64
LSE_LANES = 128


def _window_rows(prev_ref, cur_ref, next_ref, lo, hi, cols):
    n = cur_ref.shape[0]
    parts = []
    if lo < 0:
        parts.append(prev_ref[:, cols])
    parts.append(cur_ref[max(lo, 0):min(hi, n), cols])
    if hi > n:
        parts.append(next_ref[:, cols])
    return jnp.concatenate(parts, axis=0) if len(parts) > 1 else parts[0]


def _band_bias(tq, halo, radius, start, seq):
    nk = tq + 2 * halo
    r = lax.broadcasted_iota(jnp.int32, (tq, nk), 0)
    c = lax.broadcasted_iota(jnp.int32, (tq, nk), 1)
    kpos = start - halo + c
    valid = (jnp.abs(c - halo - r) <= radius) & (kpos >= 0) & (kpos < seq)
    return jnp.where(valid, 0.0, NEG_INF)


def _attn_a_kernel(q_ref, kp_ref, kc_ref, kn_ref, vp_ref, vc_ref, vn_ref, o_ref, lse_ref, *, seq):
    step = q_ref.shape[0]
    lane = lax.broadcasted_iota(jnp.int32, (A_TQ, LSE_LANES), 1)
    for b in range(step // A_TQ):
        rows = slice(b * A_TQ, (b + 1) * A_TQ)
        lo, hi = b * A_TQ - A_HALO, (b + 1) * A_TQ + A_HALO
        bias = _band_bias(A_TQ, A_HALO, A_RADIUS, pl.program_id(1) * step + b * A_TQ, seq)
        lse_all = jnp.zeros((A_TQ, LSE_LANES), jnp.float32)
        for hh in range(A_HEADS):
            sl = slice(hh * HEAD_DIM, (hh + 1) * HEAD_DIM)
            k = _window_rows(kp_ref, kc_ref, kn_ref, lo, hi, sl)
            v = _window_rows(vp_ref, vc_ref, vn_ref, lo, hi, sl)
            s = lax.dot_general(q_ref[rows, sl], k, (((1,), (1,)), ((), ())),
                                preferred_element_type=jnp.float32) + bias
            m = jnp.max(s, axis=1, keepdims=True)
            e = jnp.exp2(s - m)
            den = jnp.sum(e, axis=1, keepdims=True)
            o = jnp.dot(e.astype(jnp.bfloat16), v, preferred_element_type=jnp.float32)
            o_ref[rows, sl] = (o * (1.0 / den)).astype(o_ref.dtype)
            lse_all = jnp.where(lane == hh, m * LN2 + jnp.log(den), lse_all)
        lse_ref[rows, :] = lse_all


def _attn_a(qkv, n, seq, offs):
    qo, ko, vo = offs
    step = min(A_STEP, seq)
    nh = seq // A_HALO
    ratio = step // A_HALO
    cur = lambda off: pl.BlockSpec((None, step, A_WIDTH), lambda s, i: (off + s, i, 0))
    prev = lambda off: pl.BlockSpec(
        (None, A_HALO, A_WIDTH), lambda s, i: (off + s, jnp.maximum(ratio * i - 1, 0), 0))
    nxt = lambda off: pl.BlockSpec(
        (None, A_HALO, A_WIDTH), lambda s, i: (off + s, jnp.minimum(ratio * (i + 1), nh - 1), 0))
    return pl.pallas_call(
        functools.partial(_attn_a_kernel, seq=seq),
        grid=(n, seq // step),
        in_specs=[cur(qo), prev(ko), cur(ko), nxt(ko), prev(vo), cur(vo), nxt(vo)],
        out_specs=[pl.BlockSpec((None, step, A_WIDTH), lambda s, i: (s, i, 0)),
                   pl.BlockSpec((None, step, LSE_LANES), lambda s, i: (s, i, 0))],
        out_shape=[jax.ShapeDtypeStruct((n, seq, A_WIDTH), jnp.bfloat16),
                   jax.ShapeDtypeStruct((n, seq, LSE_LANES), jnp.float32)],
        compiler_params=_params(("parallel", "parallel")),
        name="attn_a",
    )(qkv, qkv, qkv, qkv, qkv, qkv, qkv)


def _merge_a_stages(ids, ins, outs, scratch):
    del ids
    c0_ref, c1_ref, c2_ref, cl0_ref, cl1_ref, cl2_ref, z_ref = ins
    u_ref, = outs
    o1_ref, o2_ref, l1_ref, l2_ref = scratch
    tm = u_ref.shape[0]
    o0_ref, l0_ref = c0_ref.at[0], cl0_ref.at[0]
    for c_ref, cl_ref, o_ref, l_ref in ((c1_ref, cl1_ref, o1_ref, l1_ref), (c2_ref, cl2_ref, o2_ref, l2_ref)):
        dil = c_ref.shape[0]
        for r in range(dil):
            rows = pl.ds(r, tm // dil, stride=dil)
            for hh in range(A_HEADS):
                o_ref[hh, rows, :] = c_ref[r, :, hh * HEAD_DIM:(hh + 1) * HEAD_DIM].astype(jnp.float32)
            l_ref[rows, :] = cl_ref[r]
        yield
    l0, l1, l2 = l0_ref[...], l1_ref[...], l2_ref[...]
    m = jnp.maximum(jnp.maximum(l0, l1), l2)
    e0, e1, e2 = jnp.exp(l0 - m), jnp.exp(l1 - m), jnp.exp(l2 - m)
    inv = 1.0 / (e0 + e1 + e2)
    a0, a1, a2 = e0 * inv, e1 * inv, e2 * inv
    for hh in range(A_HEADS):
        sl = slice(hh * HEAD_DIM, (hh + 1) * HEAD_DIM)
        col = slice(hh, hh + 1)
        o = a0[:, col] * o0_ref[:, sl] + a1[:, col] * o1_ref[hh] + a2[:, col] * o2_ref[hh]
        u_ref[:, sl] = (o * z_ref[:, sl].astype(jnp.float32)).astype(u_ref.dtype)
        if hh == A_HEADS // 2 - 1:
            yield


def _merge_a(outs, lses, pz, batch, seq, tm=512):
    bps = seq // tm
    cls = lambda dil, width: pl.BlockSpec((None, dil, tm // dil, width),
                                          lambda i, _: (i // bps, 0, i % bps, 0))
    dils = [d for _, d in DIL_CONFIGS]
    return dict(
        stages=_merge_a_stages,
        grid=(batch * bps, 1),
        in_specs=([cls(d, A_WIDTH) for d in dils] + [cls(d, LSE_LANES) for d in dils]
                  + [pl.BlockSpec((None, tm, TN), lambda i, _: (0, i, 0))]),
        args=[*outs, *lses, pz],
        out_specs=[pl.BlockSpec((tm, A_WIDTH), lambda i, _: (i, 0))],
        out_shape=[jax.ShapeDtypeStruct((batch * seq, A_WIDTH), jnp.bfloat16)],
        scratch=[pltpu.VMEM((A_HEADS, tm, HEAD_DIM), jnp.float32)] * 2
                + [pltpu.VMEM((tm, LSE_LANES), jnp.float32)] * 2,
        name="merge_a")


B_TQ = 128


def _attn_b_stages(ids, ins, outs, scratch, *, seq):
    sink_ref, q0_ref, q1_ref, kvp_ref, kvc_ref, kvn_ref, z0_ref, z1_ref = ins
    u_ref, = outs
    step = u_ref.shape[0]
    ones = jnp.ones((3 * B_TQ, HEAD_DIM), jnp.bfloat16)
    kv_half = B_KV_HEADS * HEAD_DIM
    units = [(b, kh) for b in range(step // B_TQ) for kh in range(B_KV_HEADS)]

    def scores(b, kh):
        rows = slice(b * B_TQ, (b + 1) * B_TQ)
        lo, hi = (b - 1) * B_TQ, (b + 2) * B_TQ
        q_ref = q0_ref if kh < 2 else q1_ref
        base = (kh % 2) * B_GROUP * HEAD_DIM
        heads = [slice(base + g * HEAD_DIM, base + (g + 1) * HEAD_DIM) for g in range(B_GROUP)]
        k = _window_rows(kvp_ref, kvc_ref, kvn_ref, lo, hi, slice(kh * HEAD_DIM, (kh + 1) * HEAD_DIM))
        q = jnp.concatenate([q_ref[rows, sl] for sl in heads], axis=0)
        return lax.dot_general(q, k, (((1,), (1,)), ((), ())), preferred_element_type=jnp.float32)

    def finish(b, kh, s):
        rows = slice(b * B_TQ, (b + 1) * B_TQ)
        lo, hi = (b - 1) * B_TQ, (b + 2) * B_TQ
        z_ref = z0_ref if kh < 2 else z1_ref
        base = (kh % 2) * B_GROUP * HEAD_DIM
        heads = [slice(base + g * HEAD_DIM, base + (g + 1) * HEAD_DIM) for g in range(B_GROUP)]
        bias = _band_bias(B_TQ, B_TQ, B_RADIUS, ids[1] * step + b * B_TQ, seq)
        v = _window_rows(kvp_ref, kvc_ref, kvn_ref, lo, hi,
                         slice(kv_half + kh * HEAD_DIM, kv_half + (kh + 1) * HEAD_DIM))
        es, ms, sks = [], [], []
        for g in range(B_GROUP):
            sg = s[g * B_TQ:(g + 1) * B_TQ] + bias
            sk = sink_ref[kh * B_GROUP + g] * LOG2E
            m = jnp.maximum(jnp.max(sg, axis=1, keepdims=True), sk)
            es.append(jnp.exp2(sg - m).astype(jnp.bfloat16))
            ms.append(m)
            sks.append(sk)
        ov = jnp.dot(jnp.concatenate(es, axis=0), jnp.concatenate([v, ones], axis=1),
                     preferred_element_type=jnp.float32)
        for g in range(B_GROUP):
            og = ov[g * B_TQ:(g + 1) * B_TQ]
            den = og[:, HEAD_DIM:HEAD_DIM + 1] + jnp.exp2(sks[g] - ms[g])
            col = (kh * B_GROUP + g) * HEAD_DIM
            u_ref[rows, col:col + HEAD_DIM] = (
                og[:, :HEAD_DIM] * (1.0 / den) * z_ref[rows, heads[g]].astype(jnp.float32)
            ).astype(u_ref.dtype)

    per_round = 4
    pending = [scores(*u) for u in units[:per_round]]
    yield
    for r in range(0, len(units), per_round):
        for u, sc in zip(units[r:r + per_round], pending):
            finish(*u, sc)
        pending = [scores(*u) for u in units[r + per_round:r + 2 * per_round]]
        yield


def _attn_b(sink, pb, pz, batch, seq, step):
    nq = seq // step
    nh = seq // B_TQ
    ratio = step // B_TQ
    row = lambda slot: pl.BlockSpec((None, step, TN), lambda b, i: (slot, b * nq + i, 0))
    kv_prev = pl.BlockSpec((None, B_TQ, TN), lambda b, i: (2, b * nh + jnp.maximum(ratio * i - 1, 0), 0))
    kv_next = pl.BlockSpec((None, B_TQ, TN),
                           lambda b, i: (2, b * nh + jnp.minimum(ratio * (i + 1), nh - 1), 0))
    width = B_Q_HEADS * HEAD_DIM
    return dict(
        stages=functools.partial(_attn_b_stages, seq=seq),
        grid=(batch, nq),
        in_specs=[pl.BlockSpec(memory_space=pltpu.SMEM),
                  row(0), row(1), kv_prev, row(2), kv_next, row(1), row(2)],
        args=[sink, pb, pb, pb, pb, pb, pz, pz],
        out_specs=[pl.BlockSpec((step, width), lambda b, i: (b * nq + i, 0))],
        out_shape=[jax.ShapeDtypeStruct((batch * seq, width), jnp.bfloat16)],
        name="attn_b")


M_TQ = 512


def _attn_m_kernel(q_ref, k_ref, v_ref, z_ref, u_ref):
    for hh in range(M_HEADS):
        sl = slice(hh * M_HEAD_DIM, (hh + 1) * M_HEAD_DIM)
        s = lax.dot_general(q_ref[:, sl], k_ref[:, sl], (((1,), (1,)), ((), ())),
                            preferred_element_type=jnp.float32)
        m = jnp.max(s, axis=1, keepdims=True)
        e = jnp.exp2(s - m)
        den = jnp.sum(e, axis=1, keepdims=True)
        o = jnp.dot(e.astype(jnp.bfloat16), v_ref[:, sl], preferred_element_type=jnp.float32) * (1.0 / den)
        u_ref[:, sl] = (o * z_ref[:, sl].astype(jnp.float32)).astype(u_ref.dtype)


def _attn_m(pb, kv_mem, pz, batch, seq):
    nq = seq // M_TQ
    width = M_HEADS * M_HEAD_DIM
    return pl.pallas_call(
        _attn_m_kernel,
        grid=(batch, nq),
        in_specs=[pl.BlockSpec((None, M_TQ, TN), lambda b, i: (3, b * nq + i, 0)),
                  pl.BlockSpec((None, MEM_LEN, TN), lambda b, i: (0, b, 0)),
                  pl.BlockSpec((None, MEM_LEN, TN), lambda b, i: (1, b, 0)),
                  pl.BlockSpec((None, M_TQ, TN), lambda b, i: (3, b * nq + i, 0))],
        out_specs=pl.BlockSpec((M_TQ, width), lambda b, i: (b * nq + i, 0)),
        out_shape=jax.ShapeDtypeStruct((batch * seq, width), jnp.bfloat16),
        compiler_params=_params(("parallel", "parallel")),
        name="attn_m",
    )(pb, kv_mem, kv_mem, pz)


def _branch_stages(ids, ins, outs, scratch):
    del ids, scratch
    ua_ref, ub_ref, um_ref, w_ref, ga_ref, gb_ref, gm_ref = ins
    u_ref, = outs
    a_hi = A_WIDTH
    b_hi = A_WIDTH + B_Q_HEADS * HEAD_DIM
    dot = functools.partial(jnp.dot, preferred_element_type=jnp.float32)
    acc = ga_ref[...].astype(jnp.float32) * dot(ua_ref[...], w_ref[:a_hi, :])
    yield
    acc += gb_ref[...].astype(jnp.float32) * dot(ub_ref[...], w_ref[a_hi:b_hi, :])
    yield
    acc += gm_ref[...].astype(jnp.float32) * dot(um_ref[...], w_ref[b_hi:, :])
    u_ref[...] = acc.astype(u_ref.dtype)


def _branch(ua, ub, um, w_branch, gates, tm=512):
    m = ua.shape[0]
    k = w_branch.shape[0]
    n_tiles = D_MODEL // TN
    act = lambda width: pl.BlockSpec((tm, width), lambda j, i: (i, 0))
    gate = pl.BlockSpec((None, tm, TN), lambda j, i: (j, i, 0))
    return dict(
        stages=_branch_stages,
        grid=(n_tiles, m // tm),
        in_specs=[act(ua.shape[1]), act(ub.shape[1]), act(um.shape[1]),
                  pl.BlockSpec((k, TN), lambda j, i: (0, j)),
                  gate, gate, gate],
        args=[ua, ub, um, w_branch, *gates],
        out_specs=[pl.BlockSpec((tm, TN), lambda j, i: (i, j))],
        out_shape=[jax.ShapeDtypeStruct((m, D_MODEL), jnp.bfloat16)],
        name="branch_proj")


def _out_kernel(u_ref, w_ref, x_ref, g_ref, y_ref):
    y = x_ref[...] + jnp.dot(u_ref[...], w_ref[...], preferred_element_type=jnp.float32)
    inv = lax.rsqrt(jnp.mean(y * y, axis=1, keepdims=True) + NORM_EPS)
    y_ref[...] = (y * inv) * g_ref[...]


def _out_proj(u, w_out, x, g_final, tm=256):
    m, k = u.shape
    n = x.shape[1]
    return pl.pallas_call(
        _out_kernel,
        grid=(m // tm,),
        in_specs=[pl.BlockSpec((tm, k), lambda i: (i, 0)),
                  pl.BlockSpec((k, n), lambda i: (0, 0), pipeline_mode=pl.Buffered(1)),
                  pl.BlockSpec((tm, n), lambda i: (i, 0)),
                  pl.BlockSpec((1, n), lambda i: (0, 0))],
        out_specs=pl.BlockSpec((tm, n), lambda i: (i, 0)),
        out_shape=jax.ShapeDtypeStruct((m, n), jnp.float32),
        compiler_params=_params(("arbitrary",), vmem=RESIDENT_VMEM_LIMIT),
        name="out_proj",
    )(u, w_out, x, g_final.reshape(1, n))


def _rope_tables(seq):
    inv_freq = ROPE_THETA ** (-jnp.arange(0, HEAD_DIM, 2, dtype=jnp.float32) / HEAD_DIM)
    ang = jnp.arange(seq, dtype=jnp.float32)[:, None] * inv_freq[None, :]
    cos, sin = jnp.cos(ang), jnp.sin(ang)
    cos, sin = jnp.concatenate([cos, cos], axis=1), jnp.concatenate([-sin, sin], axis=1)
    one, zero = jnp.ones_like(cos), jnp.zeros_like(sin)
    q_scale = HEAD_DIM ** -0.5 * LOG2E
    m_scale = M_HEAD_DIM ** -0.5 * LOG2E
    return (jnp.stack([cos * q_scale, cos, one, one * m_scale]),
            jnp.stack([sin * q_scale, sin, zero, zero]))


def _attend(h, hm, pb, pz, w, tabs, batch, seq, cast_mem=None):
    outs, lses = [], []
    for gi, (_, dil) in enumerate(DIL_CONFIGS):
        sd = seq // dil
        n = batch * dil
        a_sets = ((ROT_Q, ROT_K, ROT_ID),) * 2
        qkv = _proj(h, w["a"][gi], (0, 3), "rot", batch=batch, tm=1024, dil=dil, tabs=tabs, sets=a_sets,
                    cast=cast_mem if gi == 0 else None, name=f"proj_a{gi}")
        if gi == 0 and cast_mem is not None:
            qkv, w_mem = qkv
            w = dict(w, mem=w_mem)
        o, lse = _attn_a(qkv.reshape(3 * n, sd, A_WIDTH), n, sd, (0, n, 2 * n))
        outs.append(o.reshape(batch, dil, sd, A_WIDTH))
        lses.append(lse.reshape(batch, dil, sd, LSE_LANES))
    kv_mem = _proj(hm, w["mem"], (0, 2), "none", batch=1, tm=batch * MEM_LEN, name="proj_mem")
    um = _attn_m(pb, kv_mem.reshape(2, batch * MEM_LEN, TN), pz, batch, seq)
    return outs, lses, um, w["mem"]


def kernel(x_prompt, x_sample, mem_prompt, mem_sample, g_norm, w_in, attn_sink, g_mem, w_mem_kv, w_branch, w_out, g_final):
    bf = jnp.bfloat16
    w_in, w_mem_kv, w_branch, w_out = w_in[0], w_mem_kv[0], w_branch[0], w_out[0]
    trunks = []
    tabs = _rope_tables(max(x_prompt.shape[1], x_sample.shape[1]))
    for x, mem in ((x_prompt, mem_prompt), (x_sample, mem_sample)):
        batch, seq, d = x.shape
        x2 = x.reshape(batch * seq, d)
        trunks.append(dict(
            batch=batch, seq=seq, x2=x2,
            hm=_rmsnorm(mem.reshape(batch * MEM_LEN, d), g_mem[0], bf)))

    w_b = _cast_tiles(w_in, B_TILES)
    b_sets = ((ROT_Q, ROT_Q, ROT_K, ROT_M), (ROT_Q, ROT_Q, ROT_ID, ROT_M))
    g_lo = G_TILES[0]
    n_dil = len(DIL_CONFIGS)
    out_tiles = -(-w_out.shape[1] // TN)

    def proj(t, wts, epi, cast, name):
        out, w_next = _proj(t["h"], wts, (0, 4), epi, batch=t["batch"], tm=1024, cast=cast, name=name)
        return out.reshape(4, t["batch"] * t["seq"], TN), w_next

    p, s_ = trunks
    s_["h"], pb_s, w_z = _norm_proj(s_["x2"], g_norm[0], w_b, tabs, b_sets, batch=s_["batch"],
                                   cast=(w_in, Z_TILES))
    p["h"], pb_p = _norm_proj(p["x2"], g_norm[0], w_b, tabs, b_sets, batch=p["batch"])
    pz_p, w_ga = proj(p, w_z, "silu", (w_in, (g_lo, g_lo + 4)), "proj_z")
    pz_s, w_gb = proj(s_, w_z, "silu", (w_in, (g_lo + 4, g_lo + 8)), "proj_z")
    ga_p, w_gm = proj(p, w_ga, "sigmoid", (w_in, (g_lo + 8, g_lo + 12)), "proj_ga")
    ga_s, w_a0 = proj(s_, w_ga, "sigmoid", (w_in, (0, A_TILES[1], n_dil)), "proj_ga")
    gb_p, w_a1 = proj(p, w_gb, "sigmoid", (w_in, (1, A_TILES[1], n_dil)), "proj_gb")
    gb_s, w_a2 = proj(s_, w_gb, "sigmoid", (w_in, (2, A_TILES[1], n_dil)), "proj_gb")

    def proj_with_attn_b(t, wts, cast, pb, pz):
        rows = t["batch"] * t["seq"]
        guest = _attn_b(attn_sink[0], pb, pz, t["batch"], t["seq"], step=rows // (rows // 1024 * 4))
        (out, w_next), (ub,) = _proj(t["h"], wts, (0, 4), "sigmoid_parts", batch=t["batch"], tm=1024,
                                     cast=cast, guest=guest, name="proj_gm")
        return out.reshape(4, rows, TN), w_next, ub

    gm_p, w_o, ub_p = proj_with_attn_b(p, w_gm, (w_out, (0, out_tiles)), pb_p, pz_p)
    gm_s, w_br, ub_s = proj_with_attn_b(s_, w_gm, (w_branch, (0, D_MODEL // TN)), pb_s, pz_s)

    w = dict(a=(w_a0, w_a1, w_a2))
    ao_p, al_p, um_p, w_m = _attend(p["h"], p["hm"], pb_p, pz_p, w, tabs, p["batch"], p["seq"],
                                    cast_mem=(w_mem_kv, (0, 2)))
    ao_s, al_s, um_s, _ = _attend(s_["h"], s_["hm"], pb_s, pz_s, dict(w, mem=w_m), tabs, s_["batch"], s_["seq"])
    ua_p, = _launch(_merge_a(ao_p, al_p, pz_p, p["batch"], p["seq"]))
    branch_p = _branch(ua_p, ub_p, um_p, w_br, (ga_p, gb_p, gm_p))
    steps = branch_p["grid"][0] * branch_p["grid"][1]
    (u_p,), (ua_s,) = _launch(branch_p, _merge_a(ao_s, al_s, pz_s, s_["batch"], s_["seq"],
                                                 tm=s_["batch"] * s_["seq"] // steps))
    u_s, = _launch(_branch(ua_s, ub_s, um_s, w_br, (ga_s, gb_s, gm_s)))
    finish = lambda t, u: _out_proj(u, w_o, t["x2"], g_final).reshape(t["batch"], t["seq"], t["x2"].shape[1])
    return finish(p, u_p), finish(s_, u_s)
```

```python
import functools

import jax
import jax.numpy as jnp
from jax import lax
from jax.experimental import pallas as pl
from jax.experimental.pallas import tpu as pltpu

D_MODEL = 4096
HEAD_DIM = 128
ROPE_THETA = 10000.0
NORM_EPS = 1e-6
NEG_INF = -1e30
LOG2E = 1.4426950408889634
LN2 = 0.6931471805599453

DIL_CONFIGS = ((128, 1), (512, 4), (2048, 16))
A_HEADS = 8
A_WIDTH = A_HEADS * HEAD_DIM
A_RADIUS = 64

B_Q_HEADS = 16
B_KV_HEADS = 4
B_GROUP = B_Q_HEADS // B_KV_HEADS
B_RADIUS = 128

MEM_LEN = 256
M_HEADS = 4
M_HEAD_DIM = 256

TN = 1024
A_TILES = (0, 9)
B_TILES = (9, 13)
Z_TILES = (13, 17)
G_TILES = (17, 29)
ROT_Q, ROT_K, ROT_ID, ROT_M = range(4)

VMEM_LIMIT = 56 * 1024 * 1024
RESIDENT_VMEM_LIMIT = 60 * 1024 * 1024
CAST_ROWS = 512
GUEST_STEP = 256


def _params(sem, vmem=VMEM_LIMIT):
    return pltpu.CompilerParams(dimension_semantics=sem, vmem_limit_bytes=vmem)


def _launch(host, guest=None):
    parts = [host] if guest is None else [host, guest]
    grid = host["grid"]
    if guest is not None:
        g0, g1 = guest["grid"]
        assert g0 * g1 == grid[0] * grid[1], (guest["grid"], grid)
        to_guest = lambda i, j: divmod(i * grid[1] + j, g1)
        remap = lambda spec: pl.BlockSpec(
            spec.block_shape, (lambda i, j, f=spec.index_map: f(*to_guest(i, j))),
            memory_space=spec.memory_space) if spec.index_map is not None else spec
        guest = dict(guest, in_specs=[remap(sp) for sp in guest["in_specs"]],
                     out_specs=[remap(sp) for sp in guest["out_specs"]])
        parts = [host, guest]
    n_in = [len(p["in_specs"]) for p in parts]
    n_out = [len(p["out_specs"]) for p in parts]
    n_scr = [len(p.get("scratch", [])) for p in parts]

    def body(*refs):
        refs = list(refs)
        take = lambda n: [refs.pop(0) for _ in range(n)]
        ins = [take(n) for n in n_in]
        outs = [take(n) for n in n_out]
        scr = [take(n) for n in n_scr]
        ids = (pl.program_id(0), pl.program_id(1)) if len(grid) == 2 else (pl.program_id(0),)
        gens = [host["stages"](ids, ins[0], outs[0], scr[0])]
        if guest is not None:
            gens.insert(0, guest["stages"](to_guest(*ids), ins[1], outs[1], scr[1]))
        while gens:
            gens = [g for g in gens if next(g, "done") != "done"]

    res = pl.pallas_call(
        body,
        grid=grid,
        in_specs=[sp for p in parts for sp in p["in_specs"]],
        out_specs=[sp for p in parts for sp in p["out_specs"]],
        out_shape=[sh for p in parts for sh in p["out_shape"]],
        scratch_shapes=[sc for p in parts for sc in p.get("scratch", [])],
        compiler_params=_params(("arbitrary",) * len(grid), vmem=host.get("vmem", VMEM_LIMIT)),
        name=host["name"] if guest is None else host["name"] + "_" + guest["name"],
    )(*[a for p in parts for a in p["args"]])
    if guest is None:
        return res
    return res[:n_out[0]], res[n_out[0]:]


def _rmsnorm_kernel(x_ref, g_ref, o_ref):
    x = x_ref[...]
    ms = jnp.mean(x * x, axis=-1, keepdims=True)
    o_ref[...] = ((x * lax.rsqrt(ms + NORM_EPS)) * g_ref[...]).astype(o_ref.dtype)


def _rmsnorm(x, g, out_dtype, tm=256):
    m, d = x.shape
    return pl.pallas_call(
        _rmsnorm_kernel,
        grid=(m // tm,),
        in_specs=[pl.BlockSpec((tm, d), lambda i: (i, 0)),
                  pl.BlockSpec((1, d), lambda i: (0, 0))],
        out_specs=pl.BlockSpec((tm, d), lambda i: (i, 0)),
        out_shape=jax.ShapeDtypeStruct((m, d), out_dtype),
        compiler_params=_params(("parallel",)),
        name="rmsnorm",
    )(x, g.reshape(1, d))


def _cast_specs(src, src_tiles, steps, step_of, rows):
    src_tiles = range(*src_tiles)
    cr = min(rows, src.shape[0])
    cw = min(TN, src.shape[1])
    row_blocks = src.shape[0] // cr
    n_chunks = row_blocks * len(src_tiles)
    assert n_chunks <= steps, "not enough grid steps to cast this weight group"
    chunk = lambda *ids: jnp.minimum(step_of(*ids), n_chunks - 1)
    spec_in = pl.BlockSpec(
        (cr, cw), lambda *ids: (chunk(*ids) % row_blocks,
                                src_tiles.start + src_tiles.step * (chunk(*ids) // row_blocks)))
    spec_out = pl.BlockSpec((cr, cw), lambda *ids: (chunk(*ids) % row_blocks, chunk(*ids) // row_blocks))
    return spec_in, spec_out, jax.ShapeDtypeStruct((src.shape[0], len(src_tiles) * cw), jnp.bfloat16)


def _cast_kernel(src_ref, dst_ref):
    dst_ref[...] = src_ref[...].astype(dst_ref.dtype)


def _cast_tiles(src, tiles):
    lo, hi = tiles
    rows = src.shape[0]
    cr = min(CAST_ROWS, rows)
    return pl.pallas_call(
        _cast_kernel,
        grid=(hi - lo, rows // cr),
        in_specs=[pl.BlockSpec((cr, TN), lambda j, i: (i, lo + j))],
        out_specs=pl.BlockSpec((cr, TN), lambda j, i: (i, j)),
        out_shape=jax.ShapeDtypeStruct((rows, (hi - lo) * TN), jnp.bfloat16),
        compiler_params=_params(("parallel", "parallel")),
        name="cast_tiles",
    )(src)


def _sigmoid(x):
    return 0.5 * jnp.tanh(0.5 * x) + 0.5


def _proj_stages(ids, ins, outs, scratch, *, epi, dil, cast):
    del ids
    h_ref, w_ref = ins[:2]
    o_ref = outs[0]
    tm = h_ref.shape[0]
    if cast:
        outs[1][...] = ins[-1][...].astype(outs[1].dtype)
    yield
    acc = jnp.dot(h_ref[...], w_ref[...], preferred_element_type=jnp.float32)
    if epi == "none":
        o_ref[0] = acc.astype(o_ref.dtype)
    elif epi == "silu":
        o_ref[0] = (acc * _sigmoid(acc)).astype(o_ref.dtype)
    elif epi == "sigmoid":
        o_ref[0] = _sigmoid(acc).astype(o_ref.dtype)
    else:
        cos_lo, sin_lo, cos_hi, sin_hi = ins[2:6]
        heads = [slice(hh * HEAD_DIM, (hh + 1) * HEAD_DIM) for hh in range(TN // HEAD_DIM)]
        for hh, sl in enumerate(heads):
            cos_ref, sin_ref = (cos_lo, sin_lo) if hh < len(heads) // 2 else (cos_hi, sin_hi)
            x = acc[:, sl]
            x = x * cos_ref[...] + pltpu.roll(x, HEAD_DIM // 2, 1) * sin_ref[...]
            if dil > 1:
                o_ref[:, :, sl] = jnp.swapaxes(x.reshape(tm // dil, dil, HEAD_DIM), 0, 1).astype(o_ref.dtype)
            else:
                o_ref[0, :, sl] = x.astype(o_ref.dtype)


def _select(j, values):
    if len(set(values)) == 1:
        return values[0]
    out = values[-1]
    for t in range(len(values) - 2, -1, -1):
        out = jnp.where(j == t, values[t], out)
    return out


def _proj(h, w, tiles, epi, *, batch, tm, dil=1, tabs=None, sets=None, cast=None, name):
    m, k = h.shape
    seq = m // batch
    col_tiles = range(*tiles)
    lo, step, nt = col_tiles.start, col_tiles.step, len(col_tiles)
    assert seq % tm == 0 and tm % dil == 0 and (dil == 1 or epi == "rot")
    bps = seq // tm
    in_specs = [pl.BlockSpec((tm, k), lambda i, j: (i, 0)),
                pl.BlockSpec((k, TN), lambda i, j: (0, lo + step * j))]
    args = [h, w]
    if epi == "rot":
        for half in sets:
            assert len(half) == nt
            spec = pl.BlockSpec((None, tm, HEAD_DIM),
                                lambda i, j, half=half: (_select(j, half), i % bps, 0))
            in_specs += [spec, spec]
            args += list(tabs)
    out_specs = [pl.BlockSpec((None, None, dil, tm // dil, TN), lambda i, j: (j, i // bps, 0, i % bps, 0))]
    out_shape = [jax.ShapeDtypeStruct((nt, batch, dil, seq // dil, TN), jnp.bfloat16)]
    if cast is not None:
        spec_in, spec_out, shape = _cast_specs(*cast, (m // tm) * nt, lambda i, j: i * nt + j, CAST_ROWS)
        in_specs.append(spec_in)
        out_specs.append(spec_out)
        out_shape.append(shape)
        args.append(cast[0])
    host = dict(
        stages=functools.partial(_proj_stages, epi=epi, dil=dil, cast=cast is not None),
        grid=(m // tm, nt), in_specs=in_specs, args=args, out_specs=out_specs, out_shape=out_shape,
        name=name)
    outs = _launch(host)
    return outs if cast is not None else outs[0]


def _norm_proj_kernel(x_ref, g_ref, w_ref, cos_ref, sin_ref, *rest, sets):
    if len(rest) == 4:
        src_ref, h_ref, o_ref, dst_ref = rest
        dst_ref[...] = src_ref[...].astype(dst_ref.dtype)
    else:
        h_ref, o_ref = rest
    x = x_ref[...]
    h = ((x * lax.rsqrt(jnp.mean(x * x, axis=-1, keepdims=True) + NORM_EPS)) * g_ref[...]).astype(h_ref.dtype)
    h_ref[...] = h
    heads_per_tile = TN // HEAD_DIM
    for t in range(o_ref.shape[0]):
        acc = jnp.dot(h, w_ref[:, t * TN:(t + 1) * TN], preferred_element_type=jnp.float32)
        for hh in range(heads_per_tile):
            sl = slice(hh * HEAD_DIM, (hh + 1) * HEAD_DIM)
            tab = sets[hh * 2 // heads_per_tile][t]
            xh = acc[:, sl]
            o_ref[t, :, sl] = (xh * cos_ref[tab] + pltpu.roll(xh, HEAD_DIM // 2, 1) * sin_ref[tab]).astype(o_ref.dtype)


def _norm_proj(x, g, w, tabs, sets, *, batch, tm=256, cast=None):
    m, d = x.shape
    seq = m // batch
    bps = seq // tm
    nt = w.shape[1] // TN
    n_sets = tabs[0].shape[0]
    tab_spec = pl.BlockSpec((n_sets, tm, HEAD_DIM), lambda i: (0, i % bps, 0))
    in_specs = [pl.BlockSpec((tm, d), lambda i: (i, 0)),
                pl.BlockSpec((1, d), lambda i: (0, 0)),
                pl.BlockSpec(w.shape, lambda i: (0, 0), pipeline_mode=pl.Buffered(1)),
                tab_spec, tab_spec]
    out_specs = [pl.BlockSpec((tm, d), lambda i: (i, 0)),
                 pl.BlockSpec((nt, tm, TN), lambda i: (0, i, 0))]
    out_shape = [jax.ShapeDtypeStruct((m, d), jnp.bfloat16),
                 jax.ShapeDtypeStruct((nt, m, TN), jnp.bfloat16)]
    args = [x, g.reshape(1, d), w, *tabs]
    if cast is not None:
        spec_in, spec_out, shape = _cast_specs(*cast, m // tm, lambda i: i, min(tm, CAST_ROWS))
        in_specs.append(spec_in)
        out_specs.append(spec_out)
        out_shape.append(shape)
        args.append(cast[0])
    return pl.pallas_call(
        functools.partial(_norm_proj_kernel, sets=sets),
        grid=(m // tm,),
        in_specs=in_specs,
        out_specs=out_specs,
        out_shape=out_shape,
        compiler_params=_params(("arbitrary",), vmem=RESIDENT_VMEM_LIMIT),
        name="norm_proj_b",
    )(*args)


def _gate_stages(ids, ins, outs, scratch, *, epi, cast):
    del ids, scratch
    h_ref, w_ref = ins[:2]
    o_ref = outs[0]
    if cast:
        outs[1][...] = ins[-1][...].astype(outs[1].dtype)
    h = h_ref[...]
    for t in range(o_ref.shape[0]):
        acc = jnp.dot(h, w_ref[:, t * TN:(t + 1) * TN], preferred_element_type=jnp.float32)
        gate = _sigmoid(acc)
        o_ref[t] = (acc * gate if epi == "silu" else gate).astype(o_ref.dtype)
        yield


def _gate_proj(h, w, epi, *, tm=256, cast=None, guest=None, name):
    m, k = h.shape
    nt = w.shape[1] // TN
    steps = m // tm
    in_specs = [pl.BlockSpec((tm, k), lambda i, _: (i, 0)),
                pl.BlockSpec(w.shape, lambda i, _: (0, 0), pipeline_mode=pl.Buffered(1))]
    out_specs = [pl.BlockSpec((nt, tm, TN), lambda i, _: (0, i, 0))]
    out_shape = [jax.ShapeDtypeStruct((nt, m, TN), jnp.bfloat16)]
    args = [h, w]
    if cast is not None:
        spec_in, spec_out, shape = _cast_specs(*cast, steps, lambda i, _: i, CAST_ROWS)
        in_specs.append(spec_in)
        out_specs.append(spec_out)
        out_shape.append(shape)
        args.append(cast[0])
    host = dict(stages=functools.partial(_gate_stages, epi=epi, cast=cast is not None),
                grid=(steps, 1), in_specs=in_specs, args=args, out_specs=out_specs, out_shape=out_shape,
                name=name, vmem=RESIDENT_VMEM_LIMIT)
    outs, guest_outs = _launch(host, guest) if guest is not None else (_launch(host), None)
    return outs[0], (outs[1] if cast is not None else None), guest_outs


A_TQ = 128
A_HALO = 64
LSE_LANES = 128


def _window_rows(prev_ref, cur_ref, next_ref, lo, hi, cols):
    n = cur_ref.shape[0]
    parts = []
    if lo < 0:
        parts.append(prev_ref[:, cols])
    parts.append(cur_ref[max(lo, 0):min(hi, n), cols])
    if hi > n:
        parts.append(next_ref[:, cols])
    return jnp.concatenate(parts, axis=0) if len(parts) > 1 else parts[0]


def _band_bias(tq, halo, radius, start, seq):
    nk = tq + 2 * halo
    r = lax.broadcasted_iota(jnp.int32, (tq, nk), 0)
    c = lax.broadcasted_iota(jnp.int32, (tq, nk), 1)
    kpos = start - halo + c
    valid = (jnp.abs(c - halo - r) <= radius) & (kpos >= 0) & (kpos < seq)
    return jnp.where(valid, 0.0, NEG_INF)


def _attn_a_stages(ids, ins, outs, scratch, *, seq):
    q_ref, kp_ref, kc_ref, kn_ref, vp_ref, vc_ref, vn_ref = ins
    o_ref, lse_ref = outs
    step = q_ref.shape[0]
    lane = lax.broadcasted_iota(jnp.int32, (A_TQ, LSE_LANES), 1)
    units = [(b, hh) for b in range(step // A_TQ) for hh in range(A_HEADS)]
    per_round = 4
    lse_all = {}

    def window(b):
        return slice(b * A_TQ, (b + 1) * A_TQ), b * A_TQ - A_HALO, (b + 1) * A_TQ + A_HALO

    def scores(b, hh):
        rows, lo, hi = window(b)
        sl = slice(hh * HEAD_DIM, (hh + 1) * HEAD_DIM)
        k = _window_rows(kp_ref, kc_ref, kn_ref, lo, hi, sl)
        return lax.dot_general(q_ref[rows, sl], k, (((1,), (1,)), ((), ())),
                               preferred_element_type=jnp.float32)

    def finish(b, hh, s):
        rows, lo, hi = window(b)
        sl = slice(hh * HEAD_DIM, (hh + 1) * HEAD_DIM)
        s = s + _band_bias(A_TQ, A_HALO, A_RADIUS, ids[1] * step + b * A_TQ, seq)
        m = jnp.max(s, axis=1, keepdims=True)
        e = jnp.exp2(s - m)
        den = jnp.sum(e, axis=1, keepdims=True)
        v = _window_rows(vp_ref, vc_ref, vn_ref, lo, hi, sl)
        o = jnp.dot(e.astype(jnp.bfloat16), v, preferred_element_type=jnp.float32)
        o_ref[rows, sl] = (o * (1.0 / den)).astype(o_ref.dtype)
        prev = lse_all.get(b, jnp.zeros((A_TQ, LSE_LANES), jnp.float32))
        lse_all[b] = jnp.where(lane == hh, m * LN2 + jnp.log(den), prev)
        if hh == A_HEADS - 1:
            lse_ref[rows, :] = lse_all.pop(b)

    pending = [scores(*u) for u in units[:per_round]]
    yield
    for r in range(0, len(units), per_round):
        for u, sc in zip(units[r:r + per_round], pending):
            finish(*u, sc)
        pending = [scores(*u) for u in units[r + per_round:r + 2 * per_round]]
        yield


def _attn_a(qkv, n, seq, step):
    qo, ko, vo = 0, n, 2 * n
    step = min(step, seq)
    nh = seq // A_HALO
    ratio = step // A_HALO
    cur = lambda off: pl.BlockSpec((None, step, A_WIDTH), lambda s, i: (off + s, i, 0))
    prev = lambda off: pl.BlockSpec(
        (None, A_HALO, A_WIDTH), lambda s, i: (off + s, jnp.maximum(ratio * i - 1, 0), 0))
    nxt = lambda off: pl.BlockSpec(
        (None, A_HALO, A_WIDTH), lambda s, i: (off + s, jnp.minimum(ratio * (i + 1), nh - 1), 0))
    return dict(
        stages=functools.partial(_attn_a_stages, seq=seq),
        grid=(n, seq // step),
        in_specs=[cur(qo), prev(ko), cur(ko), nxt(ko), prev(vo), cur(vo), nxt(vo)],
        args=[qkv] * 7,
        out_specs=[pl.BlockSpec((None, step, A_WIDTH), lambda s, i: (s, i, 0)),
                   pl.BlockSpec((None, step, LSE_LANES), lambda s, i: (s, i, 0))],
        out_shape=[jax.ShapeDtypeStruct((n, seq, A_WIDTH), jnp.bfloat16),
                   jax.ShapeDtypeStruct((n, seq, LSE_LANES), jnp.float32)],
        name="attn_a")


def _merge_a_stages(ids, ins, outs, scratch):
    del ids
    c0_ref, c1_ref, c2_ref, cl0_ref, cl1_ref, cl2_ref, z_ref = ins
    u_ref, = outs
    o1_ref, o2_ref, l1_ref, l2_ref = scratch
    tm = u_ref.shape[0]
    o0_ref, l0_ref = c0_ref.at[0], cl0_ref.at[0]
    for c_ref, cl_ref, o_ref, l_ref in ((c1_ref, cl1_ref, o1_ref, l1_ref), (c2_ref, cl2_ref, o2_ref, l2_ref)):
        dil = c_ref.shape[0]
        for r in range(dil):
            rows = pl.ds(r, tm // dil, stride=dil)
            for hh in range(A_HEADS):
                o_ref[hh, rows, :] = c_ref[r, :, hh * HEAD_DIM:(hh + 1) * HEAD_DIM].astype(jnp.float32)
            l_ref[rows, :] = cl_ref[r]
        yield
    l0, l1, l2 = l0_ref[...], l1_ref[...], l2_ref[...]
    m = jnp.maximum(jnp.maximum(l0, l1), l2)
    e0, e1, e2 = jnp.exp(l0 - m), jnp.exp(l1 - m), jnp.exp(l2 - m)
    inv = 1.0 / (e0 + e1 + e2)
    a0, a1, a2 = e0 * inv, e1 * inv, e2 * inv
    for hh in range(A_HEADS):
        sl = slice(hh * HEAD_DIM, (hh + 1) * HEAD_DIM)
        col = slice(hh, hh + 1)
        o = a0[:, col] * o0_ref[:, sl] + a1[:, col] * o1_ref[hh] + a2[:, col] * o2_ref[hh]
        u_ref[:, sl] = (o * z_ref[:, sl].astype(jnp.float32)).astype(u_ref.dtype)
        if hh == A_HEADS // 2 - 1:
            yield


def _merge_a(outs, lses, pz, batch, seq, tm=512):
    bps = seq // tm
    cls = lambda dil, width: pl.BlockSpec((None, dil, tm // dil, width),
                                          lambda i, _: (i // bps, 0, i % bps, 0))
    dils = [d for _, d in DIL_CONFIGS]
    return dict(
        stages=_merge_a_stages,
        grid=(batch * bps, 1),
        in_specs=([cls(d, A_WIDTH) for d in dils] + [cls(d, LSE_LANES) for d in dils]
                  + [pl.BlockSpec((None, tm, TN), lambda i, _: (0, i, 0))]),
        args=[*outs, *lses, pz],
        out_specs=[pl.BlockSpec((tm, A_WIDTH), lambda i, _: (i, 0))],
        out_shape=[jax.ShapeDtypeStruct((batch * seq, A_WIDTH), jnp.bfloat16)],
        scratch=[pltpu.VMEM((A_HEADS, tm, HEAD_DIM), jnp.float32)] * 2
                + [pltpu.VMEM((tm, LSE_LANES), jnp.float32)] * 2,
        name="merge_a")


B_TQ = 128


def _attn_b_stages(ids, ins, outs, scratch, *, seq):
    sink_ref, q0_ref, q1_ref, kvp_ref, kvc_ref, kvn_ref, z0_ref, z1_ref = ins
    u_ref, = outs
    step = u_ref.shape[0]
    ones = jnp.ones((3 * B_TQ, HEAD_DIM), jnp.bfloat16)
    kv_half = B_KV_HEADS * HEAD_DIM
    units = [(b, kh) for b in range(step // B_TQ) for kh in range(B_KV_HEADS)]

    def scores(b, kh):
        rows = slice(b * B_TQ, (b + 1) * B_TQ)
        lo, hi = (b - 1) * B_TQ, (b + 2) * B_TQ
        q_ref = q0_ref if kh < 2 else q1_ref
        base = (kh % 2) * B_GROUP * HEAD_DIM
        heads = [slice(base + g * HEAD_DIM, base + (g + 1) * HEAD_DIM) for g in range(B_GROUP)]
        k = _window_rows(kvp_ref, kvc_ref, kvn_ref, lo, hi, slice(kh * HEAD_DIM, (kh + 1) * HEAD_DIM))
        q = jnp.concatenate([q_ref[rows, sl] for sl in heads], axis=0)
        return lax.dot_general(q, k, (((1,), (1,)), ((), ())), preferred_element_type=jnp.float32)

    def finish(b, kh, s):
        rows = slice(b * B_TQ, (b + 1) * B_TQ)
        lo, hi = (b - 1) * B_TQ, (b + 2) * B_TQ
        z_ref = z0_ref if kh < 2 else z1_ref
        base = (kh % 2) * B_GROUP * HEAD_DIM
        heads = [slice(base + g * HEAD_DIM, base + (g + 1) * HEAD_DIM) for g in range(B_GROUP)]
        bias = _band_bias(B_TQ, B_TQ, B_RADIUS, ids[1] * step + b * B_TQ, seq)
        v = _window_rows(kvp_ref, kvc_ref, kvn_ref, lo, hi,
                         slice(kv_half + kh * HEAD_DIM, kv_half + (kh + 1) * HEAD_DIM))
        es, ms, sks = [], [], []
        for g in range(B_GROUP):
            sg = s[g * B_TQ:(g + 1) * B_TQ] + bias
            sk = sink_ref[kh * B_GROUP + g] * LOG2E
            m = jnp.maximum(jnp.max(sg, axis=1, keepdims=True), sk)
            es.append(jnp.exp2(sg - m).astype(jnp.bfloat16))
            ms.append(m)
            sks.append(sk)
        ov = jnp.dot(jnp.concatenate(es, axis=0), jnp.concatenate([v, ones], axis=1),
                     preferred_element_type=jnp.float32)
        for g in range(B_GROUP):
            og = ov[g * B_TQ:(g + 1) * B_TQ]
            den = og[:, HEAD_DIM:HEAD_DIM + 1] + jnp.exp2(sks[g] - ms[g])
            col = (kh * B_GROUP + g) * HEAD_DIM
            u_ref[rows, col:col + HEAD_DIM] = (
                og[:, :HEAD_DIM] * (1.0 / den) * z_ref[rows, heads[g]].astype(jnp.float32)
            ).astype(u_ref.dtype)

    per_round = 4
    pending = [scores(*u) for u in units[:per_round]]
    yield
    for r in range(0, len(units), per_round):
        for u, sc in zip(units[r:r + per_round], pending):
            finish(*u, sc)
        pending = [scores(*u) for u in units[r + per_round:r + 2 * per_round]]
        yield


def _attn_b(sink, pb, pz, batch, seq, step):
    nq = seq // step
    nh = seq // B_TQ
    ratio = step // B_TQ
    row = lambda slot: pl.BlockSpec((None, step, TN), lambda b, i: (slot, b * nq + i, 0))
    kv_prev = pl.BlockSpec((None, B_TQ, TN), lambda b, i: (2, b * nh + jnp.maximum(ratio * i - 1, 0), 0))
    kv_next = pl.BlockSpec((None, B_TQ, TN),
                           lambda b, i: (2, b * nh + jnp.minimum(ratio * (i + 1), nh - 1), 0))
    width = B_Q_HEADS * HEAD_DIM
    return dict(
        stages=functools.partial(_attn_b_stages, seq=seq),
        grid=(batch, nq),
        in_specs=[pl.BlockSpec(memory_space=pltpu.SMEM),
                  row(0), row(1), kv_prev, row(2), kv_next, row(1), row(2)],
        args=[sink, pb, pb, pb, pb, pb, pz, pz],
        out_specs=[pl.BlockSpec((step, width), lambda b, i: (b * nq + i, 0))],
        out_shape=[jax.ShapeDtypeStruct((batch * seq, width), jnp.bfloat16)],
        name="attn_b")


M_TQ = 512


def _attn_m_kernel(q_ref, k_ref, v_ref, z_ref, u_ref):
    for hh in range(M_HEADS):
        sl = slice(hh * M_HEAD_DIM, (hh + 1) * M_HEAD_DIM)
        s = lax.dot_general(q_ref[:, sl], k_ref[:, sl], (((1,), (1,)), ((), ())),
                            preferred_element_type=jnp.float32)
        m = jnp.max(s, axis=1, keepdims=True)
        e = jnp.exp2(s - m)
        den = jnp.sum(e, axis=1, keepdims=True)
        o = jnp.dot(e.astype(jnp.bfloat16), v_ref[:, sl], preferred_element_type=jnp.float32) * (1.0 / den)
        u_ref[:, sl] = (o * z_ref[:, sl].astype(jnp.float32)).astype(u_ref.dtype)


def _attn_m(pb, kv_mem, pz, batch, seq):
    nq = seq // M_TQ
    width = M_HEADS * M_HEAD_DIM
    return pl.pallas_call(
        _attn_m_kernel,
        grid=(batch, nq),
        in_specs=[pl.BlockSpec((None, M_TQ, TN), lambda b, i: (3, b * nq + i, 0)),
                  pl.BlockSpec((None, MEM_LEN, TN), lambda b, i: (0, b, 0)),
                  pl.BlockSpec((None, MEM_LEN, TN), lambda b, i: (1, b, 0)),
                  pl.BlockSpec((None, M_TQ, TN), lambda b, i: (3, b * nq + i, 0))],
        out_specs=pl.BlockSpec((M_TQ, width), lambda b, i: (b * nq + i, 0)),
        out_shape=jax.ShapeDtypeStruct((batch * seq, width), jnp.bfloat16),
        compiler_params=_params(("parallel", "parallel")),
        name="attn_m",
    )(pb, kv_mem, kv_mem, pz)


def _branch_stages(ids, ins, outs, scratch):
    del ids, scratch
    ua_ref, ub_ref, um_ref, w_ref, ga_ref, gb_ref, gm_ref = ins
    u_ref, = outs
    a_hi = A_WIDTH
    b_hi = A_WIDTH + B_Q_HEADS * HEAD_DIM
    dot = functools.partial(jnp.dot, preferred_element_type=jnp.float32)
    acc = ga_ref[...].astype(jnp.float32) * dot(ua_ref[...], w_ref[:a_hi, :])
    yield
    acc += gb_ref[...].astype(jnp.float32) * dot(ub_ref[...], w_ref[a_hi:b_hi, :])
    yield
    acc += gm_ref[...].astype(jnp.float32) * dot(um_ref[...], w_ref[b_hi:, :])
    u_ref[...] = acc.astype(u_ref.dtype)


def _branch(ua, ub, um, w_branch, gates, tm=512):
    m = ua.shape[0]
    k = w_branch.shape[0]
    n_tiles = D_MODEL // TN
    act = lambda width: pl.BlockSpec((tm, width), lambda j, i: (i, 0))
    gate = pl.BlockSpec((None, tm, TN), lambda j, i: (j, i, 0))
    return dict(
        stages=_branch_stages,
        grid=(n_tiles, m // tm),
        in_specs=[act(ua.shape[1]), act(ub.shape[1]), act(um.shape[1]),
                  pl.BlockSpec((k, TN), lambda j, i: (0, j)),
                  gate, gate, gate],
        args=[ua, ub, um, w_branch, *gates],
        out_specs=[pl.BlockSpec((tm, TN), lambda j, i: (i, j))],
        out_shape=[jax.ShapeDtypeStruct((m, D_MODEL), jnp.bfloat16)],
        name="branch_proj")


def _out_kernel(u_ref, w_ref, x_ref, g_ref, y_ref):
    y = x_ref[...] + jnp.dot(u_ref[...], w_ref[...], preferred_element_type=jnp.float32)
    inv = lax.rsqrt(jnp.mean(y * y, axis=1, keepdims=True) + NORM_EPS)
    y_ref[...] = (y * inv) * g_ref[...]


def _out_proj(u, w_out, x, g_final, tm=256):
    m, k = u.shape
    n = x.shape[1]
    return pl.pallas_call(
        _out_kernel,
        grid=(m // tm,),
        in_specs=[pl.BlockSpec((tm, k), lambda i: (i, 0)),
                  pl.BlockSpec((k, n), lambda i: (0, 0), pipeline_mode=pl.Buffered(1)),
                  pl.BlockSpec((tm, n), lambda i: (i, 0)),
                  pl.BlockSpec((1, n), lambda i: (0, 0))],
        out_specs=pl.BlockSpec((tm, n), lambda i: (i, 0)),
        out_shape=jax.ShapeDtypeStruct((m, n), jnp.float32),
        compiler_params=_params(("arbitrary",), vmem=RESIDENT_VMEM_LIMIT),
        name="out_proj",
    )(u, w_out, x, g_final.reshape(1, n))


def _rope_tables(seq):
    inv_freq = ROPE_THETA ** (-jnp.arange(0, HEAD_DIM, 2, dtype=jnp.float32) / HEAD_DIM)
    ang = jnp.arange(seq, dtype=jnp.float32)[:, None] * inv_freq[None, :]
    cos, sin = jnp.cos(ang), jnp.sin(ang)
    cos, sin = jnp.concatenate([cos, cos], axis=1), jnp.concatenate([-sin, sin], axis=1)
    one, zero = jnp.ones_like(cos), jnp.zeros_like(sin)
    q_scale = HEAD_DIM ** -0.5 * LOG2E
    m_scale = M_HEAD_DIM ** -0.5 * LOG2E
    return (jnp.stack([cos * q_scale, cos, one, one * m_scale]),
            jnp.stack([sin * q_scale, sin, zero, zero]))


def kernel(x_prompt, x_sample, mem_prompt, mem_sample, g_norm, w_in, attn_sink, g_mem, w_mem_kv, w_branch, w_out, g_final):
    bf = jnp.bfloat16
    w_in, w_mem_kv, w_branch, w_out = w_in[0], w_mem_kv[0], w_branch[0], w_out[0]
    tabs = _rope_tables(max(x_prompt.shape[1], x_sample.shape[1]))
    trunks = []
    for x, mem in ((x_prompt, mem_prompt), (x_sample, mem_sample)):
        batch, seq, d = x.shape
        trunks.append(dict(batch=batch, seq=seq, rows=batch * seq, x2=x.reshape(batch * seq, d),
                           hm=_rmsnorm(mem.reshape(batch * MEM_LEN, d), g_mem[0], bf)))
    p, s_ = trunks

    w_b = _cast_tiles(w_in, B_TILES)
    b_sets = ((ROT_Q, ROT_Q, ROT_K, ROT_M), (ROT_Q, ROT_Q, ROT_ID, ROT_M))
    a_sets = ((ROT_Q, ROT_K, ROT_ID),) * 2
    dils = [dil for _, dil in DIL_CONFIGS]
    g_lo = G_TILES[0]
    a_cols = lambda gi: (w_in, (gi, A_TILES[1], len(dils)))
    in_cols = lambda lo: (w_in, (lo, lo + 4))

    s_["h"], pb_s, w_a0 = _norm_proj(s_["x2"], g_norm[0], w_b, tabs, b_sets, batch=s_["batch"], cast=a_cols(0))
    p["h"], pb_p = _norm_proj(p["x2"], g_norm[0], w_b, tabs, b_sets, batch=p["batch"])

    def proj_a(t, gi, wts, cast):
        out = _proj(t["h"], wts, (0, 3), "rot", batch=t["batch"], tm=1024, dil=dils[gi], tabs=tabs,
                    sets=a_sets, cast=cast, name=f"proj_a{gi}")
        qkv, w_next = out if cast is not None else (out, None)
        n, sd = t["batch"] * dils[gi], t["seq"] // dils[gi]
        return _attn_a(qkv.reshape(3 * n, sd, A_WIDTH), n, sd, GUEST_STEP), w_next

    a0_p, w_a1 = proj_a(p, 0, w_a0, a_cols(1))
    a0_s, w_a2 = proj_a(s_, 0, w_a0, a_cols(2))
    a1_p, w_m = proj_a(p, 1, w_a1, (w_mem_kv, (0, 2)))
    a1_s, w_z = proj_a(s_, 1, w_a1, (w_in, Z_TILES))
    a2_p, _ = proj_a(p, 2, w_a2, None)
    a2_s, w_ga = proj_a(s_, 2, w_a2, in_cols(g_lo))

    gate = lambda t, wts, epi, cast, guest, name: _gate_proj(t["h"], wts, epi, cast=cast, guest=guest, name=name)
    attn_b = lambda t, pb, pz: _attn_b(attn_sink[0], pb, pz, t["batch"], t["seq"], GUEST_STEP)
    pz_p, w_gb, g0_p = gate(p, w_z, "silu", in_cols(g_lo + 4), a0_p, "gate_z")
    pz_s, w_gm, g0_s = gate(s_, w_z, "silu", in_cols(g_lo + 8), a0_s, "gate_z")
    ga_p, w_o, g1_p = gate(p, w_ga, "sigmoid", (w_out, (0, -(-w_out.shape[1] // TN))), a1_p, "gate_a")
    ga_s, w_br, g1_s = gate(s_, w_ga, "sigmoid", (w_branch, (0, D_MODEL // TN)), a1_s, "gate_a")
    gb_p, _, g2_p = gate(p, w_gb, "sigmoid", None, a2_p, "gate_b")
    gb_s, _, g2_s = gate(s_, w_gb, "sigmoid", None, a2_s, "gate_b")
    gm_p, _, (ub_p,) = gate(p, w_gm, "sigmoid", None, attn_b(p, pb_p, pz_p), "gate_m")
    gm_s, _, (ub_s,) = gate(s_, w_gm, "sigmoid", None, attn_b(s_, pb_s, pz_s), "gate_m")

    def group_outputs(t, groups):
        shape = lambda dil, width: (t["batch"], dil, t["seq"] // dil, width)
        return ([o.reshape(shape(dil, A_WIDTH)) for (o, _), dil in zip(groups, dils)],
                [l.reshape(shape(dil, LSE_LANES)) for (_, l), dil in zip(groups, dils)])

    def attn_m(t, pb, pz):
        kv_mem = _proj(t["hm"], w_m, (0, 2), "none", batch=1, tm=t["batch"] * MEM_LEN, name="proj_mem")
        return _attn_m(pb, kv_mem.reshape(2, t["batch"] * MEM_LEN, TN), pz, t["batch"], t["seq"])

    um_p, um_s = attn_m(p, pb_p, pz_p), attn_m(s_, pb_s, pz_s)
    ao_p, al_p = group_outputs(p, (g0_p, g1_p, g2_p))
    ao_s, al_s = group_outputs(s_, (g0_s, g1_s, g2_s))
    ua_p, = _launch(_merge_a(ao_p, al_p, pz_p, p["batch"], p["seq"]))
    branch_p = _branch(ua_p, ub_p, um_p, w_br, (ga_p, gb_p, gm_p))
    steps = branch_p["grid"][0] * branch_p["grid"][1]
    (u_p,), (ua_s,) = _launch(branch_p, _merge_a(ao_s, al_s, pz_s, s_["batch"], s_["seq"], tm=s_["rows"] // steps))
    u_s, = _launch(_branch(ua_s, ub_s, um_s, w_br, (ga_s, gb_s, gm_s)))
    finish = lambda t, u: _out_proj(u, w_o, t["x2"], g_final).reshape(t["batch"], t["seq"], t["x2"].shape[1])
    return finish(p, u_p), finish(s_, u_s)
```

```python
import functools

import jax
import jax.numpy as jnp
from jax import lax
from jax.experimental import pallas as pl
from jax.experimental.pallas import tpu as pltpu

D_MODEL = 4096
HEAD_DIM = 128
ROPE_THETA = 10000.0
NORM_EPS = 1e-6
NEG_INF = -1e30
LOG2E = 1.4426950408889634
LN2 = 0.6931471805599453

DIL_CONFIGS = ((128, 1), (512, 4), (2048, 16))
A_HEADS = 8
A_WIDTH = A_HEADS * HEAD_DIM
A_RADIUS = 64

B_Q_HEADS = 16
B_KV_HEADS = 4
B_GROUP = B_Q_HEADS // B_KV_HEADS
B_RADIUS = 128

MEM_LEN = 256
M_HEADS = 4
M_HEAD_DIM = 256

TN = 1024
A_TILES = (0, 9)
B_TILES = (9, 13)
Z_TILES = (13, 17)
G_TILES = (17, 29)
ROT_Q, ROT_K, ROT_ID, ROT_M = range(4)

VMEM_LIMIT = 56 * 1024 * 1024
RESIDENT_VMEM_LIMIT = 60 * 1024 * 1024
CAST_ROWS = 512
GUEST_STEP = 256


def _params(sem, vmem=VMEM_LIMIT):
    return pltpu.CompilerParams(dimension_semantics=sem, vmem_limit_bytes=vmem)


def _launch(host, *guests):
    grid = host["grid"]
    steps_per_row = grid[1] if len(grid) == 2 else 1

    def mapped(guest):
        g0, g1 = guest["grid"]
        assert g0 * g1 == grid[0] * steps_per_row, (guest["grid"], grid)
        to_guest = lambda i, j: divmod(i * steps_per_row + j, g1)
        remap = lambda spec: pl.BlockSpec(
            spec.block_shape, (lambda i, j, f=spec.index_map: f(*to_guest(i, j))),
            memory_space=spec.memory_space) if spec.index_map is not None else spec
        return dict(guest, to_ids=to_guest, in_specs=[remap(sp) for sp in guest["in_specs"]],
                    out_specs=[remap(sp) for sp in guest["out_specs"]])

    parts = [host] + [mapped(g) for g in guests]
    n_in = [len(p["in_specs"]) for p in parts]
    n_out = [len(p["out_specs"]) for p in parts]
    n_scr = [len(p.get("scratch", [])) for p in parts]

    def body(*refs):
        refs = list(refs)
        take = lambda n: [refs.pop(0) for _ in range(n)]
        ins = [take(n) for n in n_in]
        outs = [take(n) for n in n_out]
        scr = [take(n) for n in n_scr]
        ids = (pl.program_id(0), pl.program_id(1)) if len(grid) == 2 else (pl.program_id(0),)
        gens = [p["stages"](p["to_ids"](*ids) if "to_ids" in p else ids, ins[k], outs[k], scr[k])
                for k, p in enumerate(parts)]
        gens = gens[1:] + gens[:1]
        while gens:
            gens = [g for g in gens if next(g, "done") != "done"]

    res = pl.pallas_call(
        body,
        grid=grid,
        in_specs=[sp for p in parts for sp in p["in_specs"]],
        out_specs=[sp for p in parts for sp in p["out_specs"]],
        out_shape=[sh for p in parts for sh in p["out_shape"]],
        scratch_shapes=[sc for p in parts for sc in p.get("scratch", [])],
        compiler_params=_params(("arbitrary",) * len(grid), vmem=host.get("vmem", VMEM_LIMIT)),
        name="_".join(p["name"] for p in parts),
    )(*[a for p in parts for a in p["args"]])
    if not guests:
        return res
    split, k = [], 0
    for n in n_out:
        split.append(res[k:k + n])
        k += n
    return tuple(split)


def _rmsnorm_kernel(x_ref, g_ref, o_ref):
    x = x_ref[...]
    ms = jnp.mean(x * x, axis=-1, keepdims=True)
    o_ref[...] = ((x * lax.rsqrt(ms + NORM_EPS)) * g_ref[...]).astype(o_ref.dtype)


def _rmsnorm(x, g, out_dtype, tm=256):
    m, d = x.shape
    return pl.pallas_call(
        _rmsnorm_kernel,
        grid=(m // tm,),
        in_specs=[pl.BlockSpec((tm, d), lambda i: (i, 0)),
                  pl.BlockSpec((1, d), lambda i: (0, 0))],
        out_specs=pl.BlockSpec((tm, d), lambda i: (i, 0)),
        out_shape=jax.ShapeDtypeStruct((m, d), out_dtype),
        compiler_params=_params(("parallel",)),
        name="rmsnorm",
    )(x, g.reshape(1, d))


def _cast_specs(src, src_tiles, steps, step_of, rows):
    src_tiles = range(*src_tiles)
    cr = min(rows, src.shape[0])
    cw = min(TN, src.shape[1])
    row_blocks = src.shape[0] // cr
    n_chunks = row_blocks * len(src_tiles)
    assert n_chunks <= steps, "not enough grid steps to cast this weight group"
    chunk = lambda *ids: jnp.minimum(step_of(*ids), n_chunks - 1)
    spec_in = pl.BlockSpec(
        (cr, cw), lambda *ids: (chunk(*ids) % row_blocks,
                                src_tiles.start + src_tiles.step * (chunk(*ids) // row_blocks)))
    spec_out = pl.BlockSpec((cr, cw), lambda *ids: (chunk(*ids) % row_blocks, chunk(*ids) // row_blocks))
    return spec_in, spec_out, jax.ShapeDtypeStruct((src.shape[0], len(src_tiles) * cw), jnp.bfloat16)


def _cast_kernel(src_ref, dst_ref):
    dst_ref[...] = src_ref[...].astype(dst_ref.dtype)


def _cast_tiles(src, tiles):
    lo, hi = tiles
    rows = src.shape[0]
    cr = min(CAST_ROWS, rows)
    return pl.pallas_call(
        _cast_kernel,
        grid=(hi - lo, rows // cr),
        in_specs=[pl.BlockSpec((cr, TN), lambda j, i: (i, lo + j))],
        out_specs=pl.BlockSpec((cr, TN), lambda j, i: (i, j)),
        out_shape=jax.ShapeDtypeStruct((rows, (hi - lo) * TN), jnp.bfloat16),
        compiler_params=_params(("parallel", "parallel")),
        name="cast_tiles",
    )(src)


def _sigmoid(x):
    return 0.5 * jnp.tanh(0.5 * x) + 0.5


def _proj_stages(ids, ins, outs, scratch, *, epi, dil, cast):
    del ids
    h_ref, w_ref = ins[:2]
    o_ref = outs[0]
    tm = h_ref.shape[0]
    if cast:
        outs[1][...] = ins[-1][...].astype(outs[1].dtype)
    yield
    acc = jnp.dot(h_ref[...], w_ref[...], preferred_element_type=jnp.float32)
    if epi == "none":
        o_ref[0] = acc.astype(o_ref.dtype)
    elif epi == "silu":
        o_ref[0] = (acc * _sigmoid(acc)).astype(o_ref.dtype)
    elif epi == "sigmoid":
        o_ref[0] = _sigmoid(acc).astype(o_ref.dtype)
    else:
        cos_lo, sin_lo, cos_hi, sin_hi = ins[2:6]
        heads = [slice(hh * HEAD_DIM, (hh + 1) * HEAD_DIM) for hh in range(TN // HEAD_DIM)]
        for hh, sl in enumerate(heads):
            cos_ref, sin_ref = (cos_lo, sin_lo) if hh < len(heads) // 2 else (cos_hi, sin_hi)
            x = acc[:, sl]
            x = x * cos_ref[...] + pltpu.roll(x, HEAD_DIM // 2, 1) * sin_ref[...]
            if dil > 1:
                o_ref[:, :, sl] = jnp.swapaxes(x.reshape(tm // dil, dil, HEAD_DIM), 0, 1).astype(o_ref.dtype)
            else:
                o_ref[0, :, sl] = x.astype(o_ref.dtype)


def _select(j, values):
    if len(set(values)) == 1:
        return values[0]
    out = values[-1]
    for t in range(len(values) - 2, -1, -1):
        out = jnp.where(j == t, values[t], out)
    return out


def _proj(h, w, tiles, epi, *, batch, tm, dil=1, tabs=None, sets=None, cast=None, name):
    m, k = h.shape
    seq = m // batch
    col_tiles = range(*tiles)
    lo, step, nt = col_tiles.start, col_tiles.step, len(col_tiles)
    assert seq % tm == 0 and tm % dil == 0 and (dil == 1 or epi == "rot")
    bps = seq // tm
    in_specs = [pl.BlockSpec((tm, k), lambda i, j: (i, 0)),
                pl.BlockSpec((k, TN), lambda i, j: (0, lo + step * j))]
    args = [h, w]
    if epi == "rot":
        for half in sets:
            assert len(half) == nt
            spec = pl.BlockSpec((None, tm, HEAD_DIM),
                                lambda i, j, half=half: (_select(j, half), i % bps, 0))
            in_specs += [spec, spec]
            args += list(tabs)
    out_specs = [pl.BlockSpec((None, None, dil, tm // dil, TN), lambda i, j: (j, i // bps, 0, i % bps, 0))]
    out_shape = [jax.ShapeDtypeStruct((nt, batch, dil, seq // dil, TN), jnp.bfloat16)]
    if cast is not None:
        spec_in, spec_out, shape = _cast_specs(*cast, (m // tm) * nt, lambda i, j: i * nt + j, CAST_ROWS)
        in_specs.append(spec_in)
        out_specs.append(spec_out)
        out_shape.append(shape)
        args.append(cast[0])
    host = dict(
        stages=functools.partial(_proj_stages, epi=epi, dil=dil, cast=cast is not None),
        grid=(m // tm, nt), in_specs=in_specs, args=args, out_specs=out_specs, out_shape=out_shape,
        name=name)
    outs = _launch(host)
    return outs if cast is not None else outs[0]


def _norm_proj_kernel(x_ref, g_ref, w_ref, cos_ref, sin_ref, *rest, sets):
    if len(rest) == 4:
        src_ref, h_ref, o_ref, dst_ref = rest
        dst_ref[...] = src_ref[...].astype(dst_ref.dtype)
    else:
        h_ref, o_ref = rest
    x = x_ref[...]
    h = ((x * lax.rsqrt(jnp.mean(x * x, axis=-1, keepdims=True) + NORM_EPS)) * g_ref[...]).astype(h_ref.dtype)
    h_ref[...] = h
    heads_per_tile = TN // HEAD_DIM
    for t in range(o_ref.shape[0]):
        acc = jnp.dot(h, w_ref[:, t * TN:(t + 1) * TN], preferred_element_type=jnp.float32)
        for hh in range(heads_per_tile):
            sl = slice(hh * HEAD_DIM, (hh + 1) * HEAD_DIM)
            tab = sets[hh * 2 // heads_per_tile][t]
            xh = acc[:, sl]
            o_ref[t, :, sl] = (xh * cos_ref[tab] + pltpu.roll(xh, HEAD_DIM // 2, 1) * sin_ref[tab]).astype(o_ref.dtype)


def _norm_proj(x, g, w, tabs, sets, *, batch, tm=256, cast=None):
    m, d = x.shape
    seq = m // batch
    bps = seq // tm
    nt = w.shape[1] // TN
    n_sets = tabs[0].shape[0]
    tab_spec = pl.BlockSpec((n_sets, tm, HEAD_DIM), lambda i: (0, i % bps, 0))
    in_specs = [pl.BlockSpec((tm, d), lambda i: (i, 0)),
                pl.BlockSpec((1, d), lambda i: (0, 0)),
                pl.BlockSpec(w.shape, lambda i: (0, 0), pipeline_mode=pl.Buffered(1)),
                tab_spec, tab_spec]
    out_specs = [pl.BlockSpec((tm, d), lambda i: (i, 0)),
                 pl.BlockSpec((nt, tm, TN), lambda i: (0, i, 0))]
    out_shape = [jax.ShapeDtypeStruct((m, d), jnp.bfloat16),
                 jax.ShapeDtypeStruct((nt, m, TN), jnp.bfloat16)]
    args = [x, g.reshape(1, d), w, *tabs]
    if cast is not None:
        spec_in, spec_out, shape = _cast_specs(*cast, m // tm, lambda i: i, min(tm, CAST_ROWS))
        in_specs.append(spec_in)
        out_specs.append(spec_out)
        out_shape.append(shape)
        args.append(cast[0])
    return pl.pallas_call(
        functools.partial(_norm_proj_kernel, sets=sets),
        grid=(m // tm,),
        in_specs=in_specs,
        out_specs=out_specs,
        out_shape=out_shape,
        compiler_params=_params(("arbitrary",), vmem=RESIDENT_VMEM_LIMIT),
        name="norm_proj_b",
    )(*args)


def _gate_stages(ids, ins, outs, scratch, *, epi, cast):
    del ids, scratch
    h_ref, w_ref = ins[:2]
    o_ref = outs[0]
    if cast:
        outs[1][...] = ins[-1][...].astype(outs[1].dtype)
    h = h_ref[...]
    for t in range(o_ref.shape[0]):
        acc = jnp.dot(h, w_ref[:, t * TN:(t + 1) * TN], preferred_element_type=jnp.float32)
        gate = _sigmoid(acc)
        o_ref[t] = (acc * gate if epi == "silu" else gate).astype(o_ref.dtype)
        yield


def _gate_proj(h, w, epi, *, tm=256, cast=None, guests=(), name):
    m, k = h.shape
    nt = w.shape[1] // TN
    steps = m // tm
    in_specs = [pl.BlockSpec((tm, k), lambda i, _: (i, 0)),
                pl.BlockSpec(w.shape, lambda i, _: (0, 0), pipeline_mode=pl.Buffered(1))]
    out_specs = [pl.BlockSpec((nt, tm, TN), lambda i, _: (0, i, 0))]
    out_shape = [jax.ShapeDtypeStruct((nt, m, TN), jnp.bfloat16)]
    args = [h, w]
    if cast is not None:
        spec_in, spec_out, shape = _cast_specs(*cast, steps, lambda i, _: i, CAST_ROWS)
        in_specs.append(spec_in)
        out_specs.append(spec_out)
        out_shape.append(shape)
        args.append(cast[0])
    host = dict(stages=functools.partial(_gate_stages, epi=epi, cast=cast is not None),
                grid=(steps, 1), in_specs=in_specs, args=args, out_specs=out_specs, out_shape=out_shape,
                name=name, vmem=RESIDENT_VMEM_LIMIT)
    outs, *guest_outs = _launch(host, *guests) if guests else (_launch(host),)
    return (outs[0], (outs[1] if cast is not None else None), *guest_outs)


A_TQ = 128
A_HALO = 64
LSE_LANES = 128


def _window_rows(prev_ref, cur_ref, next_ref, lo, hi, cols):
    n = cur_ref.shape[0]
    parts = []
    if lo < 0:
        parts.append(prev_ref[:, cols])
    parts.append(cur_ref[max(lo, 0):min(hi, n), cols])
    if hi > n:
        parts.append(next_ref[:, cols])
    return jnp.concatenate(parts, axis=0) if len(parts) > 1 else parts[0]


def _band_bias(tq, halo, radius, start, seq):
    nk = tq + 2 * halo
    r = lax.broadcasted_iota(jnp.int32, (tq, nk), 0)
    c = lax.broadcasted_iota(jnp.int32, (tq, nk), 1)
    kpos = start - halo + c
    valid = (jnp.abs(c - halo - r) <= radius) & (kpos >= 0) & (kpos < seq)
    return jnp.where(valid, 0.0, NEG_INF)


def _attn_a_stages(ids, ins, outs, scratch, *, seq):
    q_ref, kp_ref, kc_ref, kn_ref, vp_ref, vc_ref, vn_ref = ins
    o_ref, lse_ref = outs
    step = q_ref.shape[0]
    lane = lax.broadcasted_iota(jnp.int32, (A_TQ, LSE_LANES), 1)
    units = [(b, hh) for b in range(step // A_TQ) for hh in range(A_HEADS)]
    per_round = 4
    lse_all = {}

    def window(b):
        return slice(b * A_TQ, (b + 1) * A_TQ), b * A_TQ - A_HALO, (b + 1) * A_TQ + A_HALO

    def scores(b, hh):
        rows, lo, hi = window(b)
        sl = slice(hh * HEAD_DIM, (hh + 1) * HEAD_DIM)
        k = _window_rows(kp_ref, kc_ref, kn_ref, lo, hi, sl)
        return lax.dot_general(q_ref[rows, sl], k, (((1,), (1,)), ((), ())),
                               preferred_element_type=jnp.float32)

    def finish(b, hh, s):
        rows, lo, hi = window(b)
        sl = slice(hh * HEAD_DIM, (hh + 1) * HEAD_DIM)
        s = s + _band_bias(A_TQ, A_HALO, A_RADIUS, ids[1] * step + b * A_TQ, seq)
        m = jnp.max(s, axis=1, keepdims=True)
        e = jnp.exp2(s - m)
        den = jnp.sum(e, axis=1, keepdims=True)
        v = _window_rows(vp_ref, vc_ref, vn_ref, lo, hi, sl)
        o = jnp.dot(e.astype(jnp.bfloat16), v, preferred_element_type=jnp.float32)
        o_ref[rows, sl] = (o * (1.0 / den)).astype(o_ref.dtype)
        prev = lse_all.get(b, jnp.zeros((A_TQ, LSE_LANES), jnp.float32))
        lse_all[b] = jnp.where(lane == hh, m * LN2 + jnp.log(den), prev)
        if hh == A_HEADS - 1:
            lse_ref[rows, :] = lse_all.pop(b)

    pending = [scores(*u) for u in units[:per_round]]
    yield
    for r in range(0, len(units), per_round):
        for u, sc in zip(units[r:r + per_round], pending):
            finish(*u, sc)
        pending = [scores(*u) for u in units[r + per_round:r + 2 * per_round]]
        yield


def _attn_a(qkv, n, seq, step):
    qo, ko, vo = 0, n, 2 * n
    step = min(step, seq)
    nh = seq // A_HALO
    ratio = step // A_HALO
    cur = lambda off: pl.BlockSpec((None, step, A_WIDTH), lambda s, i: (off + s, i, 0))
    prev = lambda off: pl.BlockSpec(
        (None, A_HALO, A_WIDTH), lambda s, i: (off + s, jnp.maximum(ratio * i - 1, 0), 0))
    nxt = lambda off: pl.BlockSpec(
        (None, A_HALO, A_WIDTH), lambda s, i: (off + s, jnp.minimum(ratio * (i + 1), nh - 1), 0))
    return dict(
        stages=functools.partial(_attn_a_stages, seq=seq),
        grid=(n, seq // step),
        in_specs=[cur(qo), prev(ko), cur(ko), nxt(ko), prev(vo), cur(vo), nxt(vo)],
        args=[qkv] * 7,
        out_specs=[pl.BlockSpec((None, step, A_WIDTH), lambda s, i: (s, i, 0)),
                   pl.BlockSpec((None, step, LSE_LANES), lambda s, i: (s, i, 0))],
        out_shape=[jax.ShapeDtypeStruct((n, seq, A_WIDTH), jnp.bfloat16),
                   jax.ShapeDtypeStruct((n, seq, LSE_LANES), jnp.float32)],
        name="attn_a")


def _merge_a_stages(ids, ins, outs, scratch):
    del ids
    c0_ref, c1_ref, c2_ref, cl0_ref, cl1_ref, cl2_ref, z_ref = ins
    u_ref, = outs
    o1_ref, o2_ref, l1_ref, l2_ref = scratch
    tm = u_ref.shape[0]
    o0_ref, l0_ref = c0_ref.at[0], cl0_ref.at[0]
    for c_ref, cl_ref, o_ref, l_ref in ((c1_ref, cl1_ref, o1_ref, l1_ref), (c2_ref, cl2_ref, o2_ref, l2_ref)):
        dil = c_ref.shape[0]
        for r in range(dil):
            rows = pl.ds(r, tm // dil, stride=dil)
            for hh in range(A_HEADS):
                o_ref[hh, rows, :] = c_ref[r, :, hh * HEAD_DIM:(hh + 1) * HEAD_DIM].astype(jnp.float32)
            l_ref[rows, :] = cl_ref[r]
        yield
    l0, l1, l2 = l0_ref[...], l1_ref[...], l2_ref[...]
    m = jnp.maximum(jnp.maximum(l0, l1), l2)
    e0, e1, e2 = jnp.exp(l0 - m), jnp.exp(l1 - m), jnp.exp(l2 - m)
    inv = 1.0 / (e0 + e1 + e2)
    a0, a1, a2 = e0 * inv, e1 * inv, e2 * inv
    for hh in range(A_HEADS):
        sl = slice(hh * HEAD_DIM, (hh + 1) * HEAD_DIM)
        col = slice(hh, hh + 1)
        o = a0[:, col] * o0_ref[:, sl] + a1[:, col] * o1_ref[hh] + a2[:, col] * o2_ref[hh]
        u_ref[:, sl] = (o * z_ref[:, sl].astype(jnp.float32)).astype(u_ref.dtype)
        if hh == A_HEADS // 2 - 1:
            yield


def _merge_a(outs, lses, pz, batch, seq, tm=512):
    bps = seq // tm
    cls = lambda dil, width: pl.BlockSpec((None, dil, tm // dil, width),
                                          lambda i, _: (i // bps, 0, i % bps, 0))
    dils = [d for _, d in DIL_CONFIGS]
    return dict(
        stages=_merge_a_stages,
        grid=(batch * bps, 1),
        in_specs=([cls(d, A_WIDTH) for d in dils] + [cls(d, LSE_LANES) for d in dils]
                  + [pl.BlockSpec((None, tm, TN), lambda i, _: (0, i, 0))]),
        args=[*outs, *lses, pz],
        out_specs=[pl.BlockSpec((tm, A_WIDTH), lambda i, _: (i, 0))],
        out_shape=[jax.ShapeDtypeStruct((batch * seq, A_WIDTH), jnp.bfloat16)],
        scratch=[pltpu.VMEM((A_HEADS, tm, HEAD_DIM), jnp.float32)] * 2
                + [pltpu.VMEM((tm, LSE_LANES), jnp.float32)] * 2,
        name="merge_a")


B_TQ = 128


def _attn_b_stages(ids, ins, outs, scratch, *, seq):
    sink_ref, q0_ref, q1_ref, kvp_ref, kvc_ref, kvn_ref, z0_ref, z1_ref = ins
    u_ref, = outs
    step = u_ref.shape[0]
    ones = jnp.ones((3 * B_TQ, HEAD_DIM), jnp.bfloat16)
    kv_half = B_KV_HEADS * HEAD_DIM
    units = [(b, kh) for b in range(step // B_TQ) for kh in range(B_KV_HEADS)]

    def scores(b, kh):
        rows = slice(b * B_TQ, (b + 1) * B_TQ)
        lo, hi = (b - 1) * B_TQ, (b + 2) * B_TQ
        q_ref = q0_ref if kh < 2 else q1_ref
        base = (kh % 2) * B_GROUP * HEAD_DIM
        heads = [slice(base + g * HEAD_DIM, base + (g + 1) * HEAD_DIM) for g in range(B_GROUP)]
        k = _window_rows(kvp_ref, kvc_ref, kvn_ref, lo, hi, slice(kh * HEAD_DIM, (kh + 1) * HEAD_DIM))
        q = jnp.concatenate([q_ref[rows, sl] for sl in heads], axis=0)
        return lax.dot_general(q, k, (((1,), (1,)), ((), ())), preferred_element_type=jnp.float32)

    def finish(b, kh, s):
        rows = slice(b * B_TQ, (b + 1) * B_TQ)
        lo, hi = (b - 1) * B_TQ, (b + 2) * B_TQ
        z_ref = z0_ref if kh < 2 else z1_ref
        base = (kh % 2) * B_GROUP * HEAD_DIM
        heads = [slice(base + g * HEAD_DIM, base + (g + 1) * HEAD_DIM) for g in range(B_GROUP)]
        bias = _band_bias(B_TQ, B_TQ, B_RADIUS, ids[1] * step + b * B_TQ, seq)
        v = _window_rows(kvp_ref, kvc_ref, kvn_ref, lo, hi,
                         slice(kv_half + kh * HEAD_DIM, kv_half + (kh + 1) * HEAD_DIM))
        es, ms, sks = [], [], []
        for g in range(B_GROUP):
            sg = s[g * B_TQ:(g + 1) * B_TQ] + bias
            sk = sink_ref[kh * B_GROUP + g] * LOG2E
            m = jnp.maximum(jnp.max(sg, axis=1, keepdims=True), sk)
            es.append(jnp.exp2(sg - m).astype(jnp.bfloat16))
            ms.append(m)
            sks.append(sk)
        ov = jnp.dot(jnp.concatenate(es, axis=0), jnp.concatenate([v, ones], axis=1),
                     preferred_element_type=jnp.float32)
        for g in range(B_GROUP):
            og = ov[g * B_TQ:(g + 1) * B_TQ]
            den = og[:, HEAD_DIM:HEAD_DIM + 1] + jnp.exp2(sks[g] - ms[g])
            col = (kh * B_GROUP + g) * HEAD_DIM
            u_ref[rows, col:col + HEAD_DIM] = (
                og[:, :HEAD_DIM] * (1.0 / den) * z_ref[rows, heads[g]].astype(jnp.float32)
            ).astype(u_ref.dtype)

    per_round = 4
    pending = [scores(*u) for u in units[:per_round]]
    yield
    for r in range(0, len(units), per_round):
        for u, sc in zip(units[r:r + per_round], pending):
            finish(*u, sc)
        pending = [scores(*u) for u in units[r + per_round:r + 2 * per_round]]
        yield


def _attn_b(sink, pb, pz, batch, seq, step):
    nq = seq // step
    nh = seq // B_TQ
    ratio = step // B_TQ
    row = lambda slot: pl.BlockSpec((None, step, TN), lambda b, i: (slot, b * nq + i, 0))
    kv_prev = pl.BlockSpec((None, B_TQ, TN), lambda b, i: (2, b * nh + jnp.maximum(ratio * i - 1, 0), 0))
    kv_next = pl.BlockSpec((None, B_TQ, TN),
                           lambda b, i: (2, b * nh + jnp.minimum(ratio * (i + 1), nh - 1), 0))
    width = B_Q_HEADS * HEAD_DIM
    return dict(
        stages=functools.partial(_attn_b_stages, seq=seq),
        grid=(batch, nq),
        in_specs=[pl.BlockSpec(memory_space=pltpu.SMEM),
                  row(0), row(1), kv_prev, row(2), kv_next, row(1), row(2)],
        args=[sink, pb, pb, pb, pb, pb, pz, pz],
        out_specs=[pl.BlockSpec((step, width), lambda b, i: (b * nq + i, 0))],
        out_shape=[jax.ShapeDtypeStruct((batch * seq, width), jnp.bfloat16)],
        name="attn_b")


def _attn_m_stages(ids, ins, outs, scratch):
    del ids, scratch
    q_ref, k_ref, v_ref, z_ref = ins
    u_ref, = outs
    heads = [slice(hh * M_HEAD_DIM, (hh + 1) * M_HEAD_DIM) for hh in range(M_HEADS)]

    def scores(sl):
        return lax.dot_general(q_ref[:, sl], k_ref[:, sl], (((1,), (1,)), ((), ())),
                               preferred_element_type=jnp.float32)

    def finish(sl, s):
        m = jnp.max(s, axis=1, keepdims=True)
        e = jnp.exp2(s - m)
        den = jnp.sum(e, axis=1, keepdims=True)
        o = jnp.dot(e.astype(jnp.bfloat16), v_ref[:, sl], preferred_element_type=jnp.float32) * (1.0 / den)
        u_ref[:, sl] = (o * z_ref[:, sl].astype(jnp.float32)).astype(u_ref.dtype)

    per_round = 2
    pending = [scores(sl) for sl in heads[:per_round]]
    yield
    for r in range(0, len(heads), per_round):
        for sl, sc in zip(heads[r:r + per_round], pending):
            finish(sl, sc)
        pending = [scores(sl) for sl in heads[r + per_round:r + 2 * per_round]]
        yield


def _attn_m(pb, kv_mem, pz, batch, seq, step):
    nq = seq // step
    width = M_HEADS * M_HEAD_DIM
    return dict(
        stages=_attn_m_stages,
        grid=(batch, nq),
        in_specs=[pl.BlockSpec((None, step, TN), lambda b, i: (3, b * nq + i, 0)),
                  pl.BlockSpec((None, MEM_LEN, TN), lambda b, i: (0, b, 0)),
                  pl.BlockSpec((None, MEM_LEN, TN), lambda b, i: (1, b, 0)),
                  pl.BlockSpec((None, step, TN), lambda b, i: (3, b * nq + i, 0))],
        args=[pb, kv_mem, kv_mem, pz],
        out_specs=[pl.BlockSpec((step, width), lambda b, i: (b * nq + i, 0))],
        out_shape=[jax.ShapeDtypeStruct((batch * seq, width), jnp.bfloat16)],
        name="attn_m")


def _branch_stages(ids, ins, outs, scratch):
    del ids, scratch
    ua_ref, ub_ref, um_ref, w_ref, ga_ref, gb_ref, gm_ref = ins
    u_ref, = outs
    a_hi = A_WIDTH
    b_hi = A_WIDTH + B_Q_HEADS * HEAD_DIM
    dot = functools.partial(jnp.dot, preferred_element_type=jnp.float32)
    acc = ga_ref[...].astype(jnp.float32) * dot(ua_ref[...], w_ref[:a_hi, :])
    yield
    acc += gb_ref[...].astype(jnp.float32) * dot(ub_ref[...], w_ref[a_hi:b_hi, :])
    yield
    acc += gm_ref[...].astype(jnp.float32) * dot(um_ref[...], w_ref[b_hi:, :])
    u_ref[...] = acc.astype(u_ref.dtype)


def _branch(ua, ub, um, w_branch, gates, tm=512):
    m = ua.shape[0]
    k = w_branch.shape[0]
    n_tiles = D_MODEL // TN
    act = lambda width: pl.BlockSpec((tm, width), lambda j, i: (i, 0))
    gate = pl.BlockSpec((None, tm, TN), lambda j, i: (j, i, 0))
    return dict(
        stages=_branch_stages,
        grid=(n_tiles, m // tm),
        in_specs=[act(ua.shape[1]), act(ub.shape[1]), act(um.shape[1]),
                  pl.BlockSpec((k, TN), lambda j, i: (0, j)),
                  gate, gate, gate],
        args=[ua, ub, um, w_branch, *gates],
        out_specs=[pl.BlockSpec((tm, TN), lambda j, i: (i, j))],
        out_shape=[jax.ShapeDtypeStruct((m, D_MODEL), jnp.bfloat16)],
        name="branch_proj")


def _out_kernel(u_ref, w_ref, x_ref, g_ref, y_ref):
    y = x_ref[...] + jnp.dot(u_ref[...], w_ref[...], preferred_element_type=jnp.float32)
    inv = lax.rsqrt(jnp.mean(y * y, axis=1, keepdims=True) + NORM_EPS)
    y_ref[...] = (y * inv) * g_ref[...]


def _out_proj(u, w_out, x, g_final, tm=256):
    m, k = u.shape
    n = x.shape[1]
    return pl.pallas_call(
        _out_kernel,
        grid=(m // tm,),
        in_specs=[pl.BlockSpec((tm, k), lambda i: (i, 0)),
                  pl.BlockSpec((k, n), lambda i: (0, 0), pipeline_mode=pl.Buffered(1)),
                  pl.BlockSpec((tm, n), lambda i: (i, 0)),
                  pl.BlockSpec((1, n), lambda i: (0, 0))],
        out_specs=pl.BlockSpec((tm, n), lambda i: (i, 0)),
        out_shape=jax.ShapeDtypeStruct((m, n), jnp.float32),
        compiler_params=_params(("arbitrary",), vmem=RESIDENT_VMEM_LIMIT),
        name="out_proj",
    )(u, w_out, x, g_final.reshape(1, n))


def _rope_tables(seq):
    inv_freq = ROPE_THETA ** (-jnp.arange(0, HEAD_DIM, 2, dtype=jnp.float32) / HEAD_DIM)
    ang = jnp.arange(seq, dtype=jnp.float32)[:, None] * inv_freq[None, :]
    cos, sin = jnp.cos(ang), jnp.sin(ang)
    cos, sin = jnp.concatenate([cos, cos], axis=1), jnp.concatenate([-sin, sin], axis=1)
    one, zero = jnp.ones_like(cos), jnp.zeros_like(sin)
    q_scale = HEAD_DIM ** -0.5 * LOG2E
    m_scale = M_HEAD_DIM ** -0.5 * LOG2E
    return (jnp.stack([cos * q_scale, cos, one, one * m_scale]),
            jnp.stack([sin * q_scale, sin, zero, zero]))


def kernel(x_prompt, x_sample, mem_prompt, mem_sample, g_norm, w_in, attn_sink, g_mem, w_mem_kv, w_branch, w_out, g_final):
    bf = jnp.bfloat16
    w_in, w_mem_kv, w_branch, w_out = w_in[0], w_mem_kv[0], w_branch[0], w_out[0]
    tabs = _rope_tables(max(x_prompt.shape[1], x_sample.shape[1]))
    trunks = []
    for x, mem in ((x_prompt, mem_prompt), (x_sample, mem_sample)):
        batch, seq, d = x.shape
        trunks.append(dict(batch=batch, seq=seq, rows=batch * seq, x2=x.reshape(batch * seq, d),
                           hm=_rmsnorm(mem.reshape(batch * MEM_LEN, d), g_mem[0], bf)))
    p, s_ = trunks

    w_b = _cast_tiles(w_in, B_TILES)
    b_sets = ((ROT_Q, ROT_Q, ROT_K, ROT_M), (ROT_Q, ROT_Q, ROT_ID, ROT_M))
    a_sets = ((ROT_Q, ROT_K, ROT_ID),) * 2
    dils = [dil for _, dil in DIL_CONFIGS]
    g_lo = G_TILES[0]
    a_cols = lambda gi: (w_in, (gi, A_TILES[1], len(dils)))
    in_cols = lambda lo: (w_in, (lo, lo + 4))

    s_["h"], pb_s, w_a0 = _norm_proj(s_["x2"], g_norm[0], w_b, tabs, b_sets, batch=s_["batch"], cast=a_cols(0))
    p["h"], pb_p = _norm_proj(p["x2"], g_norm[0], w_b, tabs, b_sets, batch=p["batch"])

    def proj_a(t, gi, wts, cast):
        out = _proj(t["h"], wts, (0, 3), "rot", batch=t["batch"], tm=1024, dil=dils[gi], tabs=tabs,
                    sets=a_sets, cast=cast, name=f"proj_a{gi}")
        qkv, w_next = out if cast is not None else (out, None)
        n, sd = t["batch"] * dils[gi], t["seq"] // dils[gi]
        return _attn_a(qkv.reshape(3 * n, sd, A_WIDTH), n, sd, GUEST_STEP), w_next

    a0_p, w_a1 = proj_a(p, 0, w_a0, a_cols(1))
    a0_s, w_a2 = proj_a(s_, 0, w_a0, a_cols(2))
    a1_p, w_m = proj_a(p, 1, w_a1, (w_mem_kv, (0, 2)))
    a1_s, w_z = proj_a(s_, 1, w_a1, (w_in, Z_TILES))
    a2_p, _ = proj_a(p, 2, w_a2, None)
    a2_s, w_ga = proj_a(s_, 2, w_a2, in_cols(g_lo))

    gate = lambda t, wts, epi, cast, name, *guests: _gate_proj(t["h"], wts, epi, cast=cast, guests=guests, name=name)
    attn_b = lambda t, pb, pz: _attn_b(attn_sink[0], pb, pz, t["batch"], t["seq"], GUEST_STEP)

    def attn_m(t, pb, pz):
        kv_mem = _proj(t["hm"], w_m, (0, 2), "none", batch=1, tm=t["batch"] * MEM_LEN, name="proj_mem")
        return _attn_m(pb, kv_mem.reshape(2, t["batch"] * MEM_LEN, TN), pz, t["batch"], t["seq"], GUEST_STEP)

    pz_p, w_gb, g0_p = gate(p, w_z, "silu", in_cols(g_lo + 4), "gate_z", a0_p)
    pz_s, w_gm, g0_s = gate(s_, w_z, "silu", in_cols(g_lo + 8), "gate_z", a0_s)
    ga_p, w_o, g1_p = gate(p, w_ga, "sigmoid", (w_out, (0, -(-w_out.shape[1] // TN))), "gate_a", a1_p)
    ga_s, w_br, g1_s = gate(s_, w_ga, "sigmoid", (w_branch, (0, D_MODEL // TN)), "gate_a", a1_s)
    gb_p, _, g2_p, (um_p,) = gate(p, w_gb, "sigmoid", None, "gate_b", a2_p, attn_m(p, pb_p, pz_p))
    gb_s, _, g2_s, (um_s,) = gate(s_, w_gb, "sigmoid", None, "gate_b", a2_s, attn_m(s_, pb_s, pz_s))
    gm_p, _, (ub_p,) = gate(p, w_gm, "sigmoid", None, "gate_m", attn_b(p, pb_p, pz_p))
    gm_s, _, (ub_s,) = gate(s_, w_gm, "sigmoid", None, "gate_m", attn_b(s_, pb_s, pz_s))

    def group_outputs(t, groups):
        shape = lambda dil, width: (t["batch"], dil, t["seq"] // dil, width)
        return ([o.reshape(shape(dil, A_WIDTH)) for (o, _), dil in zip(groups, dils)],
                [l.reshape(shape(dil, LSE_LANES)) for (_, l), dil in zip(groups, dils)])

    ao_p, al_p = group_outputs(p, (g0_p, g1_p, g2_p))
    ao_s, al_s = group_outputs(s_, (g0_s, g1_s, g2_s))
    ua_p, = _launch(_merge_a(ao_p, al_p, pz_p, p["batch"], p["seq"]))
    branch_p = _branch(ua_p, ub_p, um_p, w_br, (ga_p, gb_p, gm_p))
    steps = branch_p["grid"][0] * branch_p["grid"][1]
    (u_p,), (ua_s,) = _launch(branch_p, _merge_a(ao_s, al_s, pz_s, s_["batch"], s_["seq"], tm=s_["rows"] // steps))
    u_s, = _launch(_branch(ua_s, ub_s, um_s, w_br, (ga_s, gb_s, gm_s)))
    finish = lambda t, u: _out_proj(u, w_o, t["x2"], g_final).reshape(t["batch"], t["seq"], t["x2"].shape[1])
    return finish(p, u_p), finish(s_, u_s)
```
